```python
import math
import jax, jax.numpy as jnp
from jax import lax
import numpy as np

D_MODEL = 1024
BATCH = 8
SEQ = 4096
DEPTH = 1

N_HEADS = 8
N_KV_HEADS = 2
HEAD_DIM = 64
GQA_GROUP = N_HEADS // N_KV_HEADS
ATTN_WIDTH = N_HEADS * HEAD_DIM
KV_WIDTH = N_KV_HEADS * HEAD_DIM
ROPE_DIM = HEAD_DIM // 4
ROPE_THETA = 500000.0
WINDOW = 128
BLOCK = 128
SSM_GROUP_CH = 16
SSM_WIDTH = D_MODEL // 2
SSM_GROUPS = SSM_WIDTH // SSM_GROUP_CH
SSM_STATE = 64
D_FF = -(-8 * D_MODEL // (3 * 256)) * 256
IN_COLS = ATTN_WIDTH + 2 * KV_WIDTH + SSM_WIDTH + 2 * D_MODEL
SPLITS = [ATTN_WIDTH, ATTN_WIDTH + KV_WIDTH, ATTN_WIDTH + 2 * KV_WIDTH,
          ATTN_WIDTH + 2 * KV_WIDTH + SSM_WIDTH,
          ATTN_WIDTH + 2 * KV_WIDTH + SSM_WIDTH + D_MODEL]
RMS_EPS = 1e-6
NEG_INF = -1e30

kernel_name = "hybrid_gated_swa_s5_encoder"


def rmsnorm(x, g):
    xf = x.astype(jnp.float32)
    xf = xf * lax.rsqrt(jnp.mean(xf * xf, axis=-1, keepdims=True) + RMS_EPS)
    return (xf * g.astype(jnp.float32)).astype(x.dtype)


def rope_tables(seq_len):
    pos = jnp.arange(seq_len, dtype=jnp.float32)
    inv_freq = ROPE_THETA ** (-jnp.arange(0, ROPE_DIM, 2, dtype=jnp.float32) / ROPE_DIM)
    ang = pos[:, None] * inv_freq[None, :]
    return jnp.cos(ang)[:, None, :], jnp.sin(ang)[:, None, :]


def partial_rope(t, cos, sin):
    t = t.astype(jnp.float32)
    r, rest = t[..., :ROPE_DIM], t[..., ROPE_DIM:]
    r1, r2 = r[..., :ROPE_DIM // 2], r[..., ROPE_DIM // 2:]
    rot = jnp.concatenate([r1 * cos - r2 * sin, r2 * cos + r1 * sin], axis=-1)
    return jnp.concatenate([rot, rest], axis=-1)


def windowed_gqa(q, k, v, sink):
    b, L = q.shape[0], q.shape[1]
    nb = L // BLOCK
    qb = q.reshape(b, nb, BLOCK, N_KV_HEADS, GQA_GROUP, HEAD_DIM)
    pad = ((0, 0), (1, 1), (0, 0), (0, 0), (0, 0))
    kp = jnp.pad(k.reshape(b, nb, BLOCK, N_KV_HEADS, HEAD_DIM), pad)
    vp = jnp.pad(v.astype(jnp.float32).reshape(b, nb, BLOCK, N_KV_HEADS, HEAD_DIM), pad)
    kw = jnp.concatenate([kp[:, :-2], kp[:, 1:-1], kp[:, 2:]], axis=2)
    vw = jnp.concatenate([vp[:, :-2], vp[:, 1:-1], vp[:, 2:]], axis=2)
    scores = jnp.einsum('bnqkgd,bnskd->bnkgqs', qb, kw) * (HEAD_DIM ** -0.5)
    blk = jnp.arange(nb)[:, None, None]
    qpos = blk * BLOCK + jnp.arange(BLOCK)[None, :, None]
    kpos = (blk - 1) * BLOCK + jnp.arange(3 * BLOCK)[None, None, :]
    valid = (jnp.abs(qpos - kpos) <= WINDOW) & (kpos >= 0) & (kpos < L)
    scores = jnp.where(valid[None, :, None, None], scores, NEG_INF)
    s = sink.astype(jnp.float32).reshape(N_KV_HEADS, GQA_GROUP)[None, None, :, :, None, None]
    m = jnp.maximum(jnp.max(scores, axis=-1, keepdims=True), s)
    p = jnp.exp(scores - m)
    p = p / (jnp.sum(p, axis=-1, keepdims=True) + jnp.exp(s - m))
    out = jnp.einsum('bnkgqs,bnskd->bnqkgd', p, vw)
    return out.reshape(b, L, ATTN_WIDTH)


def s5_scan(u, lam_re, lam_im, log_dt, b_re, b_im):
    L = u.shape[1]
    lam = lax.complex(lam_re.astype(jnp.float32), lam_im.astype(jnp.float32))
    dt = jnp.exp(log_dt.astype(jnp.float32))[:, None]
    lam_bar = jnp.exp(lam * dt)
    b_bar = ((lam_bar - 1.0) / lam)[..., None] * lax.complex(b_re.astype(jnp.float32),
                                                             b_im.astype(jnp.float32))
    bu = jnp.einsum('blgh,gph->blgp', u.astype(jnp.complex64), b_bar)
    a = jnp.broadcast_to(lam_bar, (1, L) + lam_bar.shape)

    def combine(c1, c2):
        a1, x1 = c1
        a2, x2 = c2
        return a1 * a2, a2 * x1 + x2

    _, h = lax.associative_scan(combine, (a, bu), axis=1)
    return h


def bidirectional_s5(u, lam_re, lam_im, log_dt, b_re, b_im, c_re, c_im, d, w_glu):
    b, L = u.shape[0], u.shape[1]
    ug = u.astype(jnp.float32).reshape(b, L, SSM_GROUPS, SSM_GROUP_CH)
    h_f = s5_scan(ug, lam_re[0], lam_im[0], log_dt[0], b_re[0], b_im[0])
    h_b = s5_scan(ug[:, ::-1], lam_re[1], lam_im[1], log_dt[1], b_re[1], b_im[1])[:, ::-1]
    c = lax.complex(c_re.astype(jnp.float32), c_im.astype(jnp.float32))
    y = jnp.einsum('blgp,ghp->blgh', h_f + h_b, c).real + d.astype(jnp.float32) * ug
    y = jax.nn.gelu(y.reshape(b, L, SSM_WIDTH)).astype(u.dtype)
    return y * jax.nn.sigmoid(y @ w_glu)


def setup_inputs(seed: int = 0) -> dict:
    key = jax.random.key(seed)
    ks = jax.random.split(key, 24)
    nrm = lambda k, shape, scale: jax.random.normal(k, shape, jnp.float32) * scale
    Ls, G, P, H = DEPTH, SSM_GROUPS, SSM_STATE, SSM_GROUP_CH
    lam_im_init = jnp.pi * jnp.arange(P, dtype=jnp.float32)
    return {
        "x": jax.random.normal(ks[0], (BATCH, SEQ, D_MODEL), jnp.float32),
        "norm1_g": 1.0 + nrm(ks[1], (Ls, D_MODEL), 0.02),
        "w_in": nrm(ks[2], (Ls, D_MODEL, IN_COLS), D_MODEL ** -0.5),
        "attn_sink": nrm(ks[3], (Ls, N_HEADS), 0.5),
        "ssm_lambda_re": -0.5 + nrm(ks[4], (Ls, 2, G, P), 0.01),
        "ssm_lambda_im": lam_im_init + nrm(ks[5], (Ls, 2, G, P), 0.01),
        "ssm_log_dt": jax.random.uniform(ks[6], (Ls, 2, G), jnp.float32,
                                         math.log(1e-3), math.log(1e-1)),
        "ssm_b_re": nrm(ks[7], (Ls, 2, G, P, H), (2.0 * H) ** -0.5),
        "ssm_b_im": nrm(ks[8], (Ls, 2, G, P, H), (2.0 * H) ** -0.5),
        "ssm_c_re": nrm(ks[9], (Ls, G, H, P), (2.0 * P) ** -0.5),
        "ssm_c_im": nrm(ks[10], (Ls, G, H, P), (2.0 * P) ** -0.5),
        "ssm_d": nrm(ks[11], (Ls, G, H), 1.0),
        "w_glu": nrm(ks[12], (Ls, SSM_WIDTH, SSM_WIDTH), SSM_WIDTH ** -0.5),
        "w_attn_branch": nrm(ks[13], (Ls, ATTN_WIDTH, D_MODEL), ATTN_WIDTH ** -0.5),
        "w_ssm_branch": nrm(ks[14], (Ls, SSM_WIDTH, D_MODEL), SSM_WIDTH ** -0.5),
        "w_out": nrm(ks[15], (Ls, D_MODEL, D_MODEL), D_MODEL ** -0.5),
        "norm2_g": 1.0 + nrm(ks[16], (Ls, D_MODEL), 0.02),
        "w_ffn_gate": nrm(ks[17], (Ls, D_MODEL, D_FF), D_MODEL ** -0.5),
        "w_ffn_up": nrm(ks[18], (Ls, D_MODEL, D_FF), D_MODEL ** -0.5),
        "w_ffn_down": nrm(ks[19], (Ls, D_FF, D_MODEL), D_FF ** -0.5),
        "norm_f_g": 1.0 + nrm(ks[20], (D_MODEL,), 0.02),
    }


def reference(x, norm1_g, w_in, attn_sink, ssm_lambda_re, ssm_lambda_im, ssm_log_dt,
              ssm_b_re, ssm_b_im, ssm_c_re, ssm_c_im, ssm_d, w_glu, w_attn_branch,
              w_ssm_branch, w_out, norm2_g, w_ffn_gate, w_ffn_up, w_ffn_down, norm_f_g):
    b, L, _ = x.shape
    cos, sin = rope_tables(L)
    for layer in range(DEPTH):
        h = rmsnorm(x, norm1_g[layer])
        proj = h @ w_in[layer]
        q, k, v, u, g_attn, g_ssm = jnp.split(proj, SPLITS, axis=-1)
        q = partial_rope(q.reshape(b, L, N_HEADS, HEAD_DIM), cos, sin)
        k = partial_rope(k.reshape(b, L, N_KV_HEADS, HEAD_DIM), cos, sin)
        v = v.reshape(b, L, N_KV_HEADS, HEAD_DIM)
        attn = windowed_gqa(q, k, v, attn_sink[layer]).astype(x.dtype)
        ssm = bidirectional_s5(u, ssm_lambda_re[layer], ssm_lambda_im[layer], ssm_log_dt[layer],
                               ssm_b_re[layer], ssm_b_im[layer], ssm_c_re[layer],
                               ssm_c_im[layer], ssm_d[layer], w_glu[layer])
        merged = (jax.nn.sigmoid(g_attn) * (attn @ w_attn_branch[layer])
                  + jax.nn.sigmoid(g_ssm) * (ssm @ w_ssm_branch[layer]))
        x = x + merged @ w_out[layer]
        h2 = rmsnorm(x, norm2_g[layer])
        x = x + (jax.nn.silu(h2 @ w_ffn_gate[layer]) * (h2 @ w_ffn_up[layer])) @ w_ffn_down[layer]
    return rmsnorm(x, norm_f_g)
```

```python
import math

import jax
import jax.numpy as jnp
from jax import lax
from jax.experimental import pallas as pl
from jax.experimental.pallas import tpu as pltpu

D_MODEL = 1024
BATCH = 8
SEQ = 4096
N_HEADS = 8
N_KV_HEADS = 2
HEAD_DIM = 64
GQA_GROUP = N_HEADS // N_KV_HEADS
ATTN_WIDTH = N_HEADS * HEAD_DIM
KV_WIDTH = N_KV_HEADS * HEAD_DIM
ROPE_DIM = HEAD_DIM // 4
ROPE_HALF = ROPE_DIM // 2
ROPE_THETA = 500000.0
WINDOW = 128
SSM_GROUP_CH = 16
SSM_WIDTH = D_MODEL // 2
SSM_GROUPS = SSM_WIDTH // SSM_GROUP_CH
SSM_STATE = 64
D_FF = 2816
IN_COLS = ATTN_WIDTH + 2 * KV_WIDTH + SSM_WIDTH + 2 * D_MODEL
RMS_EPS = 1e-6
NEG_INF = -1e30

LANES = 128
SUBLANES = 8
CHUNK = LANES
N_CHUNKS = SEQ // CHUNK
ROWS = N_CHUNKS * BATCH
FLAT = SSM_GROUP_CH * CHUNK
FFN_ROWS = 512
FFN_COLS = 256
VMEM_LIMIT = 58 * 1024 * 1024

assert BATCH == SUBLANES and WINDOW == CHUNK and D_FF % FFN_COLS == 0

_Q0, _K0, _V0, _U0, _GA0, _GS0 = 0, 512, 640, 768, 1280, 2304

BF16 = jnp.bfloat16
F32 = jnp.float32


def _sigmoid(x):
    return 1.0 / (1.0 + jnp.exp(-x))


def _rmsnorm(x, g):
    return x * lax.rsqrt(jnp.mean(x * x, axis=-1, keepdims=True) + RMS_EPS) * g


def _dot(a, b):
    return jnp.dot(a, b, preferred_element_type=F32)


def _lane_cat(ref, n):
    return jnp.concatenate([ref[i] for i in range(n)], axis=1)


def _in_proj_kernel(x_ref, g_ref, w_ref, cos_ref, sin_ref,
                    q_ref, k_ref, v_ref, u_ref, ga_ref, gs_ref):
    x = x_ref[...].reshape(BATCH * CHUNK, D_MODEL)
    h = _rmsnorm(x, g_ref[...]).astype(BF16)

    def proj_t(r0, r1):
        return lax.dot_general(w_ref[r0:r1, :], h, (((1,), (1,)), ((), ())),
                               preferred_element_type=F32)

    cos = jnp.concatenate([cos_ref[...]] * BATCH, axis=1)
    sin = jnp.concatenate([sin_ref[...]] * BATCH, axis=1)

    def rope_t(t, n_heads):
        parts = []
        for hd in range(n_heads):
            r1 = t[hd * HEAD_DIM:hd * HEAD_DIM + ROPE_HALF]
            r2 = t[hd * HEAD_DIM + ROPE_HALF:hd * HEAD_DIM + ROPE_DIM]
            parts += [r1 * cos - r2 * sin, r2 * cos + r1 * sin,
                      t[hd * HEAD_DIM + ROPE_DIM:(hd + 1) * HEAD_DIM]]
        return jnp.concatenate(parts, axis=0)

    def put(ref, t):
        for b in range(BATCH):
            ref[b] = t[:, b * CHUNK:(b + 1) * CHUNK].astype(ref.dtype)

    put(q_ref, rope_t(proj_t(_Q0, _K0) * (HEAD_DIM ** -0.5), N_HEADS))
    put(k_ref, rope_t(proj_t(_K0, _V0), N_KV_HEADS))
    put(v_ref, proj_t(_V0, _U0))
    u_t = proj_t(_U0, _GA0)
    for b in range(BATCH):
        u_ref[0, pl.ds(b, SSM_WIDTH, stride=BATCH), :] = u_t[:, b * CHUNK:(b + 1) * CHUNK]
    put(ga_ref, proj_t(_GA0, _GS0))
    put(gs_ref, proj_t(_GS0, IN_COLS))


def _in_proj(x, g1, w_t, cos_t, sin_t):
    tok = lambda rows, dt: jax.ShapeDtypeStruct((BATCH, rows, SEQ), dt)
    tspec = lambda rows: pl.BlockSpec((BATCH, rows, CHUNK), lambda c: (0, 0, c))
    return pl.pallas_call(
        _in_proj_kernel,
        grid=(N_CHUNKS,),
        in_specs=[
            pl.BlockSpec((BATCH, CHUNK, D_MODEL), lambda c: (0, c, 0)),
            pl.BlockSpec((1, D_MODEL), lambda c: (0, 0)),
            pl.BlockSpec((IN_COLS, D_MODEL), lambda c: (0, 0), pipeline_mode=pl.Buffered(1)),
            pl.BlockSpec((ROPE_HALF, CHUNK), lambda c: (0, c)),
            pl.BlockSpec((ROPE_HALF, CHUNK), lambda c: (0, c)),
        ],
        out_specs=[
            tspec(ATTN_WIDTH), tspec(KV_WIDTH), tspec(KV_WIDTH),
            pl.BlockSpec((1, SSM_WIDTH * BATCH, CHUNK), lambda c: (c, 0, 0)),
            tspec(D_MODEL), tspec(D_MODEL),
        ],
        out_shape=[
            tok(ATTN_WIDTH, BF16), tok(KV_WIDTH, BF16), tok(KV_WIDTH, BF16),
            jax.ShapeDtypeStruct((N_CHUNKS, SSM_WIDTH * BATCH, CHUNK), F32),
            tok(D_MODEL, F32), tok(D_MODEL, F32),
        ],
        compiler_params=pltpu.CompilerParams(vmem_limit_bytes=VMEM_LIMIT),
        name="in_proj",
    )(x, g1, w_t, cos_t, sin_t)


def _attn_kernel(q_ref, kp_ref, kc_ref, kn_ref, vp_ref, vc_ref, vn_ref, sink_ref, o_ref):
    n = pl.program_id(1)
    k_t = jnp.concatenate([kp_ref[0], kc_ref[0], kn_ref[0]], axis=1)
    v_t = jnp.concatenate([vp_ref[0], vc_ref[0], vn_ref[0]], axis=1)
    keys = k_t.astype(F32).T.astype(BF16)
    kk = lax.broadcasted_iota(jnp.int32, (3 * CHUNK, CHUNK), 0)
    qq = lax.broadcasted_iota(jnp.int32, (3 * CHUNK, CHUNK), 1)
    kpos = (n - 1) * CHUNK + kk
    rel = kk - qq
    valid = (rel >= 0) & (rel <= 2 * WINDOW) & (kpos >= 0) & (kpos < SEQ)
    valid = jnp.concatenate([valid] * GQA_GROUP, axis=1)
    zeros = jnp.zeros((HEAD_DIM, GQA_GROUP * CHUNK), BF16)
    for j in range(N_KV_HEADS):
        q_t = jnp.concatenate(
            [q_ref[0, (j * GQA_GROUP + g) * HEAD_DIM:(j * GQA_GROUP + g + 1) * HEAD_DIM, :]
             for g in range(GQA_GROUP)], axis=1)
        q_pad = jnp.concatenate([q_t, zeros] if j == 0 else [zeros, q_t], axis=0)
        s = jnp.where(valid, _dot(keys, q_pad), NEG_INF)
        sink = sink_ref[j]
        m = jnp.maximum(jnp.max(s, axis=0, keepdims=True), sink)
        p = jnp.exp(s - m)
        denom = jnp.sum(p, axis=0, keepdims=True) + jnp.exp(sink - m)
        o = _dot(v_t[j * HEAD_DIM:(j + 1) * HEAD_DIM, :], p.astype(BF16)) / denom
        for g in range(GQA_GROUP):
            hd = j * GQA_GROUP + g
            o_ref[0, hd * HEAD_DIM:(hd + 1) * HEAD_DIM, :] = (
                o[:, g * CHUNK:(g + 1) * CHUNK].astype(o_ref.dtype))


def _attention(q_t, k_t, v_t, sink_rows):
    prev = lambda b, n: (b, 0, jnp.maximum(n - 1, 0))
    cur = lambda b, n: (b, 0, n)
    nxt = lambda b, n: (b, 0, jnp.minimum(n + 1, N_CHUNKS - 1))
    kv = lambda im: pl.BlockSpec((1, KV_WIDTH, CHUNK), im)
    return pl.pallas_call(
        _attn_kernel,
        grid=(BATCH, N_CHUNKS),
        in_specs=[pl.BlockSpec((1, ATTN_WIDTH, CHUNK), cur),
                  kv(prev), kv(cur), kv(nxt), kv(prev), kv(cur), kv(nxt),
                  pl.BlockSpec((N_KV_HEADS, 1, GQA_GROUP * CHUNK), lambda b, n: (0, 0, 0))],
        out_specs=pl.BlockSpec((1, ATTN_WIDTH, CHUNK), cur),
        out_shape=jax.ShapeDtypeStruct((BATCH, ATTN_WIDTH, SEQ), BF16),
        compiler_params=pltpu.CompilerParams(vmem_limit_bytes=VMEM_LIMIT),
        name="attention",
    )(q_t, k_t, k_t, k_t, v_t, v_t, v_t, sink_rows)


def _ssm_kernel(u_ref, rowp_ref, colp_ref, bt_ref, cw_ref, ct_ref, drow_ref,
                y_ref, ktab_ref, tz_ref, in_ref, out_ref):
    half = SSM_STATE
    lo_row = lax.broadcasted_iota(jnp.int32, (1, LANES), 1) < half
    top_col = lax.broadcasted_iota(jnp.int32, (LANES, 1), 0) < half

    lam_r, lam_i = rowp_ref[0, 0:1, :], rowp_ref[0, 1:2, :]
    dt = jnp.exp(rowp_ref[0, 2:3, :])
    z_r, z_i = lam_r * dt, lam_i * dt
    e1 = jnp.exp(z_r)
    lb_r, lb_i = e1 * jnp.cos(z_i), e1 * jnp.sin(z_i)
    den = lam_r * lam_r + lam_i * lam_i
    cf_r = ((lb_r - 1.0) * lam_r + lb_i * lam_i) / den
    cf_i = (lb_i * lam_r - (lb_r - 1.0) * lam_i) / den
    b_r, b_i = bt_ref[0, 0:SSM_GROUP_CH, :], bt_ref[0, SSM_GROUP_CH:, :]
    bb_r = cf_r * b_r - cf_i * b_i
    bb_i = cf_r * b_i + cf_i * b_r
    c_r, c_i = cw_ref[0, 0:SSM_GROUP_CH, :], cw_ref[0, SSM_GROUP_CH:, :]

    z_rc = colp_ref[0, :, 0:1] * jnp.exp(colp_ref[0, :, 2:3])
    z_ic = colp_ref[0, :, 1:2] * jnp.exp(colp_ref[0, :, 2:3])

    m_idx = lax.broadcasted_iota(jnp.int32, (LANES, 2 * CHUNK), 1)
    delta = jnp.where(top_col, m_idx - CHUNK, CHUNK - m_idx)
    ok = (delta >= 0) & (delta < CHUNK)
    pw = jnp.where(ok, delta, 0).astype(F32)
    mag = jnp.exp(pw * z_rc)
    e_r = jnp.where(ok, mag * jnp.cos(pw * z_ic), 0.0)
    e_i = jnp.where(ok, mag * jnp.sin(pw * z_ic), 0.0)
    cb_r = jnp.concatenate([c_r * bb_r[h:h + 1] - c_i * bb_i[h:h + 1]
                            for h in range(SSM_GROUP_CH)], axis=0)
    cb_i = jnp.concatenate([c_r * bb_i[h:h + 1] + c_i * bb_r[h:h + 1]
                            for h in range(SSM_GROUP_CH)], axis=0)
    hi = lax.Precision.HIGHEST
    ktab_ref[...] = (jnp.dot(cb_r, e_r, precision=hi, preferred_element_type=F32)
                     - jnp.dot(cb_i, e_i, precision=hi, preferred_element_type=F32))

    for h in range(SSM_GROUP_CH):
        for hp in range(SSM_GROUP_CH):
            row = ktab_ref[pl.ds(h * SSM_GROUP_CH + hp, 1), :]
            skew = pltpu.roll(jnp.broadcast_to(row, (CHUNK, 2 * CHUNK)), 0, 1,
                              stride=1, stride_axis=0)
            tz_ref[h * CHUNK:(h + 1) * CHUNK, hp * CHUNK:(hp + 1) * CHUNK] = (
                skew[:, CHUNK:].astype(BF16))

    s_idx = lax.broadcasted_iota(jnp.int32, (CHUNK, LANES), 0)
    pw = jnp.where(lo_row, CHUNK - 1 - s_idx, s_idx).astype(F32)
    mag = jnp.exp(pw * z_r)
    p_r, p_i = mag * jnp.cos(pw * z_i), mag * jnp.sin(pw * z_i)
    for h in range(SSM_GROUP_CH):
        rows = slice(h * CHUNK, (h + 1) * CHUNK)
        in_ref[rows, 0:LANES] = (p_r * bb_r[h:h + 1] - p_i * bb_i[h:h + 1]).astype(BF16)
        in_ref[rows, LANES:] = (p_r * bb_i[h:h + 1] + p_i * bb_r[h:h + 1]).astype(BF16)

    t_idx = lax.broadcasted_iota(jnp.int32, (LANES, CHUNK), 1)
    pw = jnp.where(top_col, t_idx + 1, CHUNK - t_idx).astype(F32)
    mag = jnp.exp(pw * z_rc)
    q_r, q_i = mag * jnp.cos(pw * z_ic), mag * jnp.sin(pw * z_ic)
    for hp in range(SSM_GROUP_CH):
        cols = slice(hp * CHUNK, (hp + 1) * CHUNK)
        cc_r = ct_ref[0, :, hp:hp + 1]
        cc_i = ct_ref[0, :, SSM_GROUP_CH + hp:SSM_GROUP_CH + hp + 1]
        out_ref[0:LANES, cols] = (cc_r * q_r - cc_i * q_i).astype(BF16)
        out_ref[LANES:, cols] = (-(cc_r * q_i + cc_i * q_r)).astype(BF16)

    a = jnp.concatenate(
        [u_ref[:, h * BATCH:(h + 1) * BATCH, :].reshape(ROWS, CHUNK)
         for h in range(SSM_GROUP_CH)], axis=1)
    ab = a.astype(BF16)
    e = _dot(ab, in_ref[...])
    mag = jnp.exp(float(CHUNK) * z_r)
    lc_r, lc_i = mag * jnp.cos(float(CHUNK) * z_i), mag * jnp.sin(float(CHUNK) * z_i)
    s_r = jnp.zeros((BATCH, LANES), F32)
    s_i = jnp.zeros((BATCH, LANES), F32)
    hist_r, hist_i = [], []
    for i in range(N_CHUNKS):
        hist_r.append(s_r)
        hist_i.append(s_i)
        fr = slice(i * BATCH, (i + 1) * BATCH)
        br = slice((N_CHUNKS - 1 - i) * BATCH, (N_CHUNKS - i) * BATCH)
        in_r = jnp.where(lo_row, e[fr, 0:LANES], e[br, 0:LANES])
        in_i = jnp.where(lo_row, e[fr, LANES:], e[br, LANES:])
        s_r, s_i = (s_r * lc_r - s_i * lc_i + in_r, s_r * lc_i + s_i * lc_r + in_i)
    states = jnp.concatenate(
        [jnp.concatenate([jnp.where(lo_row, hist_r[c], hist_r[N_CHUNKS - 1 - c]),
                          jnp.where(lo_row, hist_i[c], hist_i[N_CHUNKS - 1 - c])], axis=1)
         for c in range(N_CHUNKS)], axis=0)
    y = (_dot(ab, tz_ref[...]) + _dot(states.astype(BF16), out_ref[...])
         + a * drow_ref[0])
    for hp in range(SSM_GROUP_CH):
        y_ref[:, hp * BATCH:(hp + 1) * BATCH, :] = (
            y[:, hp * CHUNK:(hp + 1) * CHUNK].reshape(N_CHUNKS, BATCH, CHUNK))


def _ssm(u4, rowp, colp, bt, cw, ct, drow):
    grp = lambda shape: pl.BlockSpec((1,) + shape, lambda g: (g, 0, 0))
    data = pl.BlockSpec((N_CHUNKS, SSM_GROUP_CH * BATCH, CHUNK), lambda g: (0, g, 0))
    return pl.pallas_call(
        _ssm_kernel,
        grid=(SSM_GROUPS,),
        in_specs=[data, grp((SUBLANES, LANES)), grp((LANES, SUBLANES)),
                  grp((2 * SSM_GROUP_CH, LANES)), grp((2 * SSM_GROUP_CH, LANES)),
                  grp((LANES, 2 * SSM_GROUP_CH)), grp((1, FLAT))],
        out_specs=data,
        out_shape=jax.ShapeDtypeStruct((N_CHUNKS, SSM_WIDTH * BATCH, CHUNK), F32),
        scratch_shapes=[pltpu.VMEM((SSM_GROUP_CH * SSM_GROUP_CH, 2 * CHUNK), F32),
                        pltpu.VMEM((FLAT, FLAT), BF16),
                        pltpu.VMEM((FLAT, 2 * LANES), BF16),
                        pltpu.VMEM((2 * LANES, FLAT), BF16)],
        compiler_params=pltpu.CompilerParams(vmem_limit_bytes=VMEM_LIMIT),
        name="ssm",
    )(u4, rowp, colp, bt, cw, ct, drow)


MERGE_B = BATCH // 2


def _merge_kernel(y_ref, at_ref, ga_ref, gs_ref, x_ref,
                  wglu_ref, wab_ref, wsb_ref, wout_ref, o_ref):
    b0 = pl.program_id(1) * MERGE_B
    y_t = jnp.concatenate(
        [y_ref[0, pl.ds(b0 + b, SSM_WIDTH, stride=BATCH), :] for b in range(MERGE_B)],
        axis=1)
    yg = 0.5 * y_t * (1.0 + jnp.tanh(math.sqrt(2.0 / math.pi) * (y_t + 0.044715 * y_t ** 3)))
    ssm_t = yg * _sigmoid(_dot(wglu_ref[...], yg.astype(BF16)))
    a_t = _dot(wab_ref[...], _lane_cat(at_ref, MERGE_B))
    s_t = _dot(wsb_ref[...], ssm_t.astype(BF16))
    merged = (_sigmoid(_lane_cat(ga_ref, MERGE_B)) * a_t
              + _sigmoid(_lane_cat(gs_ref, MERGE_B)) * s_t)
    out = _dot(wout_ref[...], merged.astype(BF16)).T
    o_ref[...] = x_ref[...] + out.reshape(MERGE_B, CHUNK, D_MODEL)


def _merge(y4, attn_t, ga_t, gs_t, x, wglu_t, wab_t, wsb_t, wout_t):
    tspec = lambda rows: pl.BlockSpec((MERGE_B, rows, CHUNK), lambda c, hb: (hb, 0, c))
    wspec = lambda r, k: pl.BlockSpec((r, k), lambda c, hb: (0, 0), pipeline_mode=pl.Buffered(1))
    xspec = pl.BlockSpec((MERGE_B, CHUNK, D_MODEL), lambda c, hb: (hb, c, 0))
    return pl.pallas_call(
        _merge_kernel,
        grid=(N_CHUNKS, BATCH // MERGE_B),
        in_specs=[pl.BlockSpec((1, SSM_WIDTH * BATCH, CHUNK), lambda c, hb: (c, 0, 0)),
                  tspec(ATTN_WIDTH), tspec(D_MODEL), tspec(D_MODEL), xspec,
                  wspec(SSM_WIDTH, SSM_WIDTH), wspec(D_MODEL, ATTN_WIDTH),
                  wspec(D_MODEL, SSM_WIDTH), wspec(D_MODEL, D_MODEL)],
        out_specs=xspec,
        out_shape=jax.ShapeDtypeStruct((BATCH, SEQ, D_MODEL), F32),
        compiler_params=pltpu.CompilerParams(vmem_limit_bytes=VMEM_LIMIT),
        name="merge",
    )(y4, attn_t, ga_t, gs_t, x, wglu_t, wab_t, wsb_t, wout_t)


def _ffn_kernel(x_ref, g2_ref, wg_ref, wu_ref, wd_ref, gf_ref, o_ref):
    x = x_ref[...]
    h = _rmsnorm(x, g2_ref[...]).astype(BF16)
    acc = jnp.zeros((FFN_ROWS, D_MODEL), F32)
    for j in range(D_FF // FFN_COLS):
        cols = slice(j * FFN_COLS, (j + 1) * FFN_COLS)
        gate = _dot(h, wg_ref[:, cols])
        up = _dot(h, wu_ref[:, cols])
        act = (gate * _sigmoid(gate) * up).astype(BF16)
        acc = acc + _dot(act, wd_ref[cols, :])
    o_ref[...] = _rmsnorm(x + acc, gf_ref[...])


def _ffn(x, g2, wg, wu, wd, gf):
    n_tok = BATCH * SEQ
    row = pl.BlockSpec((FFN_ROWS, D_MODEL), lambda i: (i, 0))
    vec = pl.BlockSpec((1, D_MODEL), lambda i: (0, 0))
    wspec = lambda r, k: pl.BlockSpec((r, k), lambda i: (0, 0), pipeline_mode=pl.Buffered(1))
    return pl.pallas_call(
        _ffn_kernel,
        grid=(n_tok // FFN_ROWS,),
        in_specs=[row, vec, wspec(D_MODEL, D_FF), wspec(D_MODEL, D_FF),
                  wspec(D_FF, D_MODEL), vec],
        out_specs=row,
        out_shape=jax.ShapeDtypeStruct((n_tok, D_MODEL), F32),
        compiler_params=pltpu.CompilerParams(vmem_limit_bytes=VMEM_LIMIT),
        name="ffn",
    )(x, g2, wg, wu, wd, gf)


def _ssm_param_layouts(lam_re, lam_im, log_dt, b_re, b_im, c_re, c_im, d):
    both = lambda t: jnp.concatenate([t[0], t[1]], axis=-1)
    ldt = jnp.broadcast_to(log_dt[:, :, None], (2, SSM_GROUPS, SSM_STATE))
    rows = jnp.stack([both(lam_re), both(lam_im), both(ldt)], axis=1)
    rowp = jnp.pad(rows, ((0, 0), (0, SUBLANES - 3), (0, 0)))
    colp = jnp.swapaxes(rowp, 1, 2)
    bt = jnp.concatenate([both(jnp.swapaxes(b_re, 2, 3)),
                          both(jnp.swapaxes(b_im, 2, 3))], axis=1)
    cw = jnp.concatenate([jnp.tile(c_re, (1, 1, 2)), jnp.tile(c_im, (1, 1, 2))], axis=1)
    ct = jnp.swapaxes(cw, 1, 2)
    drow = jnp.repeat(d, CHUNK, axis=-1)[:, None, :]
    return rowp, colp, bt, cw, ct, drow


def kernel(x, norm1_g, w_in, attn_sink, ssm_lambda_re, ssm_lambda_im, ssm_log_dt, ssm_b_re, ssm_b_im, ssm_c_re, ssm_c_im, ssm_d, w_glu, w_attn_branch, w_ssm_branch, w_out, norm2_g, w_ffn_gate, w_ffn_up, w_ffn_down, norm_f_g):
    pos = jnp.arange(SEQ, dtype=F32)
    inv_freq = ROPE_THETA ** (-jnp.arange(0, ROPE_DIM, 2, dtype=F32) / ROPE_DIM)
    ang = inv_freq[:, None] * pos[None, :]
    cos_t, sin_t = jnp.cos(ang), jnp.sin(ang)
    sink_rows = jnp.repeat(attn_sink[0].astype(F32), CHUNK).reshape(
        N_KV_HEADS, 1, GQA_GROUP * CHUNK)

    q_t, k_t, v_t, u4, ga_t, gs_t = _in_proj(
        x, norm1_g[0][None, :], w_in[0].T.astype(BF16), cos_t, sin_t)
    attn_t = _attention(q_t, k_t, v_t, sink_rows)
    y4 = _ssm(u4, *_ssm_param_layouts(
        ssm_lambda_re[0], ssm_lambda_im[0], ssm_log_dt[0], ssm_b_re[0], ssm_b_im[0],
        ssm_c_re[0], ssm_c_im[0], ssm_d[0]))
    x1 = _merge(y4, attn_t, ga_t, gs_t, x,
                w_glu[0].T.astype(BF16), w_attn_branch[0].T.astype(BF16),
                w_ssm_branch[0].T.astype(BF16), w_out[0].T.astype(BF16))
    out = _ffn(x1.reshape(BATCH * SEQ, D_MODEL), norm2_g[0][None, :],
               w_ffn_gate[0].astype(BF16), w_ffn_up[0].astype(BF16),
               w_ffn_down[0].astype(BF16), norm_f_g[None, :])
    return out.reshape(BATCH, SEQ, D_MODEL)
```

```python
import math

import jax
import jax.numpy as jnp
from jax import lax
from jax.experimental import pallas as pl
from jax.experimental.pallas import tpu as pltpu

D_MODEL = 1024
BATCH = 8
SEQ = 4096
N_HEADS = 8
N_KV_HEADS = 2
HEAD_DIM = 64
GQA_GROUP = N_HEADS // N_KV_HEADS
ATTN_WIDTH = N_HEADS * HEAD_DIM
KV_WIDTH = N_KV_HEADS * HEAD_DIM
ROPE_DIM = HEAD_DIM // 4
ROPE_HALF = ROPE_DIM // 2
ROPE_THETA = 500000.0
WINDOW = 128
SSM_GROUP_CH = 16
SSM_WIDTH = D_MODEL // 2
SSM_GROUPS = SSM_WIDTH // SSM_GROUP_CH
SSM_STATE = 64
D_FF = 2816
IN_COLS = ATTN_WIDTH + 2 * KV_WIDTH + SSM_WIDTH + 2 * D_MODEL
RMS_EPS = 1e-6
NEG_INF = -1e30

LANES = 128
SUBLANES = 8
CHUNK = LANES
N_CHUNKS = SEQ // CHUNK
ROWS = N_CHUNKS * BATCH
FLAT = SSM_GROUP_CH * CHUNK
ATTN_QB = 8
LOG2E = math.log2(math.e)
SUM_ROWS = 16
FFN_ROWS = 512
FFN_COLS = 256
VMEM_LIMIT = 58 * 1024 * 1024

assert BATCH == SUBLANES and WINDOW == CHUNK and D_FF % FFN_COLS == 0

_Q0, _K0, _V0, _U0, _GA0, _GS0 = 0, 512, 640, 768, 1280, 2304

BF16 = jnp.bfloat16
F32 = jnp.float32


def _sigmoid(x):
    return 1.0 / (1.0 + jnp.exp(-x))


def _rmsnorm(x, g):
    return x * lax.rsqrt(jnp.mean(x * x, axis=-1, keepdims=True) + RMS_EPS) * g


def _dot(a, b):
    return jnp.dot(a, b, preferred_element_type=F32)


def _lane_cat(ref, n):
    return jnp.concatenate([ref[i] for i in range(n)], axis=1)


def _in_proj_kernel(x_ref, g_ref, w_ref, cos_ref, sin_ref,
                    q_ref, k_ref, v_ref, u_ref, ga_ref, gs_ref):
    x = x_ref[...].reshape(BATCH * CHUNK, D_MODEL)
    h = _rmsnorm(x, g_ref[...]).astype(BF16)

    def proj_t(r0, r1):
        return lax.dot_general(w_ref[r0:r1, :], h, (((1,), (1,)), ((), ())),
                               preferred_element_type=F32)

    cos = jnp.concatenate([cos_ref[...]] * BATCH, axis=1)
    sin = jnp.concatenate([sin_ref[...]] * BATCH, axis=1)

    def rope_t(t, n_heads):
        parts = []
        for hd in range(n_heads):
            r1 = t[hd * HEAD_DIM:hd * HEAD_DIM + ROPE_HALF]
            r2 = t[hd * HEAD_DIM + ROPE_HALF:hd * HEAD_DIM + ROPE_DIM]
            parts += [r1 * cos - r2 * sin, r2 * cos + r1 * sin,
                      t[hd * HEAD_DIM + ROPE_DIM:(hd + 1) * HEAD_DIM]]
        return jnp.concatenate(parts, axis=0)

    def put(ref, t):
        for b in range(BATCH):
            ref[b] = t[:, b * CHUNK:(b + 1) * CHUNK].astype(ref.dtype)

    put(q_ref, rope_t(proj_t(_Q0, _K0) * (HEAD_DIM ** -0.5 * LOG2E), N_HEADS))
    put(k_ref, rope_t(proj_t(_K0, _V0), N_KV_HEADS))
    put(v_ref, proj_t(_V0, _U0))
    u_t = proj_t(_U0, _GA0)
    for b in range(BATCH):
        u_ref[0, pl.ds(b, SSM_WIDTH, stride=BATCH), :] = u_t[:, b * CHUNK:(b + 1) * CHUNK]
    put(ga_ref, proj_t(_GA0, _GS0))
    put(gs_ref, proj_t(_GS0, IN_COLS))


def _in_proj(x, g1, w_t, cos_t, sin_t):
    tok = lambda rows, dt: jax.ShapeDtypeStruct((BATCH, rows, SEQ), dt)
    tspec = lambda rows: pl.BlockSpec((BATCH, rows, CHUNK), lambda c: (0, 0, c))
    return pl.pallas_call(
        _in_proj_kernel,
        grid=(N_CHUNKS,),
        in_specs=[
            pl.BlockSpec((BATCH, CHUNK, D_MODEL), lambda c: (0, c, 0)),
            pl.BlockSpec((1, D_MODEL), lambda c: (0, 0)),
            pl.BlockSpec((IN_COLS, D_MODEL), lambda c: (0, 0), pipeline_mode=pl.Buffered(1)),
            pl.BlockSpec((ROPE_HALF, CHUNK), lambda c: (0, c)),
            pl.BlockSpec((ROPE_HALF, CHUNK), lambda c: (0, c)),
        ],
        out_specs=[
            tspec(ATTN_WIDTH), tspec(KV_WIDTH), tspec(KV_WIDTH),
            pl.BlockSpec((1, SSM_WIDTH * BATCH, CHUNK), lambda c: (c, 0, 0)),
            tspec(D_MODEL), tspec(D_MODEL),
        ],
        out_shape=[
            tok(ATTN_WIDTH, BF16), tok(KV_WIDTH, BF16), tok(KV_WIDTH, BF16),
            jax.ShapeDtypeStruct((N_CHUNKS, SSM_WIDTH * BATCH, CHUNK), F32),
            tok(D_MODEL, F32), tok(D_MODEL, F32),
        ],
        compiler_params=pltpu.CompilerParams(vmem_limit_bytes=VMEM_LIMIT),
        name="in_proj",
    )(x, g1, w_t, cos_t, sin_t)


def _attn_kernel(q_ref, kp_ref, kc_ref, kn_ref, vp_ref, vc_ref, vn_ref, sink_ref, o_ref):
    n = pl.program_id(1)
    k_t = jnp.concatenate([kp_ref[0], kc_ref[0], kn_ref[0]], axis=1)
    v_t = jnp.concatenate([vp_ref[0], vc_ref[0], vn_ref[0]], axis=1)
    keys = k_t.astype(F32).T.astype(BF16)
    kk = lax.broadcasted_iota(jnp.int32, (CHUNK, CHUNK), 0)
    qq = lax.broadcasted_iota(jnp.int32, (CHUNK, CHUNK), 1)
    in_prev = jnp.concatenate([kk >= qq] * GQA_GROUP, axis=1)
    in_next = jnp.concatenate([kk <= qq] * GQA_GROUP, axis=1)
    zeros = jnp.zeros((HEAD_DIM, GQA_GROUP * CHUNK), BF16)
    ones = jnp.ones((SUM_ROWS, 3 * CHUNK), BF16)
    cases = [(i, j) for i in range(ATTN_QB) for j in range(N_KV_HEADS)]
    scores = []
    for i, j in cases:
        q_t = jnp.concatenate(
            [q_ref[0, (j * GQA_GROUP + g) * HEAD_DIM:(j * GQA_GROUP + g + 1) * HEAD_DIM,
                   i * CHUNK:(i + 1) * CHUNK] for g in range(GQA_GROUP)], axis=1)
        q_pad = jnp.concatenate([q_t, zeros] if j == 0 else [zeros, q_t], axis=0)
        scores.append(_dot(keys[i * CHUNK:(i + 3) * CHUNK], q_pad))
    for (i, j), s in zip(cases, scores):
        prev_ok = in_prev & (n > 0) if i == 0 else in_prev
        next_ok = in_next & (n < N_CHUNKS // ATTN_QB - 1) if i == ATTN_QB - 1 else in_next
        s = jnp.concatenate([jnp.where(prev_ok, s[:CHUNK], NEG_INF), s[CHUNK:2 * CHUNK],
                             jnp.where(next_ok, s[2 * CHUNK:], NEG_INF)], axis=0)
        sink = sink_ref[j] * LOG2E
        m = jnp.maximum(jnp.max(s, axis=0, keepdims=True), sink)
        p = jnp.exp2(s - m).astype(BF16)
        v_aug = jnp.concatenate(
            [v_t[j * HEAD_DIM:(j + 1) * HEAD_DIM, i * CHUNK:(i + 3) * CHUNK], ones], axis=0)
        o_aug = _dot(v_aug, p)
        o = o_aug[:HEAD_DIM] / (o_aug[HEAD_DIM:HEAD_DIM + 1] + jnp.exp2(sink - m))
        for g in range(GQA_GROUP):
            hd = j * GQA_GROUP + g
            o_ref[0, hd * HEAD_DIM:(hd + 1) * HEAD_DIM, i * CHUNK:(i + 1) * CHUNK] = (
                o[:, g * CHUNK:(g + 1) * CHUNK].astype(o_ref.dtype))


def _attention(q_t, k_t, v_t, sink_rows):
    prev = lambda b, n: (b, 0, jnp.maximum(n * ATTN_QB - 1, 0))
    cur = lambda b, n: (b, 0, n)
    nxt = lambda b, n: (b, 0, jnp.minimum((n + 1) * ATTN_QB, N_CHUNKS - 1))
    edge = lambda im: pl.BlockSpec((1, KV_WIDTH, CHUNK), im)
    mid = pl.BlockSpec((1, KV_WIDTH, ATTN_QB * CHUNK), cur)
    return pl.pallas_call(
        _attn_kernel,
        grid=(BATCH, N_CHUNKS // ATTN_QB),
        in_specs=[pl.BlockSpec((1, ATTN_WIDTH, ATTN_QB * CHUNK), cur),
                  edge(prev), mid, edge(nxt), edge(prev), mid, edge(nxt),
                  pl.BlockSpec((N_KV_HEADS, 1, GQA_GROUP * CHUNK), lambda b, n: (0, 0, 0))],
        out_specs=pl.BlockSpec((1, ATTN_WIDTH, ATTN_QB * CHUNK), cur),
        out_shape=jax.ShapeDtypeStruct((BATCH, ATTN_WIDTH, SEQ), BF16),
        compiler_params=pltpu.CompilerParams(vmem_limit_bytes=VMEM_LIMIT),
        name="attention",
    )(q_t, k_t, k_t, k_t, v_t, v_t, v_t, sink_rows)


def _ssm_kernel(u_ref, rowp_ref, colp_ref, bt_ref, cw_ref, ct_ref, drow_ref,
                y_ref, ktab_ref, tz_ref, in_ref, out_ref):
    half = SSM_STATE
    lo_row = lax.broadcasted_iota(jnp.int32, (1, LANES), 1) < half
    top_col = lax.broadcasted_iota(jnp.int32, (LANES, 1), 0) < half

    lam_r, lam_i = rowp_ref[0, 0:1, :], rowp_ref[0, 1:2, :]
    dt = jnp.exp(rowp_ref[0, 2:3, :])
    z_r, z_i = lam_r * dt, lam_i * dt
    e1 = jnp.exp(z_r)
    lb_r, lb_i = e1 * jnp.cos(z_i), e1 * jnp.sin(z_i)
    den = lam_r * lam_r + lam_i * lam_i
    cf_r = ((lb_r - 1.0) * lam_r + lb_i * lam_i) / den
    cf_i = (lb_i * lam_r - (lb_r - 1.0) * lam_i) / den
    b_r, b_i = bt_ref[0, 0:SSM_GROUP_CH, :], bt_ref[0, SSM_GROUP_CH:, :]
    bb_r = cf_r * b_r - cf_i * b_i
    bb_i = cf_r * b_i + cf_i * b_r
    c_r, c_i = cw_ref[0, 0:SSM_GROUP_CH, :], cw_ref[0, SSM_GROUP_CH:, :]

    z_rc = colp_ref[0, :, 0:1] * jnp.exp(colp_ref[0, :, 2:3])
    z_ic = colp_ref[0, :, 1:2] * jnp.exp(colp_ref[0, :, 2:3])

    m_idx = lax.broadcasted_iota(jnp.int32, (LANES, 2 * CHUNK), 1)
    delta = jnp.where(top_col, m_idx - CHUNK, CHUNK - m_idx)
    ok = (delta >= 0) & (delta < CHUNK)
    pw = jnp.where(ok, delta, 0).astype(F32)
    mag = jnp.exp(pw * z_rc)
    e_r = jnp.where(ok, mag * jnp.cos(pw * z_ic), 0.0)
    e_i = jnp.where(ok, mag * jnp.sin(pw * z_ic), 0.0)
    cb_r = jnp.concatenate([c_r * bb_r[h:h + 1] - c_i * bb_i[h:h + 1]
                            for h in range(SSM_GROUP_CH)], axis=0)
    cb_i = jnp.concatenate([c_r * bb_i[h:h + 1] + c_i * bb_r[h:h + 1]
                            for h in range(SSM_GROUP_CH)], axis=0)
    hi = lax.Precision.HIGHEST
    ktab_ref[...] = (jnp.dot(cb_r, e_r, precision=hi, preferred_element_type=F32)
                     - jnp.dot(cb_i, e_i, precision=hi, preferred_element_type=F32))

    for h in range(SSM_GROUP_CH):
        for hp in range(SSM_GROUP_CH):
            row = ktab_ref[pl.ds(h * SSM_GROUP_CH + hp, 1), :]
            skew = pltpu.roll(jnp.broadcast_to(row, (CHUNK, 2 * CHUNK)), 0, 1,
                              stride=1, stride_axis=0)
            tz_ref[h * CHUNK:(h + 1) * CHUNK, hp * CHUNK:(hp + 1) * CHUNK] = (
                skew[:, CHUNK:].astype(BF16))

    s_idx = lax.broadcasted_iota(jnp.int32, (CHUNK, LANES), 0)
    pw = jnp.where(lo_row, CHUNK - 1 - s_idx, s_idx).astype(F32)
    mag = jnp.exp(pw * z_r)
    p_r, p_i = mag * jnp.cos(pw * z_i), mag * jnp.sin(pw * z_i)
    for h in range(SSM_GROUP_CH):
        rows = slice(h * CHUNK, (h + 1) * CHUNK)
        in_ref[rows, 0:LANES] = (p_r * bb_r[h:h + 1] - p_i * bb_i[h:h + 1]).astype(BF16)
        in_ref[rows, LANES:] = (p_r * bb_i[h:h + 1] + p_i * bb_r[h:h + 1]).astype(BF16)

    t_idx = lax.broadcasted_iota(jnp.int32, (LANES, CHUNK), 1)
    pw = jnp.where(top_col, t_idx + 1, CHUNK - t_idx).astype(F32)
    mag = jnp.exp(pw * z_rc)
    q_r, q_i = mag * jnp.cos(pw * z_ic), mag * jnp.sin(pw * z_ic)
    for hp in range(SSM_GROUP_CH):
        cols = slice(hp * CHUNK, (hp + 1) * CHUNK)
        cc_r = ct_ref[0, :, hp:hp + 1]
        cc_i = ct_ref[0, :, SSM_GROUP_CH + hp:SSM_GROUP_CH + hp + 1]
        out_ref[0:LANES, cols] = (cc_r * q_r - cc_i * q_i).astype(BF16)
        out_ref[LANES:, cols] = (-(cc_r * q_i + cc_i * q_r)).astype(BF16)

    a = jnp.concatenate(
        [u_ref[:, h * BATCH:(h + 1) * BATCH, :].reshape(ROWS, CHUNK)
         for h in range(SSM_GROUP_CH)], axis=1)
    ab = a.astype(BF16)
    e = _dot(ab, in_ref[...])
    mag = jnp.exp(float(CHUNK) * z_r)
    lc_r, lc_i = mag * jnp.cos(float(CHUNK) * z_i), mag * jnp.sin(float(CHUNK) * z_i)
    s_r = jnp.zeros((BATCH, LANES), F32)
    s_i = jnp.zeros((BATCH, LANES), F32)
    hist_r, hist_i = [], []
    for i in range(N_CHUNKS):
        hist_r.append(s_r)
        hist_i.append(s_i)
        fr = slice(i * BATCH, (i + 1) * BATCH)
        br = slice((N_CHUNKS - 1 - i) * BATCH, (N_CHUNKS - i) * BATCH)
        in_r = jnp.where(lo_row, e[fr, 0:LANES], e[br, 0:LANES])
        in_i = jnp.where(lo_row, e[fr, LANES:], e[br, LANES:])
        s_r, s_i = (s_r * lc_r - s_i * lc_i + in_r, s_r * lc_i + s_i * lc_r + in_i)
    states = jnp.concatenate(
        [jnp.concatenate([jnp.where(lo_row, hist_r[c], hist_r[N_CHUNKS - 1 - c]),
                          jnp.where(lo_row, hist_i[c], hist_i[N_CHUNKS - 1 - c])], axis=1)
         for c in range(N_CHUNKS)], axis=0)
    y = (_dot(ab, tz_ref[...]) + _dot(states.astype(BF16), out_ref[...])
         + a * drow_ref[0])
    for hp in range(SSM_GROUP_CH):
        y_ref[:, hp * BATCH:(hp + 1) * BATCH, :] = (
            y[:, hp * CHUNK:(hp + 1) * CHUNK].reshape(N_CHUNKS, BATCH, CHUNK))


def _ssm(u4, rowp, colp, bt, cw, ct, drow):
    grp = lambda shape: pl.BlockSpec((1,) + shape, lambda g: (g, 0, 0))
    data = pl.BlockSpec((N_CHUNKS, SSM_GROUP_CH * BATCH, CHUNK), lambda g: (0, g, 0))
    return pl.pallas_call(
        _ssm_kernel,
        grid=(SSM_GROUPS,),
        in_specs=[data, grp((SUBLANES, LANES)), grp((LANES, SUBLANES)),
                  grp((2 * SSM_GROUP_CH, LANES)), grp((2 * SSM_GROUP_CH, LANES)),
                  grp((LANES, 2 * SSM_GROUP_CH)), grp((1, FLAT))],
        out_specs=data,
        out_shape=jax.ShapeDtypeStruct((N_CHUNKS, SSM_WIDTH * BATCH, CHUNK), F32),
        scratch_shapes=[pltpu.VMEM((SSM_GROUP_CH * SSM_GROUP_CH, 2 * CHUNK), F32),
                        pltpu.VMEM((FLAT, FLAT), BF16),
                        pltpu.VMEM((FLAT, 2 * LANES), BF16),
                        pltpu.VMEM((2 * LANES, FLAT), BF16)],
        compiler_params=pltpu.CompilerParams(vmem_limit_bytes=VMEM_LIMIT),
        name="ssm",
    )(u4, rowp, colp, bt, cw, ct, drow)


MERGE_B = BATCH // 2


def _merge_kernel(y_ref, at_ref, ga_ref, gs_ref, x_ref,
                  wglu_ref, wab_ref, wsb_ref, wout_ref, o_ref):
    b0 = pl.program_id(1) * MERGE_B
    y_t = jnp.concatenate(
        [y_ref[0, pl.ds(b0 + b, SSM_WIDTH, stride=BATCH), :] for b in range(MERGE_B)],
        axis=1)
    yg = 0.5 * y_t * (1.0 + jnp.tanh(math.sqrt(2.0 / math.pi) * (y_t + 0.044715 * y_t ** 3)))
    ssm_t = yg * _sigmoid(_dot(wglu_ref[...], yg.astype(BF16)))
    a_t = _dot(wab_ref[...], _lane_cat(at_ref, MERGE_B))
    s_t = _dot(wsb_ref[...], ssm_t.astype(BF16))
    merged = (_sigmoid(_lane_cat(ga_ref, MERGE_B)) * a_t
              + _sigmoid(_lane_cat(gs_ref, MERGE_B)) * s_t)
    out = _dot(wout_ref[...], merged.astype(BF16)).T
    o_ref[...] = x_ref[...] + out.reshape(MERGE_B, CHUNK, D_MODEL)


def _merge(y4, attn_t, ga_t, gs_t, x, wglu_t, wab_t, wsb_t, wout_t):
    tspec = lambda rows: pl.BlockSpec((MERGE_B, rows, CHUNK), lambda c, hb: (hb, 0, c))
    wspec = lambda r, k: pl.BlockSpec((r, k), lambda c, hb: (0, 0), pipeline_mode=pl.Buffered(1))
    xspec = pl.BlockSpec((MERGE_B, CHUNK, D_MODEL), lambda c, hb: (hb, c, 0))
    return pl.pallas_call(
        _merge_kernel,
        grid=(N_CHUNKS, BATCH // MERGE_B),
        in_specs=[pl.BlockSpec((1, SSM_WIDTH * BATCH, CHUNK), lambda c, hb: (c, 0, 0)),
                  tspec(ATTN_WIDTH), tspec(D_MODEL), tspec(D_MODEL), xspec,
                  wspec(SSM_WIDTH, SSM_WIDTH), wspec(D_MODEL, ATTN_WIDTH),
                  wspec(D_MODEL, SSM_WIDTH), wspec(D_MODEL, D_MODEL)],
        out_specs=xspec,
        out_shape=jax.ShapeDtypeStruct((BATCH, SEQ, D_MODEL), F32),
        compiler_params=pltpu.CompilerParams(vmem_limit_bytes=VMEM_LIMIT),
        name="merge",
    )(y4, attn_t, ga_t, gs_t, x, wglu_t, wab_t, wsb_t, wout_t)


def _ffn_kernel(x_ref, g2_ref, wg_ref, wu_ref, wd_ref, gf_ref, o_ref):
    x = x_ref[...]
    h = _rmsnorm(x, g2_ref[...]).astype(BF16)
    acc = jnp.zeros((FFN_ROWS, D_MODEL), F32)
    for j in range(D_FF // FFN_COLS):
        cols = slice(j * FFN_COLS, (j + 1) * FFN_COLS)
        gate = _dot(h, wg_ref[:, cols])
        up = _dot(h, wu_ref[:, cols])
        act = (gate * _sigmoid(gate) * up).astype(BF16)
        acc = acc + _dot(act, wd_ref[cols, :])
    o_ref[...] = _rmsnorm(x + acc, gf_ref[...])


def _ffn(x, g2, wg, wu, wd, gf):
    n_tok = BATCH * SEQ
    row = pl.BlockSpec((FFN_ROWS, D_MODEL), lambda i: (i, 0))
    vec = pl.BlockSpec((1, D_MODEL), lambda i: (0, 0))
    wspec = lambda r, k: pl.BlockSpec((r, k), lambda i: (0, 0), pipeline_mode=pl.Buffered(1))
    return pl.pallas_call(
        _ffn_kernel,
        grid=(n_tok // FFN_ROWS,),
        in_specs=[row, vec, wspec(D_MODEL, D_FF), wspec(D_MODEL, D_FF),
                  wspec(D_FF, D_MODEL), vec],
        out_specs=row,
        out_shape=jax.ShapeDtypeStruct((n_tok, D_MODEL), F32),
        compiler_params=pltpu.CompilerParams(vmem_limit_bytes=VMEM_LIMIT),
        name="ffn",
    )(x, g2, wg, wu, wd, gf)


def _ssm_param_layouts(lam_re, lam_im, log_dt, b_re, b_im, c_re, c_im, d):
    both = lambda t: jnp.concatenate([t[0], t[1]], axis=-1)
    ldt = jnp.broadcast_to(log_dt[:, :, None], (2, SSM_GROUPS, SSM_STATE))
    rows = jnp.stack([both(lam_re), both(lam_im), both(ldt)], axis=1)
    rowp = jnp.pad(rows, ((0, 0), (0, SUBLANES - 3), (0, 0)))
    colp = jnp.swapaxes(rowp, 1, 2)
    bt = jnp.concatenate([both(jnp.swapaxes(b_re, 2, 3)),
                          both(jnp.swapaxes(b_im, 2, 3))], axis=1)
    cw = jnp.concatenate([jnp.tile(c_re, (1, 1, 2)), jnp.tile(c_im, (1, 1, 2))], axis=1)
    ct = jnp.swapaxes(cw, 1, 2)
    drow = jnp.repeat(d, CHUNK, axis=-1)[:, None, :]
    return rowp, colp, bt, cw, ct, drow


def kernel(x, norm1_g, w_in, attn_sink, ssm_lambda_re, ssm_lambda_im, ssm_log_dt, ssm_b_re, ssm_b_im, ssm_c_re, ssm_c_im, ssm_d, w_glu, w_attn_branch, w_ssm_branch, w_out, norm2_g, w_ffn_gate, w_ffn_up, w_ffn_down, norm_f_g):
    pos = jnp.arange(SEQ, dtype=F32)
    inv_freq = ROPE_THETA ** (-jnp.arange(0, ROPE_DIM, 2, dtype=F32) / ROPE_DIM)
    ang = inv_freq[:, None] * pos[None, :]
    cos_t, sin_t = jnp.cos(ang), jnp.sin(ang)
    sink_rows = jnp.repeat(attn_sink[0].astype(F32), CHUNK).reshape(
        N_KV_HEADS, 1, GQA_GROUP * CHUNK)

    q_t, k_t, v_t, u4, ga_t, gs_t = _in_proj(
        x, norm1_g[0][None, :], w_in[0].T.astype(BF16), cos_t, sin_t)
    attn_t = _attention(q_t, k_t, v_t, sink_rows)
    y4 = _ssm(u4, *_ssm_param_layouts(
        ssm_lambda_re[0], ssm_lambda_im[0], ssm_log_dt[0], ssm_b_re[0], ssm_b_im[0],
        ssm_c_re[0], ssm_c_im[0], ssm_d[0]))
    x1 = _merge(y4, attn_t, ga_t, gs_t, x,
                w_glu[0].T.astype(BF16), w_attn_branch[0].T.astype(BF16),
                w_ssm_branch[0].T.astype(BF16), w_out[0].T.astype(BF16))
    out = _ffn(x1.reshape(BATCH * SEQ, D_MODEL), norm2_g[0][None, :],
               w_ffn_gate[0].astype(BF16), w_ffn_up[0].astype(BF16),
               w_ffn_down[0].astype(BF16), norm_f_g[None, :])
    return out.reshape(BATCH, SEQ, D_MODEL)
```

```python
import math

import jax
import jax.numpy as jnp
from jax import lax
from jax.experimental import pallas as pl
from jax.experimental.pallas import tpu as pltpu

D_MODEL = 1024
BATCH = 8
SEQ = 4096
N_HEADS = 8
N_KV_HEADS = 2
HEAD_DIM = 64
GQA_GROUP = N_HEADS // N_KV_HEADS
ATTN_WIDTH = N_HEADS * HEAD_DIM
KV_WIDTH = N_KV_HEADS * HEAD_DIM
ROPE_DIM = HEAD_DIM // 4
ROPE_HALF = ROPE_DIM // 2
ROPE_THETA = 500000.0
WINDOW = 128
SSM_GROUP_CH = 16
SSM_WIDTH = D_MODEL // 2
SSM_GROUPS = SSM_WIDTH // SSM_GROUP_CH
SSM_STATE = 64
D_FF = 2816
IN_COLS = ATTN_WIDTH + 2 * KV_WIDTH + SSM_WIDTH + 2 * D_MODEL
RMS_EPS = 1e-6
NEG_INF = -1e30

LANES = 128
SUBLANES = 8
MXU_DEPTH = 256
CHUNK = LANES
LOG2_CHUNK = CHUNK.bit_length() - 1
N_CHUNKS = SEQ // CHUNK
ROWS = N_CHUNKS * BATCH
FLAT = SSM_GROUP_CH * CHUNK
ATTN_QB = 8
LOG2E = math.log2(math.e)
SUM_ROWS = 16
FFN_ROWS = 512
FFN_COLS = 256
VMEM_LIMIT = 58 * 1024 * 1024

assert BATCH == SUBLANES and WINDOW == CHUNK and D_FF % FFN_COLS == 0

_Q0, _K0, _V0, _U0, _GA0, _GS0 = 0, 512, 640, 768, 1280, 2304

BF16 = jnp.bfloat16
F32 = jnp.float32


def _sigmoid(x):
    return 1.0 / (1.0 + jnp.exp(-x))


def _rmsnorm(x, g):
    return x * lax.rsqrt(jnp.mean(x * x, axis=-1, keepdims=True) + RMS_EPS) * g


def _dot(a, b):
    return jnp.dot(a, b, preferred_element_type=F32)


def _lane_cat(ref, n):
    return jnp.concatenate([ref[i] for i in range(n)], axis=1)


def _in_proj_kernel(x_ref, g_ref, w_ref, cos_ref, sin_ref,
                    q_ref, k_ref, v_ref, u_ref, ga_ref, gs_ref):
    x = x_ref[...].reshape(BATCH * CHUNK, D_MODEL)
    h = _rmsnorm(x, g_ref[...]).astype(BF16)

    def proj_t(r0, r1):
        return lax.dot_general(w_ref[r0:r1, :], h, (((1,), (1,)), ((), ())),
                               preferred_element_type=F32)

    cos = jnp.concatenate([cos_ref[...]] * BATCH, axis=1)
    sin = jnp.concatenate([sin_ref[...]] * BATCH, axis=1)

    def rope_t(t, n_heads):
        parts = []
        for hd in range(n_heads):
            r1 = t[hd * HEAD_DIM:hd * HEAD_DIM + ROPE_HALF]
            r2 = t[hd * HEAD_DIM + ROPE_HALF:hd * HEAD_DIM + ROPE_DIM]
            parts += [r1 * cos - r2 * sin, r2 * cos + r1 * sin,
                      t[hd * HEAD_DIM + ROPE_DIM:(hd + 1) * HEAD_DIM]]
        return jnp.concatenate(parts, axis=0)

    def put(ref, t):
        for b in range(BATCH):
            ref[b] = t[:, b * CHUNK:(b + 1) * CHUNK].astype(ref.dtype)

    put(q_ref, rope_t(proj_t(_Q0, _K0) * (HEAD_DIM ** -0.5 * LOG2E), N_HEADS))
    put(k_ref, rope_t(proj_t(_K0, _V0), N_KV_HEADS))
    put(v_ref, proj_t(_V0, _U0))
    u_t = proj_t(_U0, _GA0)
    for b in range(BATCH):
        u_ref[0, pl.ds(b, SSM_WIDTH, stride=BATCH), :] = u_t[:, b * CHUNK:(b + 1) * CHUNK]
    put(ga_ref, proj_t(_GA0, _GS0))
    put(gs_ref, proj_t(_GS0, IN_COLS))


def _in_proj(x, g1, w_t, cos_t, sin_t):
    tok = lambda rows, dt: jax.ShapeDtypeStruct((BATCH, rows, SEQ), dt)
    tspec = lambda rows: pl.BlockSpec((BATCH, rows, CHUNK), lambda c: (0, 0, c))
    return pl.pallas_call(
        _in_proj_kernel,
        grid=(N_CHUNKS,),
        in_specs=[
            pl.BlockSpec((BATCH, CHUNK, D_MODEL), lambda c: (0, c, 0)),
            pl.BlockSpec((1, D_MODEL), lambda c: (0, 0)),
            pl.BlockSpec((IN_COLS, D_MODEL), lambda c: (0, 0), pipeline_mode=pl.Buffered(1)),
            pl.BlockSpec((ROPE_HALF, CHUNK), lambda c: (0, c)),
            pl.BlockSpec((ROPE_HALF, CHUNK), lambda c: (0, c)),
        ],
        out_specs=[
            tspec(ATTN_WIDTH), tspec(KV_WIDTH), tspec(KV_WIDTH),
            pl.BlockSpec((1, SSM_WIDTH * BATCH, CHUNK), lambda c: (c, 0, 0)),
            tspec(D_MODEL), tspec(D_MODEL),
        ],
        out_shape=[
            tok(ATTN_WIDTH, BF16), tok(KV_WIDTH, BF16), tok(KV_WIDTH, BF16),
            jax.ShapeDtypeStruct((N_CHUNKS, SSM_WIDTH * BATCH, CHUNK), F32),
            tok(D_MODEL, F32), tok(D_MODEL, F32),
        ],
        compiler_params=pltpu.CompilerParams(vmem_limit_bytes=VMEM_LIMIT),
        name="in_proj",
    )(x, g1, w_t, cos_t, sin_t)


def _attn_kernel(q_ref, kp_ref, kc_ref, kn_ref, vp_ref, vc_ref, vn_ref, sink_ref, o_ref):
    n = pl.program_id(1)
    k_t = jnp.concatenate([kp_ref[0], kc_ref[0], kn_ref[0]], axis=1)
    v_t = jnp.concatenate([vp_ref[0], vc_ref[0], vn_ref[0]], axis=1)
    keys = k_t.astype(F32).T.astype(BF16)
    kk = lax.broadcasted_iota(jnp.int32, (CHUNK, CHUNK), 0)
    qq = lax.broadcasted_iota(jnp.int32, (CHUNK, CHUNK), 1)
    in_prev = jnp.concatenate([kk >= qq] * GQA_GROUP, axis=1)
    in_next = jnp.concatenate([kk <= qq] * GQA_GROUP, axis=1)
    zeros = jnp.zeros((HEAD_DIM, GQA_GROUP * CHUNK), BF16)
    ones = jnp.ones((SUM_ROWS, 3 * CHUNK), BF16)
    cases = [(i, j) for i in range(ATTN_QB) for j in range(N_KV_HEADS)]
    scores = []
    for i, j in cases:
        q_t = jnp.concatenate(
            [q_ref[0, (j * GQA_GROUP + g) * HEAD_DIM:(j * GQA_GROUP + g + 1) * HEAD_DIM,
                   i * CHUNK:(i + 1) * CHUNK] for g in range(GQA_GROUP)], axis=1)
        q_pad = jnp.concatenate([q_t, zeros] if j == 0 else [zeros, q_t], axis=0)
        scores.append(_dot(keys[i * CHUNK:(i + 3) * CHUNK], q_pad))
    for (i, j), s in zip(cases, scores):
        prev_ok = in_prev & (n > 0) if i == 0 else in_prev
        next_ok = in_next & (n < N_CHUNKS // ATTN_QB - 1) if i == ATTN_QB - 1 else in_next
        s = jnp.concatenate([jnp.where(prev_ok, s[:CHUNK], NEG_INF), s[CHUNK:2 * CHUNK],
                             jnp.where(next_ok, s[2 * CHUNK:], NEG_INF)], axis=0)
        sink = sink_ref[j] * LOG2E
        m = jnp.maximum(jnp.max(s, axis=0, keepdims=True), sink)
        p = jnp.exp2(s - m).astype(BF16)
        v_aug = jnp.concatenate(
            [v_t[j * HEAD_DIM:(j + 1) * HEAD_DIM, i * CHUNK:(i + 3) * CHUNK], ones], axis=0)
        o_aug = _dot(v_aug, p)
        o = o_aug[:HEAD_DIM] / (o_aug[HEAD_DIM:HEAD_DIM + 1] + jnp.exp2(sink - m))
        for g in range(GQA_GROUP):
            hd = j * GQA_GROUP + g
            o_ref[0, hd * HEAD_DIM:(hd + 1) * HEAD_DIM, i * CHUNK:(i + 1) * CHUNK] = (
                o[:, g * CHUNK:(g + 1) * CHUNK].astype(o_ref.dtype))


def _attention(q_t, k_t, v_t, sink_rows):
    prev = lambda b, n: (b, 0, jnp.maximum(n * ATTN_QB - 1, 0))
    cur = lambda b, n: (b, 0, n)
    nxt = lambda b, n: (b, 0, jnp.minimum((n + 1) * ATTN_QB, N_CHUNKS - 1))
    edge = lambda im: pl.BlockSpec((1, KV_WIDTH, CHUNK), im)
    mid = pl.BlockSpec((1, KV_WIDTH, ATTN_QB * CHUNK), cur)
    return pl.pallas_call(
        _attn_kernel,
        grid=(BATCH, N_CHUNKS // ATTN_QB),
        in_specs=[pl.BlockSpec((1, ATTN_WIDTH, ATTN_QB * CHUNK), cur),
                  edge(prev), mid, edge(nxt), edge(prev), mid, edge(nxt),
                  pl.BlockSpec((N_KV_HEADS, 1, GQA_GROUP * CHUNK), lambda b, n: (0, 0, 0))],
        out_specs=pl.BlockSpec((1, ATTN_WIDTH, ATTN_QB * CHUNK), cur),
        out_shape=jax.ShapeDtypeStruct((BATCH, ATTN_WIDTH, SEQ), BF16),
        compiler_params=pltpu.CompilerParams(vmem_limit_bytes=VMEM_LIMIT),
        name="attention",
    )(q_t, k_t, k_t, k_t, v_t, v_t, v_t, sink_rows)


def _cmul(a_r, a_i, b_r, b_i):
    return a_r * b_r - a_i * b_i, a_r * b_i + a_i * b_r


def _ssm_kernel(u_ref, rowp_ref, bt_ref, cw_ref, ct_ref, drow_ref,
                y_ref, wf_ref, wb_ref, in_ref, out_ref):
    half = SSM_STATE
    lo_row = lax.broadcasted_iota(jnp.int32, (1, LANES), 1) < half
    top_col = lax.broadcasted_iota(jnp.int32, (LANES, 1), 0) < half

    lam_r, lam_i = rowp_ref[0, 0:1, :], rowp_ref[0, 1:2, :]
    dt = jnp.exp(rowp_ref[0, 2:3, :])
    z_r, z_i = lam_r * dt, lam_i * dt
    e1 = jnp.exp(z_r)
    lb_r, lb_i = e1 * jnp.cos(z_i), e1 * jnp.sin(z_i)
    den = lam_r * lam_r + lam_i * lam_i
    cf_r = ((lb_r - 1.0) * lam_r + lb_i * lam_i) / den
    cf_i = (lb_i * lam_r - (lb_r - 1.0) * lam_i) / den
    b_r, b_i = bt_ref[0, 0:SSM_GROUP_CH, :], bt_ref[0, SSM_GROUP_CH:, :]
    bb_r = cf_r * b_r - cf_i * b_i
    bb_i = cf_r * b_i + cf_i * b_r
    c_r, c_i = cw_ref[0, 0:SSM_GROUP_CH, :], cw_ref[0, SSM_GROUP_CH:, :]

    sq_r, sq_i = [lb_r], [lb_i]
    for _ in range(LOG2_CHUNK):
        r, i = _cmul(sq_r[-1], sq_i[-1], sq_r[-1], sq_i[-1])
        sq_r.append(r)
        sq_i.append(i)
    lc_r, lc_i = sq_r[LOG2_CHUNK], sq_i[LOG2_CHUNK]
    pad = jnp.zeros((LANES - 2 * SUBLANES, LANES), F32)
    cols = jnp.concatenate(sq_r[:SUBLANES] + sq_i[:SUBLANES] + [pad], axis=0).T
    sq_rc = [cols[:, k:k + 1] for k in range(LOG2_CHUNK)]
    sq_ic = [cols[:, SUBLANES + k:SUBLANES + k + 1] for k in range(LOG2_CHUNK)]

    j_idx = lax.broadcasted_iota(jnp.int32, (LANES, CHUNK), 1)
    one, zero = jnp.ones((LANES, CHUNK), F32), jnp.zeros((LANES, CHUNK), F32)
    asc_r, asc_i, dsc_r, dsc_i = one, zero, one, zero
    for k in range(LOG2_CHUNK):
        bit = (j_idx & (1 << k)) != 0
        n_r, n_i = _cmul(asc_r, asc_i, sq_rc[k], sq_ic[k])
        asc_r, asc_i = jnp.where(bit, n_r, asc_r), jnp.where(bit, n_i, asc_i)
        n_r, n_i = _cmul(dsc_r, dsc_i, sq_rc[k], sq_ic[k])
        dsc_r, dsc_i = jnp.where(bit, dsc_r, n_r), jnp.where(bit, dsc_i, n_i)
    q_r, q_i = _cmul(jnp.where(top_col, asc_r, dsc_r), jnp.where(top_col, asc_i, dsc_i),
                     sq_rc[0], sq_ic[0])
    p_r = jnp.where(top_col, dsc_r, asc_r).T
    p_i = jnp.where(top_col, dsc_i, asc_i).T

    first = j_idx == 0
    ef_r = jnp.where(top_col, asc_r, jnp.where(first, 1.0, 0.0))
    ef_i = jnp.where(top_col, asc_i, 0.0)
    eb_r = jnp.where(top_col, 0.0, jnp.where(first, 0.0, q_r))
    eb_i = jnp.where(top_col, 0.0, jnp.where(first, 0.0, q_i))
    e_r = jnp.concatenate([eb_r, ef_r], axis=1)
    e_i = jnp.concatenate([eb_i, ef_i], axis=1)
    cb_r = jnp.concatenate([c_r * bb_r[h:h + 1] - c_i * bb_i[h:h + 1]
                            for h in range(SSM_GROUP_CH)], axis=0)
    cb_i = jnp.concatenate([c_r * bb_i[h:h + 1] + c_i * bb_r[h:h + 1]
                            for h in range(SSM_GROUP_CH)], axis=0)
    hi = lax.Precision.HIGHEST
    ktab = (jnp.dot(cb_r, e_r, precision=hi, preferred_element_type=F32)
            - jnp.dot(cb_i, e_i, precision=hi, preferred_element_type=F32))
    kb, kf = ktab[:, :CHUNK], ktab[:, CHUNK:]
    kb1 = pltpu.roll(kb, 1, 1)
    kf1 = jnp.where(lax.broadcasted_iota(jnp.int32, (1, CHUNK), 1) == 0, kb1,
                    pltpu.roll(kf, 1, 1))
    bf16_bits = lambda t: lax.bitcast_convert_type(t.astype(BF16).astype(F32), jnp.int32)
    pack = lambda even, odd: bf16_bits(odd) | lax.shift_right_logical(bf16_bits(even), 16)
    wf_ref[...] = pack(kf, kf1)
    wb_ref[...] = pack(kb, kb1)

    a = jnp.concatenate(
        [u_ref[:, h * BATCH:(h + 1) * BATCH, :].reshape(ROWS, CHUNK)
         for h in range(SSM_GROUP_CH)], axis=1)
    ab = a.astype(BF16)

    pair = lax.broadcasted_iota(jnp.int32, (CHUNK // 2, CHUNK), 0)
    lane = lax.broadcasted_iota(jnp.int32, (CHUNK // 2, CHUNK), 1)
    upper = lane + 2 * pair < CHUNK
    per_slab = MXU_DEPTH // CHUNK

    def toeplitz_slab(slab):
        col_blocks = []
        for h in range(SSM_GROUP_CH):
            blocks = []
            for hp in range(slab * per_slab, (slab + 1) * per_slab):
                row = pl.ds(h * SSM_GROUP_CH + hp, 1)
                w = jnp.where(upper, jnp.broadcast_to(wf_ref[row, :], (CHUNK // 2, CHUNK)),
                              jnp.broadcast_to(wb_ref[row, :], (CHUNK // 2, CHUNK)))
                w = pltpu.roll(w, 0, 1, stride=2, stride_axis=0)
                blocks.append(pltpu.bitcast(w, BF16))
            col_blocks.append(jnp.concatenate(blocks, axis=1))
        return jnp.concatenate(col_blocks, axis=0)

    for h in range(SSM_GROUP_CH):
        rows = slice(h * CHUNK, (h + 1) * CHUNK)
        in_ref[rows, 0:LANES] = (p_r * bb_r[h:h + 1] - p_i * bb_i[h:h + 1]).astype(BF16)
        in_ref[rows, LANES:] = (p_r * bb_i[h:h + 1] + p_i * bb_r[h:h + 1]).astype(BF16)

    for hp in range(SSM_GROUP_CH):
        cols = slice(hp * CHUNK, (hp + 1) * CHUNK)
        cc_r = ct_ref[0, :, hp:hp + 1]
        cc_i = ct_ref[0, :, SSM_GROUP_CH + hp:SSM_GROUP_CH + hp + 1]
        out_ref[0:LANES, cols] = (cc_r * q_r - cc_i * q_i).astype(BF16)
        out_ref[LANES:, cols] = (-(cc_r * q_i + cc_i * q_r)).astype(BF16)

    e = _dot(ab, in_ref[...])
    s_r = jnp.zeros((BATCH, LANES), F32)
    s_i = jnp.zeros((BATCH, LANES), F32)
    hist_r, hist_i = [], []
    for i in range(N_CHUNKS):
        hist_r.append(s_r)
        hist_i.append(s_i)
        fr = slice(i * BATCH, (i + 1) * BATCH)
        br = slice((N_CHUNKS - 1 - i) * BATCH, (N_CHUNKS - i) * BATCH)
        in_r = jnp.where(lo_row, e[fr, 0:LANES], e[br, 0:LANES])
        in_i = jnp.where(lo_row, e[fr, LANES:], e[br, LANES:])
        s_r, s_i = (s_r * lc_r - s_i * lc_i + in_r, s_r * lc_i + s_i * lc_r + in_i)
    states = jnp.concatenate(
        [jnp.concatenate([jnp.where(lo_row, hist_r[c], hist_r[N_CHUNKS - 1 - c]),
                          jnp.where(lo_row, hist_i[c], hist_i[N_CHUNKS - 1 - c])], axis=1)
         for c in range(N_CHUNKS)], axis=0).astype(BF16)

    for slab in range(SSM_GROUP_CH // per_slab):
        cols = slice(slab * MXU_DEPTH, (slab + 1) * MXU_DEPTH)
        y = (_dot(ab, toeplitz_slab(slab)) + _dot(states, out_ref[:, cols])
             + a[:, cols] * drow_ref[0][:, cols])
        for k in range(per_slab):
            hp = slab * per_slab + k
            y_ref[:, hp * BATCH:(hp + 1) * BATCH, :] = (
                y[:, k * CHUNK:(k + 1) * CHUNK].reshape(N_CHUNKS, BATCH, CHUNK))


def _ssm(u4, rowp, bt, cw, ct, drow):
    grp = lambda shape: pl.BlockSpec((1,) + shape, lambda g: (g, 0, 0))
    data = pl.BlockSpec((N_CHUNKS, SSM_GROUP_CH * BATCH, CHUNK), lambda g: (0, g, 0))
    return pl.pallas_call(
        _ssm_kernel,
        grid=(SSM_GROUPS,),
        in_specs=[data, grp((SUBLANES, LANES)),
                  grp((2 * SSM_GROUP_CH, LANES)), grp((2 * SSM_GROUP_CH, LANES)),
                  grp((LANES, 2 * SSM_GROUP_CH)), grp((1, FLAT))],
        out_specs=data,
        out_shape=jax.ShapeDtypeStruct((N_CHUNKS, SSM_WIDTH * BATCH, CHUNK), F32),
        scratch_shapes=[pltpu.VMEM((SSM_GROUP_CH * SSM_GROUP_CH, CHUNK), jnp.int32),
                        pltpu.VMEM((SSM_GROUP_CH * SSM_GROUP_CH, CHUNK), jnp.int32),
                        pltpu.VMEM((FLAT, 2 * LANES), BF16),
                        pltpu.VMEM((2 * LANES, FLAT), BF16)],
        compiler_params=pltpu.CompilerParams(vmem_limit_bytes=VMEM_LIMIT),
        name="ssm",
    )(u4, rowp, bt, cw, ct, drow)


MERGE_B = BATCH // 2


def _merge_kernel(y_ref, at_ref, ga_ref, gs_ref, x_ref,
                  wglu_ref, wab_ref, wsb_ref, wout_ref, o_ref):
    b0 = pl.program_id(1) * MERGE_B
    y_t = jnp.concatenate(
        [y_ref[0, pl.ds(b0 + b, SSM_WIDTH, stride=BATCH), :] for b in range(MERGE_B)],
        axis=1)
    yg = 0.5 * y_t * (1.0 + jnp.tanh(math.sqrt(2.0 / math.pi) * (y_t + 0.044715 * y_t ** 3)))
    ssm_t = yg * _sigmoid(_dot(wglu_ref[...], yg.astype(BF16)))
    a_t = _dot(wab_ref[...], _lane_cat(at_ref, MERGE_B))
    s_t = _dot(wsb_ref[...], ssm_t.astype(BF16))
    merged = (_sigmoid(_lane_cat(ga_ref, MERGE_B)) * a_t
              + _sigmoid(_lane_cat(gs_ref, MERGE_B)) * s_t)
    out = _dot(wout_ref[...], merged.astype(BF16)).T
    o_ref[...] = x_ref[...] + out.reshape(MERGE_B, CHUNK, D_MODEL)


def _merge(y4, attn_t, ga_t, gs_t, x, wglu_t, wab_t, wsb_t, wout_t):
    tspec = lambda rows: pl.BlockSpec((MERGE_B, rows, CHUNK), lambda c, hb: (hb, 0, c))
    wspec = lambda r, k: pl.BlockSpec((r, k), lambda c, hb: (0, 0), pipeline_mode=pl.Buffered(1))
    xspec = pl.BlockSpec((MERGE_B, CHUNK, D_MODEL), lambda c, hb: (hb, c, 0))
    return pl.pallas_call(
        _merge_kernel,
        grid=(N_CHUNKS, BATCH // MERGE_B),
        in_specs=[pl.BlockSpec((1, SSM_WIDTH * BATCH, CHUNK), lambda c, hb: (c, 0, 0)),
                  tspec(ATTN_WIDTH), tspec(D_MODEL), tspec(D_MODEL), xspec,
                  wspec(SSM_WIDTH, SSM_WIDTH), wspec(D_MODEL, ATTN_WIDTH),
                  wspec(D_MODEL, SSM_WIDTH), wspec(D_MODEL, D_MODEL)],
        out_specs=xspec,
        out_shape=jax.ShapeDtypeStruct((BATCH, SEQ, D_MODEL), F32),
        compiler_params=pltpu.CompilerParams(vmem_limit_bytes=VMEM_LIMIT),
        name="merge",
    )(y4, attn_t, ga_t, gs_t, x, wglu_t, wab_t, wsb_t, wout_t)


def _ffn_kernel(x_ref, g2_ref, wg_ref, wu_ref, wd_ref, gf_ref, o_ref):
    x = x_ref[...]
    h = _rmsnorm(x, g2_ref[...]).astype(BF16)
    acc = jnp.zeros((FFN_ROWS, D_MODEL), F32)
    for j in range(D_FF // FFN_COLS):
        cols = slice(j * FFN_COLS, (j + 1) * FFN_COLS)
        gate = _dot(h, wg_ref[:, cols])
        up = _dot(h, wu_ref[:, cols])
        act = (gate * _sigmoid(gate) * up).astype(BF16)
        acc = acc + _dot(act, wd_ref[cols, :])
    o_ref[...] = _rmsnorm(x + acc, gf_ref[...])


def _ffn(x, g2, wg, wu, wd, gf):
    n_tok = BATCH * SEQ
    row = pl.BlockSpec((FFN_ROWS, D_MODEL), lambda i: (i, 0))
    vec = pl.BlockSpec((1, D_MODEL), lambda i: (0, 0))
    wspec = lambda r, k: pl.BlockSpec((r, k), lambda i: (0, 0), pipeline_mode=pl.Buffered(1))
    return pl.pallas_call(
        _ffn_kernel,
        grid=(n_tok // FFN_ROWS,),
        in_specs=[row, vec, wspec(D_MODEL, D_FF), wspec(D_MODEL, D_FF),
                  wspec(D_FF, D_MODEL), vec],
        out_specs=row,
        out_shape=jax.ShapeDtypeStruct((n_tok, D_MODEL), F32),
        compiler_params=pltpu.CompilerParams(vmem_limit_bytes=VMEM_LIMIT),
        name="ffn",
    )(x, g2, wg, wu, wd, gf)


def _ssm_param_layouts(lam_re, lam_im, log_dt, b_re, b_im, c_re, c_im, d):
    both = lambda t: jnp.concatenate([t[0], t[1]], axis=-1)
    ldt = jnp.broadcast_to(log_dt[:, :, None], (2, SSM_GROUPS, SSM_STATE))
    rows = jnp.stack([both(lam_re), both(lam_im), both(ldt)], axis=1)
    rowp = jnp.pad(rows, ((0, 0), (0, SUBLANES - 3), (0, 0)))
    bt =jnp.concatenate([both(jnp.swapaxes(b_re, 2, 3)),
                          both(jnp.swapaxes(b_im, 2, 3))], axis=1)
    cw = jnp.concatenate([jnp.tile(c_re, (1, 1, 2)), jnp.tile(c_im, (1, 1, 2))], axis=1)
    ct = jnp.swapaxes(cw, 1, 2)
    drow = jnp.repeat(d, CHUNK, axis=-1)[:, None, :]
    return rowp, bt, cw, ct, drow


def kernel(x, norm1_g, w_in, attn_sink, ssm_lambda_re, ssm_lambda_im, ssm_log_dt, ssm_b_re, ssm_b_im, ssm_c_re, ssm_c_im, ssm_d, w_glu, w_attn_branch, w_ssm_branch, w_out, norm2_g, w_ffn_gate, w_ffn_up, w_ffn_down, norm_f_g):
    pos = jnp.arange(SEQ, dtype=F32)
    inv_freq = ROPE_THETA ** (-jnp.arange(0, ROPE_DIM, 2, dtype=F32) / ROPE_DIM)
    ang = inv_freq[:, None] * pos[None, :]
    cos_t, sin_t = jnp.cos(ang), jnp.sin(ang)
    sink_rows = jnp.repeat(attn_sink[0].astype(F32), CHUNK).reshape(
        N_KV_HEADS, 1, GQA_GROUP * CHUNK)

    q_t, k_t, v_t, u4, ga_t, gs_t = _in_proj(
        x, norm1_g[0][None, :], w_in[0].T.astype(BF16), cos_t, sin_t)
    attn_t = _attention(q_t, k_t, v_t, sink_rows)
    y4 = _ssm(u4, *_ssm_param_layouts(
        ssm_lambda_re[0], ssm_lambda_im[0], ssm_log_dt[0], ssm_b_re[0], ssm_b_im[0],
        ssm_c_re[0], ssm_c_im[0], ssm_d[0]))
    x1 = _merge(y4, attn_t, ga_t, gs_t, x,
                w_glu[0].T.astype(BF16), w_attn_branch[0].T.astype(BF16),
                w_ssm_branch[0].T.astype(BF16), w_out[0].T.astype(BF16))
    out = _ffn(x1.reshape(BATCH * SEQ, D_MODEL), norm2_g[0][None, :],
               w_ffn_gate[0].astype(BF16), w_ffn_up[0].astype(BF16),
               w_ffn_down[0].astype(BF16), norm_f_g[None, :])
    return out.reshape(BATCH, SEQ, D_MODEL)
```

```python
import math

import jax
import jax.numpy as jnp
from jax import lax
from jax.experimental import pallas as pl
from jax.experimental.pallas import tpu as pltpu

D_MODEL = 1024
BATCH = 8
SEQ = 4096
N_HEADS = 8
N_KV_HEADS = 2
HEAD_DIM = 64
GQA_GROUP = N_HEADS // N_KV_HEADS
ATTN_WIDTH = N_HEADS * HEAD_DIM
KV_WIDTH = N_KV_HEADS * HEAD_DIM
ROPE_DIM = HEAD_DIM // 4
ROPE_HALF = ROPE_DIM // 2
ROPE_THETA = 500000.0
WINDOW = 128
SSM_GROUP_CH = 16
SSM_WIDTH = D_MODEL // 2
SSM_GROUPS = SSM_WIDTH // SSM_GROUP_CH
SSM_STATE = 64
D_FF = 2816
IN_COLS = ATTN_WIDTH + 2 * KV_WIDTH + SSM_WIDTH + 2 * D_MODEL
RMS_EPS = 1e-6
NEG_INF = -1e30

LANES = 128
SUBLANES = 8
MXU_DEPTH = 256
CHUNK = LANES
LOG2_CHUNK = CHUNK.bit_length() - 1
N_CHUNKS = SEQ // CHUNK
ROWS = N_CHUNKS * BATCH
FLAT = SSM_GROUP_CH * CHUNK
ATTN_QB = 8
LOG2E = math.log2(math.e)
SUM_ROWS = 16
FFN_ROWS = 512
FFN_COLS = 256
VMEM_LIMIT = 58 * 1024 * 1024

assert BATCH == SUBLANES and WINDOW == CHUNK and D_FF % FFN_COLS == 0

_Q0, _K0, _V0, _U0, _GA0, _GS0 = 0, 512, 640, 768, 1280, 2304

BF16 = jnp.bfloat16
F32 = jnp.float32


def _sigmoid(x):
    return 0.5 * jnp.tanh(0.5 * x) + 0.5


def _rmsnorm(x, g):
    return x * lax.rsqrt(jnp.mean(x * x, axis=-1, keepdims=True) + RMS_EPS) * g


def _dot(a, b):
    return jnp.dot(a, b, preferred_element_type=F32)


def _lane_cat(ref, n):
    return jnp.concatenate([ref[i] for i in range(n)], axis=1)


def _in_proj_kernel(x_ref, g_ref, w_ref, cos_ref, sin_ref,
                    q_ref, k_ref, v_ref, u_ref, ga_ref, gs_ref):
    x = x_ref[...].reshape(BATCH * CHUNK, D_MODEL)
    h = _rmsnorm(x, g_ref[...]).astype(BF16)

    def proj_t(r0, r1):
        return lax.dot_general(w_ref[r0:r1, :], h, (((1,), (1,)), ((), ())),
                               preferred_element_type=F32)

    cos = jnp.concatenate([cos_ref[...]] * BATCH, axis=1)
    sin = jnp.concatenate([sin_ref[...]] * BATCH, axis=1)

    def rope_t(t, n_heads):
        parts = []
        for hd in range(n_heads):
            r1 = t[hd * HEAD_DIM:hd * HEAD_DIM + ROPE_HALF]
            r2 = t[hd * HEAD_DIM + ROPE_HALF:hd * HEAD_DIM + ROPE_DIM]
            parts += [r1 * cos - r2 * sin, r2 * cos + r1 * sin,
                      t[hd * HEAD_DIM + ROPE_DIM:(hd + 1) * HEAD_DIM]]
        return jnp.concatenate(parts, axis=0)

    def put(ref, t):
        for b in range(BATCH):
            ref[b] = t[:, b * CHUNK:(b + 1) * CHUNK].astype(ref.dtype)

    put(q_ref, rope_t(proj_t(_Q0, _K0) * (HEAD_DIM ** -0.5 * LOG2E), N_HEADS))
    put(k_ref, rope_t(proj_t(_K0, _V0), N_KV_HEADS))
    put(v_ref, proj_t(_V0, _U0))
    u_t = proj_t(_U0, _GA0)
    for b in range(BATCH):
        u_ref[0, pl.ds(b, SSM_WIDTH, stride=BATCH), :] = u_t[:, b * CHUNK:(b + 1) * CHUNK]
    put(ga_ref, _sigmoid(proj_t(_GA0, _GS0)))
    put(gs_ref, _sigmoid(proj_t(_GS0, IN_COLS)))


def _in_proj(x, g1, w_t, cos_t, sin_t):
    tok = lambda rows, dt: jax.ShapeDtypeStruct((BATCH, rows, SEQ), dt)
    tspec = lambda rows: pl.BlockSpec((BATCH, rows, CHUNK), lambda c: (0, 0, c))
    return pl.pallas_call(
        _in_proj_kernel,
        grid=(N_CHUNKS,),
        in_specs=[
            pl.BlockSpec((BATCH, CHUNK, D_MODEL), lambda c: (0, c, 0)),
            pl.BlockSpec((1, D_MODEL), lambda c: (0, 0)),
            pl.BlockSpec((IN_COLS, D_MODEL), lambda c: (0, 0), pipeline_mode=pl.Buffered(1)),
            pl.BlockSpec((ROPE_HALF, CHUNK), lambda c: (0, c)),
            pl.BlockSpec((ROPE_HALF, CHUNK), lambda c: (0, c)),
        ],
        out_specs=[
            tspec(ATTN_WIDTH), tspec(KV_WIDTH), tspec(KV_WIDTH),
            pl.BlockSpec((1, SSM_WIDTH * BATCH, CHUNK), lambda c: (c, 0, 0)),
            tspec(D_MODEL), tspec(D_MODEL),
        ],
        out_shape=[
            tok(ATTN_WIDTH, BF16), tok(KV_WIDTH, BF16), tok(KV_WIDTH, BF16),
            jax.ShapeDtypeStruct((N_CHUNKS, SSM_WIDTH * BATCH, CHUNK), F32),
            tok(D_MODEL, BF16), tok(D_MODEL, BF16),
        ],
        compiler_params=pltpu.CompilerParams(vmem_limit_bytes=VMEM_LIMIT),
        name="in_proj",
    )(x, g1, w_t, cos_t, sin_t)


def _attn_kernel(q_ref, kp_ref, kc_ref, kn_ref, vp_ref, vc_ref, vn_ref, sink_ref, o_ref):
    n = pl.program_id(1)
    k_t = jnp.concatenate([kp_ref[0], kc_ref[0], kn_ref[0]], axis=1)
    v_t = jnp.concatenate([vp_ref[0], vc_ref[0], vn_ref[0]], axis=1)
    keys = k_t.astype(F32).T.astype(BF16)
    kk = lax.broadcasted_iota(jnp.int32, (CHUNK, CHUNK), 0)
    qq = lax.broadcasted_iota(jnp.int32, (CHUNK, CHUNK), 1)
    in_prev = jnp.concatenate([kk >= qq] * GQA_GROUP, axis=1)
    in_next = jnp.concatenate([kk <= qq] * GQA_GROUP, axis=1)
    zeros = jnp.zeros((HEAD_DIM, GQA_GROUP * CHUNK), BF16)
    ones = jnp.ones((SUM_ROWS, 3 * CHUNK), BF16)
    cases = [(i, j) for i in range(ATTN_QB) for j in range(N_KV_HEADS)]
    scores = []
    for i, j in cases:
        q_t = jnp.concatenate(
            [q_ref[0, (j * GQA_GROUP + g) * HEAD_DIM:(j * GQA_GROUP + g + 1) * HEAD_DIM,
                   i * CHUNK:(i + 1) * CHUNK] for g in range(GQA_GROUP)], axis=1)
        q_pad = jnp.concatenate([q_t, zeros] if j == 0 else [zeros, q_t], axis=0)
        scores.append(_dot(keys[i * CHUNK:(i + 3) * CHUNK], q_pad))
    for (i, j), s in zip(cases, scores):
        prev_ok = in_prev & (n > 0) if i == 0 else in_prev
        next_ok = in_next & (n < N_CHUNKS // ATTN_QB - 1) if i == ATTN_QB - 1 else in_next
        s = jnp.concatenate([jnp.where(prev_ok, s[:CHUNK], NEG_INF), s[CHUNK:2 * CHUNK],
                             jnp.where(next_ok, s[2 * CHUNK:], NEG_INF)], axis=0)
        sink = sink_ref[j] * LOG2E
        m = jnp.maximum(jnp.max(s, axis=0, keepdims=True), sink)
        p = jnp.exp2(s - m).astype(BF16)
        v_aug = jnp.concatenate(
            [v_t[j * HEAD_DIM:(j + 1) * HEAD_DIM, i * CHUNK:(i + 3) * CHUNK], ones], axis=0)
        o_aug = _dot(v_aug, p)
        o = o_aug[:HEAD_DIM] / (o_aug[HEAD_DIM:HEAD_DIM + 1] + jnp.exp2(sink - m))
        for g in range(GQA_GROUP):
            hd = j * GQA_GROUP + g
            o_ref[0, hd * HEAD_DIM:(hd + 1) * HEAD_DIM, i * CHUNK:(i + 1) * CHUNK] = (
                o[:, g * CHUNK:(g + 1) * CHUNK].astype(o_ref.dtype))


def _attention(q_t, k_t, v_t, sink_rows):
    prev = lambda b, n: (b, 0, jnp.maximum(n * ATTN_QB - 1, 0))
    cur = lambda b, n: (b, 0, n)
    nxt = lambda b, n: (b, 0, jnp.minimum((n + 1) * ATTN_QB, N_CHUNKS - 1))
    edge = lambda im: pl.BlockSpec((1, KV_WIDTH, CHUNK), im)
    mid = pl.BlockSpec((1, KV_WIDTH, ATTN_QB * CHUNK), cur)
    return pl.pallas_call(
        _attn_kernel,
        grid=(BATCH, N_CHUNKS // ATTN_QB),
        in_specs=[pl.BlockSpec((1, ATTN_WIDTH, ATTN_QB * CHUNK), cur),
                  edge(prev), mid, edge(nxt), edge(prev), mid, edge(nxt),
                  pl.BlockSpec((N_KV_HEADS, 1, GQA_GROUP * CHUNK), lambda b, n: (0, 0, 0))],
        out_specs=pl.BlockSpec((1, ATTN_WIDTH, ATTN_QB * CHUNK), cur),
        out_shape=jax.ShapeDtypeStruct((BATCH, ATTN_WIDTH, SEQ), BF16),
        compiler_params=pltpu.CompilerParams(vmem_limit_bytes=VMEM_LIMIT),
        name="attention",
    )(q_t, k_t, k_t, k_t, v_t, v_t, v_t, sink_rows)


def _cmul(a_r, a_i, b_r, b_i):
    return a_r * b_r - a_i * b_i, a_r * b_i + a_i * b_r


def _ssm_kernel(u_ref, rowp_ref, bt_ref, cw_ref, ct_ref, drow_ref,
                y_ref, wf_ref, wb_ref, in_ref, out_ref):
    half = SSM_STATE
    lo_row = lax.broadcasted_iota(jnp.int32, (1, LANES), 1) < half
    top_col = lax.broadcasted_iota(jnp.int32, (LANES, 1), 0) < half

    lam_r, lam_i = rowp_ref[0, 0:1, :], rowp_ref[0, 1:2, :]
    dt = jnp.exp(rowp_ref[0, 2:3, :])
    z_r, z_i = lam_r * dt, lam_i * dt
    e1 = jnp.exp(z_r)
    lb_r, lb_i = e1 * jnp.cos(z_i), e1 * jnp.sin(z_i)
    den = lam_r * lam_r + lam_i * lam_i
    cf_r = ((lb_r - 1.0) * lam_r + lb_i * lam_i) / den
    cf_i = (lb_i * lam_r - (lb_r - 1.0) * lam_i) / den
    b_r, b_i = bt_ref[0, 0:SSM_GROUP_CH, :], bt_ref[0, SSM_GROUP_CH:, :]
    bb_r = cf_r * b_r - cf_i * b_i
    bb_i = cf_r * b_i + cf_i * b_r
    c_r, c_i = cw_ref[0, 0:SSM_GROUP_CH, :], cw_ref[0, SSM_GROUP_CH:, :]

    sq_r, sq_i = [lb_r], [lb_i]
    for _ in range(LOG2_CHUNK):
        r, i = _cmul(sq_r[-1], sq_i[-1], sq_r[-1], sq_i[-1])
        sq_r.append(r)
        sq_i.append(i)
    lc_r, lc_i = sq_r[LOG2_CHUNK], sq_i[LOG2_CHUNK]
    pad = jnp.zeros((LANES - 2 * SUBLANES, LANES), F32)
    cols = jnp.concatenate(sq_r[:SUBLANES] + sq_i[:SUBLANES] + [pad], axis=0).T
    sq_rc = [cols[:, k:k + 1] for k in range(LOG2_CHUNK)]
    sq_ic = [cols[:, SUBLANES + k:SUBLANES + k + 1] for k in range(LOG2_CHUNK)]

    j_idx = lax.broadcasted_iota(jnp.int32, (LANES, CHUNK), 1)
    one, zero = jnp.ones((LANES, CHUNK), F32), jnp.zeros((LANES, CHUNK), F32)
    asc_r, asc_i, dsc_r, dsc_i = one, zero, one, zero
    for k in range(LOG2_CHUNK):
        bit = (j_idx & (1 << k)) != 0
        n_r, n_i = _cmul(asc_r, asc_i, sq_rc[k], sq_ic[k])
        asc_r, asc_i = jnp.where(bit, n_r, asc_r), jnp.where(bit, n_i, asc_i)
        n_r, n_i = _cmul(dsc_r, dsc_i, sq_rc[k], sq_ic[k])
        dsc_r, dsc_i = jnp.where(bit, dsc_r, n_r), jnp.where(bit, dsc_i, n_i)
    q_r, q_i = _cmul(jnp.where(top_col, asc_r, dsc_r), jnp.where(top_col, asc_i, dsc_i),
                     sq_rc[0], sq_ic[0])
    p_r = jnp.where(top_col, dsc_r, asc_r).T
    p_i = jnp.where(top_col, dsc_i, asc_i).T

    first = j_idx == 0
    ef_r = jnp.where(top_col, asc_r, jnp.where(first, 1.0, 0.0))
    ef_i = jnp.where(top_col, asc_i, 0.0)
    eb_r = jnp.where(top_col, 0.0, jnp.where(first, 0.0, q_r))
    eb_i = jnp.where(top_col, 0.0, jnp.where(first, 0.0, q_i))
    e_r = jnp.concatenate([eb_r, ef_r], axis=1)
    e_i = jnp.concatenate([eb_i, ef_i], axis=1)
    cb_r = jnp.concatenate([c_r * bb_r[h:h + 1] - c_i * bb_i[h:h + 1]
                            for h in range(SSM_GROUP_CH)], axis=0)
    cb_i = jnp.concatenate([c_r * bb_i[h:h + 1] + c_i * bb_r[h:h + 1]
                            for h in range(SSM_GROUP_CH)], axis=0)
    hi = lax.Precision.HIGHEST
    ktab = (jnp.dot(cb_r, e_r, precision=hi, preferred_element_type=F32)
            - jnp.dot(cb_i, e_i, precision=hi, preferred_element_type=F32))
    kb, kf = ktab[:, :CHUNK], ktab[:, CHUNK:]
    kb1 = pltpu.roll(kb, 1, 1)
    kf1 = jnp.where(lax.broadcasted_iota(jnp.int32, (1, CHUNK), 1) == 0, kb1,
                    pltpu.roll(kf, 1, 1))
    bf16_bits = lambda t: lax.bitcast_convert_type(t.astype(BF16).astype(F32), jnp.int32)
    pack = lambda even, odd: bf16_bits(odd) | lax.shift_right_logical(bf16_bits(even), 16)
    wf_ref[...] = pack(kf, kf1)
    wb_ref[...] = pack(kb, kb1)

    a = jnp.concatenate(
        [u_ref[:, h * BATCH:(h + 1) * BATCH, :].reshape(ROWS, CHUNK)
         for h in range(SSM_GROUP_CH)], axis=1)
    ab = a.astype(BF16)

    pair = lax.broadcasted_iota(jnp.int32, (CHUNK // 2, CHUNK), 0)
    lane = lax.broadcasted_iota(jnp.int32, (CHUNK // 2, CHUNK), 1)
    upper = lane + 2 * pair < CHUNK
    per_slab = MXU_DEPTH // CHUNK

    def toeplitz_slab(slab):
        col_blocks = []
        for h in range(SSM_GROUP_CH):
            blocks = []
            for hp in range(slab * per_slab, (slab + 1) * per_slab):
                row = pl.ds(h * SSM_GROUP_CH + hp, 1)
                w = jnp.where(upper, jnp.broadcast_to(wf_ref[row, :], (CHUNK // 2, CHUNK)),
                              jnp.broadcast_to(wb_ref[row, :], (CHUNK // 2, CHUNK)))
                w = pltpu.roll(w, 0, 1, stride=2, stride_axis=0)
                blocks.append(pltpu.bitcast(w, BF16))
            col_blocks.append(jnp.concatenate(blocks, axis=1))
        return jnp.concatenate(col_blocks, axis=0)

    for h in range(SSM_GROUP_CH):
        rows = slice(h * CHUNK, (h + 1) * CHUNK)
        in_ref[rows, 0:LANES] = (p_r * bb_r[h:h + 1] - p_i * bb_i[h:h + 1]).astype(BF16)
        in_ref[rows, LANES:] = (p_r * bb_i[h:h + 1] + p_i * bb_r[h:h + 1]).astype(BF16)

    for hp in range(SSM_GROUP_CH):
        cols = slice(hp * CHUNK, (hp + 1) * CHUNK)
        cc_r = ct_ref[0, :, hp:hp + 1]
        cc_i = ct_ref[0, :, SSM_GROUP_CH + hp:SSM_GROUP_CH + hp + 1]
        out_ref[0:LANES, cols] = (cc_r * q_r - cc_i * q_i).astype(BF16)
        out_ref[LANES:, cols] = (-(cc_r * q_i + cc_i * q_r)).astype(BF16)

    e = _dot(ab, in_ref[...])
    s_r = jnp.zeros((BATCH, LANES), F32)
    s_i = jnp.zeros((BATCH, LANES), F32)
    hist_r, hist_i = [], []
    for i in range(N_CHUNKS):
        hist_r.append(s_r)
        hist_i.append(s_i)
        fr = slice(i * BATCH, (i + 1) * BATCH)
        br = slice((N_CHUNKS - 1 - i) * BATCH, (N_CHUNKS - i) * BATCH)
        in_r = jnp.where(lo_row, e[fr, 0:LANES], e[br, 0:LANES])
        in_i = jnp.where(lo_row, e[fr, LANES:], e[br, LANES:])
        s_r, s_i = (s_r * lc_r - s_i * lc_i + in_r, s_r * lc_i + s_i * lc_r + in_i)
    states = jnp.concatenate(
        [jnp.concatenate([jnp.where(lo_row, hist_r[c], hist_r[N_CHUNKS - 1 - c]),
                          jnp.where(lo_row, hist_i[c], hist_i[N_CHUNKS - 1 - c])], axis=1)
         for c in range(N_CHUNKS)], axis=0).astype(BF16)

    for slab in range(SSM_GROUP_CH // per_slab):
        cols = slice(slab * MXU_DEPTH, (slab + 1) * MXU_DEPTH)
        y = (_dot(ab, toeplitz_slab(slab)) + _dot(states, out_ref[:, cols])
             + a[:, cols] * drow_ref[0][:, cols])
        for k in range(per_slab):
            hp = slab * per_slab + k
            y_ref[:, hp * BATCH:(hp + 1) * BATCH, :] = (
                y[:, k * CHUNK:(k + 1) * CHUNK].reshape(N_CHUNKS, BATCH, CHUNK))


def _ssm(u4, rowp, bt, cw, ct, drow):
    grp = lambda shape: pl.BlockSpec((1,) + shape, lambda g: (g, 0, 0))
    data = pl.BlockSpec((N_CHUNKS, SSM_GROUP_CH * BATCH, CHUNK), lambda g: (0, g, 0))
    return pl.pallas_call(
        _ssm_kernel,
        grid=(SSM_GROUPS,),
        in_specs=[data, grp((SUBLANES, LANES)),
                  grp((2 * SSM_GROUP_CH, LANES)), grp((2 * SSM_GROUP_CH, LANES)),
                  grp((LANES, 2 * SSM_GROUP_CH)), grp((1, FLAT))],
        out_specs=data,
        out_shape=jax.ShapeDtypeStruct((N_CHUNKS, SSM_WIDTH * BATCH, CHUNK), F32),
        scratch_shapes=[pltpu.VMEM((SSM_GROUP_CH * SSM_GROUP_CH, CHUNK), jnp.int32),
                        pltpu.VMEM((SSM_GROUP_CH * SSM_GROUP_CH, CHUNK), jnp.int32),
                        pltpu.VMEM((FLAT, 2 * LANES), BF16),
                        pltpu.VMEM((2 * LANES, FLAT), BF16)],
        compiler_params=pltpu.CompilerParams(vmem_limit_bytes=VMEM_LIMIT),
        name="ssm",
    )(u4, rowp, bt, cw, ct, drow)


MERGE_B = BATCH
MERGE_SPLIT = 4


def _merge_kernel(y_ref, at_ref, ga_ref, gs_ref, x_ref,
                  wglu_ref, wab_ref, wsb_ref, wout_ref, o_ref):
    b0 = pl.program_id(1) * MERGE_B
    nb = MERGE_B // MERGE_SPLIT
    tiles = [range(t * nb, (t + 1) * nb) for t in range(MERGE_SPLIT)]
    cat = lambda ref, bs: jnp.concatenate([ref[b] for b in bs], axis=1)
    y_t = [jnp.concatenate([y_ref[0, pl.ds(b0 + b, SSM_WIDTH, stride=BATCH), :] for b in bs],
                           axis=1) for bs in tiles]
    a_t = [_dot(wab_ref[...], cat(at_ref, bs)) for bs in tiles]
    yg = [0.5 * y * (1.0 + jnp.tanh(math.sqrt(2.0 / math.pi) * (y + 0.044715 * y ** 3)))
          for y in y_t]
    z_t = [_dot(wglu_ref[...], g.astype(BF16)) for g in yg]
    ssm_t = [g * _sigmoid(z) for g, z in zip(yg, z_t)]
    s_t = [_dot(wsb_ref[...], s.astype(BF16)) for s in ssm_t]
    merged = [cat(ga_ref, bs).astype(F32) * a + cat(gs_ref, bs).astype(F32) * s
              for bs, a, s in zip(tiles, a_t, s_t)]
    for bs, m in zip(tiles, merged):
        mb = m.astype(BF16)
        for r in range(0, D_MODEL, MXU_DEPTH):
            o = _dot(wout_ref[r:r + MXU_DEPTH, :], mb).T
            o_ref[bs.start:bs.stop, :, r:r + MXU_DEPTH] = (
                x_ref[bs.start:bs.stop, :, r:r + MXU_DEPTH] + o.reshape(nb, CHUNK, MXU_DEPTH))


def _merge(y4, attn_t, ga_t, gs_t, x, wglu_t, wab_t, wsb_t, wout_t):
    tspec = lambda rows: pl.BlockSpec((MERGE_B, rows, CHUNK), lambda c, hb: (hb, 0, c))
    wspec = lambda r, k: pl.BlockSpec((r, k), lambda c, hb: (0, 0), pipeline_mode=pl.Buffered(1))
    xspec = pl.BlockSpec((MERGE_B, CHUNK, D_MODEL), lambda c, hb: (hb, c, 0))
    return pl.pallas_call(
        _merge_kernel,
        grid=(N_CHUNKS, BATCH // MERGE_B),
        in_specs=[pl.BlockSpec((1, SSM_WIDTH * BATCH, CHUNK), lambda c, hb: (c, 0, 0)),
                  tspec(ATTN_WIDTH), tspec(D_MODEL), tspec(D_MODEL), xspec,
                  wspec(SSM_WIDTH, SSM_WIDTH), wspec(D_MODEL, ATTN_WIDTH),
                  wspec(D_MODEL, SSM_WIDTH), wspec(D_MODEL, D_MODEL)],
        out_specs=xspec,
        out_shape=jax.ShapeDtypeStruct((BATCH, SEQ, D_MODEL), F32),
        compiler_params=pltpu.CompilerParams(vmem_limit_bytes=VMEM_LIMIT),
        name="merge",
    )(y4, attn_t, ga_t, gs_t, x, wglu_t, wab_t, wsb_t, wout_t)


def _ffn_kernel(x_ref, g2_ref, wg_ref, wu_ref, wd_ref, gf_ref, o_ref):
    x = x_ref[...]
    h = _rmsnorm(x, g2_ref[...]).astype(BF16)
    acc = jnp.zeros((FFN_ROWS, D_MODEL), F32)
    for j in range(D_FF // FFN_COLS):
        cols = slice(j * FFN_COLS, (j + 1) * FFN_COLS)
        gate = _dot(h, wg_ref[:, cols])
        up = _dot(h, wu_ref[:, cols])
        act = (gate * _sigmoid(gate) * up).astype(BF16)
        acc = acc + _dot(act, wd_ref[cols, :])
    o_ref[...] = _rmsnorm(x + acc, gf_ref[...])


def _ffn(x, g2, wg, wu, wd, gf):
    n_tok = BATCH * SEQ
    row = pl.BlockSpec((FFN_ROWS, D_MODEL), lambda i: (i, 0))
    vec = pl.BlockSpec((1, D_MODEL), lambda i: (0, 0))
    wspec = lambda r, k: pl.BlockSpec((r, k), lambda i: (0, 0), pipeline_mode=pl.Buffered(1))
    return pl.pallas_call(
        _ffn_kernel,
        grid=(n_tok // FFN_ROWS,),
        in_specs=[row, vec, wspec(D_MODEL, D_FF), wspec(D_MODEL, D_FF),
                  wspec(D_FF, D_MODEL), vec],
        out_specs=row,
        out_shape=jax.ShapeDtypeStruct((n_tok, D_MODEL), F32),
        compiler_params=pltpu.CompilerParams(vmem_limit_bytes=VMEM_LIMIT),
        name="ffn",
    )(x, g2, wg, wu, wd, gf)


def _ssm_param_layouts(lam_re, lam_im, log_dt, b_re, b_im, c_re, c_im, d):
    both = lambda t: jnp.concatenate([t[0], t[1]], axis=-1)
    ldt = jnp.broadcast_to(log_dt[:, :, None], (2, SSM_GROUPS, SSM_STATE))
    rows = jnp.stack([both(lam_re), both(lam_im), both(ldt)], axis=1)
    rowp = jnp.pad(rows, ((0, 0), (0, SUBLANES - 3), (0, 0)))
    bt =jnp.concatenate([both(jnp.swapaxes(b_re, 2, 3)),
                          both(jnp.swapaxes(b_im, 2, 3))], axis=1)
    cw = jnp.concatenate([jnp.tile(c_re, (1, 1, 2)), jnp.tile(c_im, (1, 1, 2))], axis=1)
    ct = jnp.swapaxes(cw, 1, 2)
    drow = jnp.repeat(d, CHUNK, axis=-1)[:, None, :]
    return rowp, bt, cw, ct, drow


def kernel(x, norm1_g, w_in, attn_sink, ssm_lambda_re, ssm_lambda_im, ssm_log_dt, ssm_b_re, ssm_b_im, ssm_c_re, ssm_c_im, ssm_d, w_glu, w_attn_branch, w_ssm_branch, w_out, norm2_g, w_ffn_gate, w_ffn_up, w_ffn_down, norm_f_g):
    pos = jnp.arange(SEQ, dtype=F32)
    inv_freq = ROPE_THETA ** (-jnp.arange(0, ROPE_DIM, 2, dtype=F32) / ROPE_DIM)
    ang = inv_freq[:, None] * pos[None, :]
    cos_t, sin_t = jnp.cos(ang), jnp.sin(ang)
    sink_rows = jnp.repeat(attn_sink[0].astype(F32), CHUNK).reshape(
        N_KV_HEADS, 1, GQA_GROUP * CHUNK)

    q_t, k_t, v_t, u4, ga_t, gs_t = _in_proj(
        x, norm1_g[0][None, :], w_in[0].T.astype(BF16), cos_t, sin_t)
    attn_t = _attention(q_t, k_t, v_t, sink_rows)
    y4 = _ssm(u4, *_ssm_param_layouts(
        ssm_lambda_re[0], ssm_lambda_im[0], ssm_log_dt[0], ssm_b_re[0], ssm_b_im[0],
        ssm_c_re[0], ssm_c_im[0], ssm_d[0]))
    x1 = _merge(y4, attn_t, ga_t, gs_t, x,
                w_glu[0].T.astype(BF16), w_attn_branch[0].T.astype(BF16),
                w_ssm_branch[0].T.astype(BF16), w_out[0].T.astype(BF16))
    out = _ffn(x1.reshape(BATCH * SEQ, D_MODEL), norm2_g[0][None, :],
               w_ffn_gate[0].astype(BF16), w_ffn_up[0].astype(BF16),
               w_ffn_down[0].astype(BF16), norm_f_g[None, :])
    return out.reshape(BATCH, SEQ, D_MODEL)
```

```python
import math

import jax
import jax.numpy as jnp
from jax import lax
from jax.experimental import pallas as pl
from jax.experimental.pallas import tpu as pltpu

D_MODEL = 1024
BATCH = 8
SEQ = 4096
N_HEADS = 8
N_KV_HEADS = 2
HEAD_DIM = 64
GQA_GROUP = N_HEADS // N_KV_HEADS
ATTN_WIDTH = N_HEADS * HEAD_DIM
KV_WIDTH = N_KV_HEADS * HEAD_DIM
ROPE_DIM = HEAD_DIM // 4
ROPE_HALF = ROPE_DIM // 2
ROPE_THETA = 500000.0
WINDOW = 128
SSM_GROUP_CH = 16
SSM_WIDTH = D_MODEL // 2
SSM_GROUPS = SSM_WIDTH // SSM_GROUP_CH
SSM_STATE = 64
D_FF = 2816
IN_COLS = ATTN_WIDTH + 2 * KV_WIDTH + SSM_WIDTH + 2 * D_MODEL
RMS_EPS = 1e-6
NEG_INF = -1e30

LANES = 128
SUBLANES = 8
MXU_DEPTH = 256
CHUNK = LANES
LOG2_CHUNK = CHUNK.bit_length() - 1
N_CHUNKS = SEQ // CHUNK
ROWS = N_CHUNKS * BATCH
FLAT = SSM_GROUP_CH * CHUNK
SSM_STEP_GROUPS = 2
ATTN_QB = 8
LOG2E = math.log2(math.e)
SUM_ROWS = 16
FFN_ROWS = 1024
FFN_SPLIT = 2
FFN_COLS = 256
VMEM_LIMIT = 58 * 1024 * 1024

assert BATCH == SUBLANES and WINDOW == CHUNK and D_FF % FFN_COLS == 0

_Q0, _K0, _V0, _U0, _GA0, _GS0 = 0, 512, 640, 768, 1280, 2304

BF16 = jnp.bfloat16
F32 = jnp.float32


def _sigmoid(x):
    return 0.5 * jnp.tanh(0.5 * x) + 0.5


def _rmsnorm(x, g):
    return x * lax.rsqrt(jnp.mean(x * x, axis=-1, keepdims=True) + RMS_EPS) * g


def _dot(a, b):
    return jnp.dot(a, b, preferred_element_type=F32)


def _lane_cat(ref, n):
    return jnp.concatenate([ref[i] for i in range(n)], axis=1)


def _in_proj_kernel(x_ref, g_ref, w_ref, cos_ref, sin_ref,
                    q_ref, k_ref, v_ref, u_ref, ga_ref, gs_ref):
    nb = MXU_DEPTH // CHUNK
    h = [_rmsnorm(x_ref[b:b + nb].reshape(MXU_DEPTH, D_MODEL), g_ref[...]).astype(BF16)
         for b in range(0, BATCH, nb)]

    def proj_t(r0, r1):
        return jnp.concatenate(
            [lax.dot_general(w_ref[r0:r1, :], t, (((1,), (1,)), ((), ())),
                             preferred_element_type=F32) for t in h], axis=1)

    cos = jnp.concatenate([cos_ref[...]] * BATCH, axis=1)
    sin = jnp.concatenate([sin_ref[...]] * BATCH, axis=1)

    def rope_t(t, n_heads):
        parts = []
        for hd in range(n_heads):
            r1 = t[hd * HEAD_DIM:hd * HEAD_DIM + ROPE_HALF]
            r2 = t[hd * HEAD_DIM + ROPE_HALF:hd * HEAD_DIM + ROPE_DIM]
            parts += [r1 * cos - r2 * sin, r2 * cos + r1 * sin,
                      t[hd * HEAD_DIM + ROPE_DIM:(hd + 1) * HEAD_DIM]]
        return jnp.concatenate(parts, axis=0)

    def put(ref, r0, t):
        for b in range(BATCH):
            ref[b, r0:r0 + t.shape[0], :] = t[:, b * CHUNK:(b + 1) * CHUNK].astype(ref.dtype)

    slab = MXU_DEPTH
    for r in range(0, ATTN_WIDTH, slab):
        t = proj_t(_Q0 + r, _Q0 + r + slab) * (HEAD_DIM ** -0.5 * LOG2E)
        put(q_ref, r, rope_t(t, slab // HEAD_DIM))
    kv = proj_t(_K0, _U0)
    put(k_ref, 0, rope_t(kv[:KV_WIDTH], N_KV_HEADS))
    put(v_ref, 0, kv[KV_WIDTH:])
    for r in range(0, SSM_WIDTH, slab):
        u_t = proj_t(_U0 + r, _U0 + r + slab)
        for b in range(BATCH):
            u_ref[0, pl.ds(r * BATCH + b, slab, stride=BATCH), :] = (
                u_t[:, b * CHUNK:(b + 1) * CHUNK])
    for r in range(0, D_MODEL, slab):
        put(ga_ref, r, _sigmoid(proj_t(_GA0 + r, _GA0 + r + slab)))
    for r in range(0, D_MODEL, slab):
        put(gs_ref, r, _sigmoid(proj_t(_GS0 + r, _GS0 + r + slab)))


def _in_proj(x, g1, w_t, cos_t, sin_t):
    tok = lambda rows, dt: jax.ShapeDtypeStruct((BATCH, rows, SEQ), dt)
    tspec = lambda rows: pl.BlockSpec((BATCH, rows, CHUNK), lambda c: (0, 0, c))
    return pl.pallas_call(
        _in_proj_kernel,
        grid=(N_CHUNKS,),
        in_specs=[
            pl.BlockSpec((BATCH, CHUNK, D_MODEL), lambda c: (0, c, 0)),
            pl.BlockSpec((1, D_MODEL), lambda c: (0, 0)),
            pl.BlockSpec((IN_COLS, D_MODEL), lambda c: (0, 0), pipeline_mode=pl.Buffered(1)),
            pl.BlockSpec((ROPE_HALF, CHUNK), lambda c: (0, c)),
            pl.BlockSpec((ROPE_HALF, CHUNK), lambda c: (0, c)),
        ],
        out_specs=[
            tspec(ATTN_WIDTH), tspec(KV_WIDTH), tspec(KV_WIDTH),
            pl.BlockSpec((1, SSM_WIDTH * BATCH, CHUNK), lambda c: (c, 0, 0)),
            tspec(D_MODEL), tspec(D_MODEL),
        ],
        out_shape=[
            tok(ATTN_WIDTH, BF16), tok(KV_WIDTH, BF16), tok(KV_WIDTH, BF16),
            jax.ShapeDtypeStruct((N_CHUNKS, SSM_WIDTH * BATCH, CHUNK), F32),
            tok(D_MODEL, BF16), tok(D_MODEL, BF16),
        ],
        compiler_params=pltpu.CompilerParams(vmem_limit_bytes=VMEM_LIMIT),
        name="in_proj",
    )(x, g1, w_t, cos_t, sin_t)


def _attn_kernel(q_ref, kp_ref, kc_ref, kn_ref, vp_ref, vc_ref, vn_ref, sink_ref, o_ref):
    n = pl.program_id(1)
    k_t = jnp.concatenate([kp_ref[0], kc_ref[0], kn_ref[0]], axis=1)
    v_t = jnp.concatenate([vp_ref[0], vc_ref[0], vn_ref[0]], axis=1)
    keys = k_t.astype(F32).T.astype(BF16)
    kk = lax.broadcasted_iota(jnp.int32, (CHUNK, CHUNK), 0)
    qq = lax.broadcasted_iota(jnp.int32, (CHUNK, CHUNK), 1)
    in_prev = jnp.concatenate([kk >= qq] * GQA_GROUP, axis=1)
    in_next = jnp.concatenate([kk <= qq] * GQA_GROUP, axis=1)
    zeros = jnp.zeros((HEAD_DIM, GQA_GROUP * CHUNK), BF16)
    ones = jnp.ones((SUM_ROWS, 3 * CHUNK), BF16)
    cases = [(i, j) for i in range(ATTN_QB) for j in range(N_KV_HEADS)]
    scores = []
    for i, j in cases:
        q_t = jnp.concatenate(
            [q_ref[0, (j * GQA_GROUP + g) * HEAD_DIM:(j * GQA_GROUP + g + 1) * HEAD_DIM,
                   i * CHUNK:(i + 1) * CHUNK] for g in range(GQA_GROUP)], axis=1)
        q_pad = jnp.concatenate([q_t, zeros] if j == 0 else [zeros, q_t], axis=0)
        scores.append(_dot(keys[i * CHUNK:(i + 3) * CHUNK], q_pad))
    for (i, j), s in zip(cases, scores):
        prev_ok = in_prev & (n > 0) if i == 0 else in_prev
        next_ok = in_next & (n < N_CHUNKS // ATTN_QB - 1) if i == ATTN_QB - 1 else in_next
        s = jnp.concatenate([jnp.where(prev_ok, s[:CHUNK], NEG_INF), s[CHUNK:2 * CHUNK],
                             jnp.where(next_ok, s[2 * CHUNK:], NEG_INF)], axis=0)
        sink = sink_ref[j] * LOG2E
        m = jnp.maximum(jnp.max(s, axis=0, keepdims=True), sink)
        p = jnp.exp2(s - m).astype(BF16)
        v_aug = jnp.concatenate(
            [v_t[j * HEAD_DIM:(j + 1) * HEAD_DIM, i * CHUNK:(i + 3) * CHUNK], ones], axis=0)
        o_aug = _dot(v_aug, p)
        o = o_aug[:HEAD_DIM] / (o_aug[HEAD_DIM:HEAD_DIM + 1] + jnp.exp2(sink - m))
        for g in range(GQA_GROUP):
            hd = j * GQA_GROUP + g
            o_ref[0, hd * HEAD_DIM:(hd + 1) * HEAD_DIM, i * CHUNK:(i + 1) * CHUNK] = (
                o[:, g * CHUNK:(g + 1) * CHUNK].astype(o_ref.dtype))


def _attention(q_t, k_t, v_t, sink_rows):
    prev = lambda b, n: (b, 0, jnp.maximum(n * ATTN_QB - 1, 0))
    cur = lambda b, n: (b, 0, n)
    nxt = lambda b, n: (b, 0, jnp.minimum((n + 1) * ATTN_QB, N_CHUNKS - 1))
    edge = lambda im: pl.BlockSpec((1, KV_WIDTH, CHUNK), im)
    mid = pl.BlockSpec((1, KV_WIDTH, ATTN_QB * CHUNK), cur)
    return pl.pallas_call(
        _attn_kernel,
        grid=(BATCH, N_CHUNKS // ATTN_QB),
        in_specs=[pl.BlockSpec((1, ATTN_WIDTH, ATTN_QB * CHUNK), cur),
                  edge(prev), mid, edge(nxt), edge(prev), mid, edge(nxt),
                  pl.BlockSpec((N_KV_HEADS, 1, GQA_GROUP * CHUNK), lambda b, n: (0, 0, 0))],
        out_specs=pl.BlockSpec((1, ATTN_WIDTH, ATTN_QB * CHUNK), cur),
        out_shape=jax.ShapeDtypeStruct((BATCH, ATTN_WIDTH, SEQ), BF16),
        compiler_params=pltpu.CompilerParams(vmem_limit_bytes=VMEM_LIMIT),
        name="attention",
    )(q_t, k_t, k_t, k_t, v_t, v_t, v_t, sink_rows)


def _cmul(a_r, a_i, b_r, b_i):
    return a_r * b_r - a_i * b_i, a_r * b_i + a_i * b_r


def _ssm_kernel(u_ref, rowp_ref, bt_ref, cw_ref, ct_ref, drow_ref,
                y_ref, wf_ref, wb_ref, in_ref, out_ref):
    for g in range(SSM_STEP_GROUPS):
        lam_chunk = _ssm_tables(g, rowp_ref, bt_ref, cw_ref, ct_ref, wf_ref, wb_ref, in_ref, out_ref)
        _ssm_apply(g, lam_chunk, u_ref, drow_ref, y_ref, wf_ref, wb_ref, in_ref, out_ref)


def _ssm_tables(g, rowp_ref, bt_ref, cw_ref, ct_ref, wf_ref, wb_ref, in_ref, out_ref):
    half = SSM_STATE
    top_col = lax.broadcasted_iota(jnp.int32, (LANES, 1), 0) < half

    lam_r, lam_i = rowp_ref[g, 0:1, :], rowp_ref[g, 1:2, :]
    dt = jnp.exp(rowp_ref[g, 2:3, :])
    z_r, z_i = lam_r * dt, lam_i * dt
    e1 = jnp.exp(z_r)
    lb_r, lb_i = e1 * jnp.cos(z_i), e1 * jnp.sin(z_i)
    den = lam_r * lam_r + lam_i * lam_i
    cf_r = ((lb_r - 1.0) * lam_r + lb_i * lam_i) / den
    cf_i = (lb_i * lam_r - (lb_r - 1.0) * lam_i) / den
    b_r, b_i = bt_ref[g, 0:SSM_GROUP_CH, :], bt_ref[g, SSM_GROUP_CH:, :]
    bb_r = cf_r * b_r - cf_i * b_i
    bb_i = cf_r * b_i + cf_i * b_r
    c_r, c_i = cw_ref[g, 0:SSM_GROUP_CH, :], cw_ref[g, SSM_GROUP_CH:, :]

    sq_r, sq_i = [lb_r], [lb_i]
    for _ in range(LOG2_CHUNK):
        r, i = _cmul(sq_r[-1], sq_i[-1], sq_r[-1], sq_i[-1])
        sq_r.append(r)
        sq_i.append(i)
    lc_r, lc_i = sq_r[LOG2_CHUNK], sq_i[LOG2_CHUNK]
    pad = jnp.zeros((LANES - 2 * SUBLANES, LANES), F32)
    cols = jnp.concatenate(sq_r[:SUBLANES] + sq_i[:SUBLANES] + [pad], axis=0).T
    sq_rc = [cols[:, k:k + 1] for k in range(LOG2_CHUNK)]
    sq_ic = [cols[:, SUBLANES + k:SUBLANES + k + 1] for k in range(LOG2_CHUNK)]

    j_idx = lax.broadcasted_iota(jnp.int32, (LANES, CHUNK), 1)
    one, zero = jnp.ones((LANES, CHUNK), F32), jnp.zeros((LANES, CHUNK), F32)
    asc_r, asc_i, dsc_r, dsc_i = one, zero, one, zero
    for k in range(LOG2_CHUNK):
        bit = (j_idx & (1 << k)) != 0
        n_r, n_i = _cmul(asc_r, asc_i, sq_rc[k], sq_ic[k])
        asc_r, asc_i = jnp.where(bit, n_r, asc_r), jnp.where(bit, n_i, asc_i)
        n_r, n_i = _cmul(dsc_r, dsc_i, sq_rc[k], sq_ic[k])
        dsc_r, dsc_i = jnp.where(bit, dsc_r, n_r), jnp.where(bit, dsc_i, n_i)
    q_r, q_i = _cmul(jnp.where(top_col, asc_r, dsc_r), jnp.where(top_col, asc_i, dsc_i),
                     sq_rc[0], sq_ic[0])
    p_r = jnp.where(top_col, dsc_r, asc_r).T
    p_i = jnp.where(top_col, dsc_i, asc_i).T

    first = j_idx == 0
    ef_r = jnp.where(top_col, asc_r, jnp.where(first, 1.0, 0.0))
    ef_i = jnp.where(top_col, asc_i, 0.0)
    eb_r = jnp.where(top_col, 0.0, jnp.where(first, 0.0, q_r))
    eb_i = jnp.where(top_col, 0.0, jnp.where(first, 0.0, q_i))
    e_r = jnp.concatenate([eb_r, ef_r], axis=1)
    e_i = jnp.concatenate([eb_i, ef_i], axis=1)
    cb_r = jnp.concatenate([c_r * bb_r[h:h + 1] - c_i * bb_i[h:h + 1]
                            for h in range(SSM_GROUP_CH)], axis=0)
    cb_i = jnp.concatenate([c_r * bb_i[h:h + 1] + c_i * bb_r[h:h + 1]
                            for h in range(SSM_GROUP_CH)], axis=0)
    hi = lax.Precision.HIGHEST
    ktab = (jnp.dot(cb_r, e_r, precision=hi, preferred_element_type=F32)
            - jnp.dot(cb_i, e_i, precision=hi, preferred_element_type=F32))
    kb, kf = ktab[:, :CHUNK], ktab[:, CHUNK:]
    kb1 = pltpu.roll(kb, 1, 1)
    kf1 = jnp.where(lax.broadcasted_iota(jnp.int32, (1, CHUNK), 1) == 0, kb1,
                    pltpu.roll(kf, 1, 1))
    bf16_bits = lambda t: lax.bitcast_convert_type(t.astype(BF16).astype(F32), jnp.int32)
    pack = lambda even, odd: bf16_bits(odd) | lax.shift_right_logical(bf16_bits(even), 16)
    wf_ref[g] = pack(kf, kf1)
    wb_ref[g] = pack(kb, kb1)

    for h in range(SSM_GROUP_CH):
        rows = slice(h * CHUNK, (h + 1) * CHUNK)
        in_ref[g, rows, 0:LANES] = (p_r * bb_r[h:h + 1] - p_i * bb_i[h:h + 1]).astype(BF16)
        in_ref[g, rows, LANES:] = (p_r * bb_i[h:h + 1] + p_i * bb_r[h:h + 1]).astype(BF16)

    for hp in range(SSM_GROUP_CH):
        cols = slice(hp * CHUNK, (hp + 1) * CHUNK)
        cc_r = ct_ref[g, :, hp:hp + 1]
        cc_i = ct_ref[g, :, SSM_GROUP_CH + hp:SSM_GROUP_CH + hp + 1]
        out_ref[g, 0:LANES, cols] = (cc_r * q_r - cc_i * q_i).astype(BF16)
        out_ref[g, LANES:, cols] = (-(cc_r * q_i + cc_i * q_r)).astype(BF16)
    return lc_r, lc_i


def _ssm_apply(g, lam_chunk, u_ref, drow_ref, y_ref, wf_ref, wb_ref, in_ref, out_ref):
    lc_r, lc_i = lam_chunk
    lo_row = lax.broadcasted_iota(jnp.int32, (1, LANES), 1) < SSM_STATE
    row0 = g * SSM_GROUP_CH * BATCH
    a = jnp.concatenate(
        [u_ref[:, row0 + h * BATCH:row0 + (h + 1) * BATCH, :].reshape(ROWS, CHUNK)
         for h in range(SSM_GROUP_CH)], axis=1)
    ab = a.astype(BF16)

    pair = lax.broadcasted_iota(jnp.int32, (CHUNK // 2, CHUNK), 0)
    lane = lax.broadcasted_iota(jnp.int32, (CHUNK // 2, CHUNK), 1)
    upper = lane + 2 * pair < CHUNK
    per_slab = MXU_DEPTH // CHUNK

    def toeplitz_slab(slab):
        col_blocks = []
        for h in range(SSM_GROUP_CH):
            blocks = []
            for hp in range(slab * per_slab, (slab + 1) * per_slab):
                row = pl.ds(h * SSM_GROUP_CH + hp, 1)
                w = jnp.where(upper,
                              jnp.broadcast_to(wf_ref[g, row, :], (CHUNK // 2, CHUNK)),
                              jnp.broadcast_to(wb_ref[g, row, :], (CHUNK // 2, CHUNK)))
                w = pltpu.roll(w, 0, 1, stride=2, stride_axis=0)
                blocks.append(pltpu.bitcast(w, BF16))
            col_blocks.append(jnp.concatenate(blocks, axis=1))
        return jnp.concatenate(col_blocks, axis=0)

    e = _dot(ab, in_ref[g])
    s_r = jnp.zeros((BATCH, LANES), F32)
    s_i = jnp.zeros((BATCH, LANES), F32)
    hist_r, hist_i = [], []
    for i in range(N_CHUNKS):
        hist_r.append(s_r)
        hist_i.append(s_i)
        fr = slice(i * BATCH, (i + 1) * BATCH)
        br = slice((N_CHUNKS - 1 - i) * BATCH, (N_CHUNKS - i) * BATCH)
        in_r = jnp.where(lo_row, e[fr, 0:LANES], e[br, 0:LANES])
        in_i = jnp.where(lo_row, e[fr, LANES:], e[br, LANES:])
        s_r, s_i = (s_r * lc_r - s_i * lc_i + in_r, s_r * lc_i + s_i * lc_r + in_i)
    states = jnp.concatenate(
        [jnp.concatenate([jnp.where(lo_row, hist_r[c], hist_r[N_CHUNKS - 1 - c]),
                          jnp.where(lo_row, hist_i[c], hist_i[N_CHUNKS - 1 - c])], axis=1)
         for c in range(N_CHUNKS)], axis=0).astype(BF16)

    for slab in range(SSM_GROUP_CH // per_slab):
        cols = slice(slab * MXU_DEPTH, (slab + 1) * MXU_DEPTH)
        y = (_dot(ab, toeplitz_slab(slab)) + _dot(states, out_ref[g, :, cols])
             + a[:, cols] * drow_ref[g][:, cols])
        for k in range(per_slab):
            hp = slab * per_slab + k
            y_ref[:, row0 + hp * BATCH:row0 + (hp + 1) * BATCH, :] = (
                y[:, k * CHUNK:(k + 1) * CHUNK].reshape(N_CHUNKS, BATCH, CHUNK))


def _ssm(u4, rowp, bt, cw, ct, drow):
    ng = SSM_STEP_GROUPS
    grp = lambda shape: pl.BlockSpec((ng,) + shape, lambda g: (g, 0, 0))
    data = pl.BlockSpec((N_CHUNKS, ng * SSM_GROUP_CH * BATCH, CHUNK), lambda g: (0, g, 0))
    pairs = SSM_GROUP_CH * SSM_GROUP_CH
    return pl.pallas_call(
        _ssm_kernel,
        grid=(SSM_GROUPS // ng,),
        in_specs=[data, grp((SUBLANES, LANES)),
                  grp((2 * SSM_GROUP_CH, LANES)), grp((2 * SSM_GROUP_CH, LANES)),
                  grp((LANES, 2 * SSM_GROUP_CH)), grp((1, FLAT))],
        out_specs=data,
        out_shape=jax.ShapeDtypeStruct((N_CHUNKS, SSM_WIDTH * BATCH, CHUNK), F32),
        scratch_shapes=[pltpu.VMEM((ng, pairs, CHUNK), jnp.int32),
                        pltpu.VMEM((ng, pairs, CHUNK), jnp.int32),
                        pltpu.VMEM((ng, FLAT, 2 * LANES), BF16),
                        pltpu.VMEM((ng, 2 * LANES, FLAT), BF16)],
        compiler_params=pltpu.CompilerParams(vmem_limit_bytes=VMEM_LIMIT),
        name="ssm",
    )(u4, rowp, bt, cw, ct, drow)


MERGE_B = BATCH
MERGE_SPLIT = 4


def _merge_kernel(y_ref, at_ref, ga_ref, gs_ref, x_ref,
                  wglu_ref, wab_ref, wsb_ref, wout_ref, o_ref):
    b0 = pl.program_id(1) * MERGE_B
    nb = MERGE_B // MERGE_SPLIT
    tiles = [range(t * nb, (t + 1) * nb) for t in range(MERGE_SPLIT)]
    cat = lambda ref, bs: jnp.concatenate([ref[b] for b in bs], axis=1)
    y_t = [jnp.concatenate([y_ref[0, pl.ds(b0 + b, SSM_WIDTH, stride=BATCH), :] for b in bs],
                           axis=1) for bs in tiles]
    a_t = [_dot(wab_ref[...], cat(at_ref, bs)) for bs in tiles]
    yg = [0.5 * y * (1.0 + jnp.tanh(math.sqrt(2.0 / math.pi) * (y + 0.044715 * y ** 3)))
          for y in y_t]
    z_t = [_dot(wglu_ref[...], g.astype(BF16)) for g in yg]
    ssm_t = [g * _sigmoid(z) for g, z in zip(yg, z_t)]
    s_t = [_dot(wsb_ref[...], s.astype(BF16)) for s in ssm_t]
    merged = [cat(ga_ref, bs).astype(F32) * a + cat(gs_ref, bs).astype(F32) * s
              for bs, a, s in zip(tiles, a_t, s_t)]
    for bs, m in zip(tiles, merged):
        mb = m.astype(BF16)
        for r in range(0, D_MODEL, MXU_DEPTH):
            o = _dot(wout_ref[r:r + MXU_DEPTH, :], mb).T
            o_ref[bs.start:bs.stop, :, r:r + MXU_DEPTH] = (
                x_ref[bs.start:bs.stop, :, r:r + MXU_DEPTH] + o.reshape(nb, CHUNK, MXU_DEPTH))


def _merge(y4, attn_t, ga_t, gs_t, x, wglu_t, wab_t, wsb_t, wout_t):
    tspec = lambda rows: pl.BlockSpec((MERGE_B, rows, CHUNK), lambda c, hb: (hb, 0, c))
    wspec = lambda r, k: pl.BlockSpec((r, k), lambda c, hb: (0, 0), pipeline_mode=pl.Buffered(1))
    xspec = pl.BlockSpec((MERGE_B, CHUNK, D_MODEL), lambda c, hb: (hb, c, 0))
    return pl.pallas_call(
        _merge_kernel,
        grid=(N_CHUNKS, BATCH // MERGE_B),
        in_specs=[pl.BlockSpec((1, SSM_WIDTH * BATCH, CHUNK), lambda c, hb: (c, 0, 0)),
                  tspec(ATTN_WIDTH), tspec(D_MODEL), tspec(D_MODEL), xspec,
                  wspec(SSM_WIDTH, SSM_WIDTH), wspec(D_MODEL, ATTN_WIDTH),
                  wspec(D_MODEL, SSM_WIDTH), wspec(D_MODEL, D_MODEL)],
        out_specs=xspec,
        out_shape=jax.ShapeDtypeStruct((BATCH, SEQ, D_MODEL), F32),
        compiler_params=pltpu.CompilerParams(vmem_limit_bytes=VMEM_LIMIT),
        name="merge",
    )(y4, attn_t, ga_t, gs_t, x, wglu_t, wab_t, wsb_t, wout_t)


def _ffn_kernel(x_ref, g2_ref, wg_ref, wu_ref, wd_ref, gf_ref, o_ref):
    tile = FFN_ROWS // FFN_SPLIT
    rows = [slice(t * tile, (t + 1) * tile) for t in range(FFN_SPLIT)]
    h = [_rmsnorm(x_ref[r, :], g2_ref[...]).astype(BF16) for r in rows]
    acc = [jnp.zeros((tile, D_MODEL), F32) for _ in rows]
    for j in range(D_FF // FFN_COLS):
        cols = slice(j * FFN_COLS, (j + 1) * FFN_COLS)
        for t in range(FFN_SPLIT):
            gate = _dot(h[t], wg_ref[:, cols])
            up = _dot(h[t], wu_ref[:, cols])
            act = (gate * _sigmoid(gate) * up).astype(BF16)
            acc[t] = acc[t] + _dot(act, wd_ref[cols, :])
    for t, r in enumerate(rows):
        o_ref[r, :] = _rmsnorm(x_ref[r, :] + acc[t], gf_ref[...])


def _ffn(x, g2, wg, wu, wd, gf):
    n_tok = BATCH * SEQ
    row = pl.BlockSpec((FFN_ROWS, D_MODEL), lambda i: (i, 0))
    vec = pl.BlockSpec((1, D_MODEL), lambda i: (0, 0))
    wspec = lambda r, k: pl.BlockSpec((r, k), lambda i: (0, 0), pipeline_mode=pl.Buffered(1))
    return pl.pallas_call(
        _ffn_kernel,
        grid=(n_tok // FFN_ROWS,),
        in_specs=[row, vec, wspec(D_MODEL, D_FF), wspec(D_MODEL, D_FF),
                  wspec(D_FF, D_MODEL), vec],
        out_specs=row,
        out_shape=jax.ShapeDtypeStruct((n_tok, D_MODEL), F32),
        compiler_params=pltpu.CompilerParams(vmem_limit_bytes=VMEM_LIMIT),
        name="ffn",
    )(x, g2, wg, wu, wd, gf)


def _ssm_param_layouts(lam_re, lam_im, log_dt, b_re, b_im, c_re, c_im, d):
    both = lambda t: jnp.concatenate([t[0], t[1]], axis=-1)
    ldt = jnp.broadcast_to(log_dt[:, :, None], (2, SSM_GROUPS, SSM_STATE))
    rows = jnp.stack([both(lam_re), both(lam_im), both(ldt)], axis=1)
    rowp = jnp.pad(rows, ((0, 0), (0, SUBLANES - 3), (0, 0)))
    bt =jnp.concatenate([both(jnp.swapaxes(b_re, 2, 3)),
                          both(jnp.swapaxes(b_im, 2, 3))], axis=1)
    cw = jnp.concatenate([jnp.tile(c_re, (1, 1, 2)), jnp.tile(c_im, (1, 1, 2))], axis=1)
    ct = jnp.swapaxes(cw, 1, 2)
    drow = jnp.repeat(d, CHUNK, axis=-1)[:, None, :]
    return rowp, bt, cw, ct, drow


def kernel(x, norm1_g, w_in, attn_sink, ssm_lambda_re, ssm_lambda_im, ssm_log_dt, ssm_b_re, ssm_b_im, ssm_c_re, ssm_c_im, ssm_d, w_glu, w_attn_branch, w_ssm_branch, w_out, norm2_g, w_ffn_gate, w_ffn_up, w_ffn_down, norm_f_g):
    pos = jnp.arange(SEQ, dtype=F32)
    inv_freq = ROPE_THETA ** (-jnp.arange(0, ROPE_DIM, 2, dtype=F32) / ROPE_DIM)
    ang = inv_freq[:, None] * pos[None, :]
    cos_t, sin_t = jnp.cos(ang), jnp.sin(ang)
    sink_rows = jnp.repeat(attn_sink[0].astype(F32), CHUNK).reshape(
        N_KV_HEADS, 1, GQA_GROUP * CHUNK)

    q_t, k_t, v_t, u4, ga_t, gs_t = _in_proj(
        x, norm1_g[0][None, :], w_in[0].T.astype(BF16), cos_t, sin_t)
    attn_t = _attention(q_t, k_t, v_t, sink_rows)
    y4 = _ssm(u4, *_ssm_param_layouts(
        ssm_lambda_re[0], ssm_lambda_im[0], ssm_log_dt[0], ssm_b_re[0], ssm_b_im[0],
        ssm_c_re[0], ssm_c_im[0], ssm_d[0]))
    x1 = _merge(y4, attn_t, ga_t, gs_t, x,
                w_glu[0].T.astype(BF16), w_attn_branch[0].T.astype(BF16),
                w_ssm_branch[0].T.astype(BF16), w_out[0].T.astype(BF16))
    out = _ffn(x1.reshape(BATCH * SEQ, D_MODEL), norm2_g[0][None, :],
               w_ffn_gate[0].astype(BF16), w_ffn_up[0].astype(BF16),
               w_ffn_down[0].astype(BF16), norm_f_g[None, :])
    return out.reshape(BATCH, SEQ, D_MODEL)
```

```python
import math

import jax
import jax.numpy as jnp
from jax import lax
from jax.experimental import pallas as pl
from jax.experimental.pallas import tpu as pltpu

D_MODEL = 1024
BATCH = 8
SEQ = 4096
N_HEADS = 8
N_KV_HEADS = 2
HEAD_DIM = 64
GQA_GROUP = N_HEADS // N_KV_HEADS
ATTN_WIDTH = N_HEADS * HEAD_DIM
KV_WIDTH = N_KV_HEADS * HEAD_DIM
ROPE_DIM = HEAD_DIM // 4
ROPE_HALF = ROPE_DIM // 2
ROPE_THETA = 500000.0
WINDOW = 128
SSM_GROUP_CH = 16
SSM_WIDTH = D_MODEL // 2
SSM_GROUPS = SSM_WIDTH // SSM_GROUP_CH
SSM_STATE = 64
D_FF = 2816
IN_COLS = ATTN_WIDTH + 2 * KV_WIDTH + SSM_WIDTH + 2 * D_MODEL
RMS_EPS = 1e-6
NEG_INF = -1e30

LANES = 128
SUBLANES = 8
MXU_DEPTH = 256
CHUNK = LANES
LOG2_CHUNK = CHUNK.bit_length() - 1
N_CHUNKS = SEQ // CHUNK
ROWS = N_CHUNKS * BATCH
FLAT = SSM_GROUP_CH * CHUNK
SSM_STEP_GROUPS = 2
ATTN_QB = 8
LOG2E = math.log2(math.e)
SUM_ROWS = 16
FFN_ROWS = 1024
FFN_SPLIT = 2
FFN_COLS = 256
VMEM_LIMIT = 58 * 1024 * 1024

assert BATCH == SUBLANES and WINDOW == CHUNK and D_FF % FFN_COLS == 0

_Q0, _K0, _V0, _U0, _GA0, _GS0 = 0, 512, 640, 768, 1280, 2304

BF16 = jnp.bfloat16
F32 = jnp.float32


def _sigmoid(x):
    return 0.5 * jnp.tanh(0.5 * x) + 0.5


def _rmsnorm(x, g):
    return x * lax.rsqrt(jnp.mean(x * x, axis=-1, keepdims=True) + RMS_EPS) * g


def _dot(a, b):
    return jnp.dot(a, b, preferred_element_type=F32)


def _lane_cat(ref, n):
    return jnp.concatenate([ref[i] for i in range(n)], axis=1)


def _in_proj_kernel(x_ref, g_ref, w_ref, cos_ref, sin_ref,
                    q_ref, k_ref, v_ref, u_ref, ga_ref, gs_ref):
    nb = MXU_DEPTH // CHUNK
    h = [_rmsnorm(x_ref[b:b + nb].reshape(MXU_DEPTH, D_MODEL), g_ref[...]).astype(BF16)
         for b in range(0, BATCH, nb)]

    def proj_t(r0, r1):
        return jnp.concatenate(
            [lax.dot_general(w_ref[r0:r1, :], t, (((1,), (1,)), ((), ())),
                             preferred_element_type=F32) for t in h], axis=1)

    cos = jnp.concatenate([cos_ref[...]] * BATCH, axis=1)
    sin = jnp.concatenate([sin_ref[...]] * BATCH, axis=1)

    def rope_t(t, n_heads):
        parts = []
        for hd in range(n_heads):
            r1 = t[hd * HEAD_DIM:hd * HEAD_DIM + ROPE_HALF]
            r2 = t[hd * HEAD_DIM + ROPE_HALF:hd * HEAD_DIM + ROPE_DIM]
            parts += [r1 * cos - r2 * sin, r2 * cos + r1 * sin,
                      t[hd * HEAD_DIM + ROPE_DIM:(hd + 1) * HEAD_DIM]]
        return jnp.concatenate(parts, axis=0)

    def put(ref, r0, t):
        for b in range(BATCH):
            ref[b, r0:r0 + t.shape[0], :] = t[:, b * CHUNK:(b + 1) * CHUNK].astype(ref.dtype)

    slab = MXU_DEPTH
    for r in range(0, ATTN_WIDTH, slab):
        t = proj_t(_Q0 + r, _Q0 + r + slab) * (HEAD_DIM ** -0.5 * LOG2E)
        put(q_ref, r, rope_t(t, slab // HEAD_DIM))
    kv = proj_t(_K0, _U0)
    put(k_ref, 0, rope_t(kv[:KV_WIDTH], N_KV_HEADS))
    put(v_ref, 0, kv[KV_WIDTH:])
    for r in range(0, SSM_WIDTH, slab):
        u_t = proj_t(_U0 + r, _U0 + r + slab)
        for b in range(BATCH):
            u_ref[0, pl.ds(r * BATCH + b, slab, stride=BATCH), :] = (
                u_t[:, b * CHUNK:(b + 1) * CHUNK])
    for r in range(0, D_MODEL, slab):
        put(ga_ref, r, _sigmoid(proj_t(_GA0 + r, _GA0 + r + slab)))
    for r in range(0, D_MODEL, slab):
        put(gs_ref, r, _sigmoid(proj_t(_GS0 + r, _GS0 + r + slab)))


def _in_proj(x, g1, w_t, cos_t, sin_t):
    tok = lambda rows, dt: jax.ShapeDtypeStruct((BATCH, rows, SEQ), dt)
    tspec = lambda rows: pl.BlockSpec((BATCH, rows, CHUNK), lambda c: (0, 0, c))
    return pl.pallas_call(
        _in_proj_kernel,
        grid=(N_CHUNKS,),
        in_specs=[
            pl.BlockSpec((BATCH, CHUNK, D_MODEL), lambda c: (0, c, 0)),
            pl.BlockSpec((1, D_MODEL), lambda c: (0, 0)),
            pl.BlockSpec((IN_COLS, D_MODEL), lambda c: (0, 0), pipeline_mode=pl.Buffered(1)),
            pl.BlockSpec((ROPE_HALF, CHUNK), lambda c: (0, c)),
            pl.BlockSpec((ROPE_HALF, CHUNK), lambda c: (0, c)),
        ],
        out_specs=[
            tspec(ATTN_WIDTH), tspec(KV_WIDTH), tspec(KV_WIDTH),
            pl.BlockSpec((1, SSM_WIDTH * BATCH, CHUNK), lambda c: (c, 0, 0)),
            tspec(D_MODEL), tspec(D_MODEL),
        ],
        out_shape=[
            tok(ATTN_WIDTH, BF16), tok(KV_WIDTH, BF16), tok(KV_WIDTH, BF16),
            jax.ShapeDtypeStruct((N_CHUNKS, SSM_WIDTH * BATCH, CHUNK), F32),
            tok(D_MODEL, BF16), tok(D_MODEL, BF16),
        ],
        compiler_params=pltpu.CompilerParams(vmem_limit_bytes=VMEM_LIMIT),
        name="in_proj",
    )(x, g1, w_t, cos_t, sin_t)


def _attn_kernel(q_ref, kp_ref, kc_ref, kn_ref, vp_ref, vc_ref, vn_ref, sink_ref, o_ref):
    n = pl.program_id(1)
    k_t = jnp.concatenate([kp_ref[0], kc_ref[0], kn_ref[0]], axis=1)
    v_t = jnp.concatenate([vp_ref[0], vc_ref[0], vn_ref[0]], axis=1)
    keys = k_t.astype(F32).T.astype(BF16)
    kk = lax.broadcasted_iota(jnp.int32, (CHUNK, CHUNK), 0)
    qq = lax.broadcasted_iota(jnp.int32, (CHUNK, CHUNK), 1)
    in_prev = jnp.concatenate([kk >= qq] * GQA_GROUP, axis=1)
    in_next = jnp.concatenate([kk <= qq] * GQA_GROUP, axis=1)
    zeros = jnp.zeros((HEAD_DIM, GQA_GROUP * CHUNK), BF16)
    ones = jnp.ones((SUM_ROWS, 3 * CHUNK), BF16)
    cases = [(i, j) for i in range(ATTN_QB) for j in range(N_KV_HEADS)]
    scores = []
    for i, j in cases:
        q_t = jnp.concatenate(
            [q_ref[0, (j * GQA_GROUP + g) * HEAD_DIM:(j * GQA_GROUP + g + 1) * HEAD_DIM,
                   i * CHUNK:(i + 1) * CHUNK] for g in range(GQA_GROUP)], axis=1)
        q_pad = jnp.concatenate([q_t, zeros] if j == 0 else [zeros, q_t], axis=0)
        scores.append(_dot(keys[i * CHUNK:(i + 3) * CHUNK], q_pad))
    for (i, j), s in zip(cases, scores):
        prev_ok = in_prev & (n > 0) if i == 0 else in_prev
        next_ok = in_next & (n < N_CHUNKS // ATTN_QB - 1) if i == ATTN_QB - 1 else in_next
        s = jnp.concatenate([jnp.where(prev_ok, s[:CHUNK], NEG_INF), s[CHUNK:2 * CHUNK],
                             jnp.where(next_ok, s[2 * CHUNK:], NEG_INF)], axis=0)
        sink = sink_ref[j] * LOG2E
        m = jnp.maximum(jnp.max(s, axis=0, keepdims=True), sink)
        p = jnp.exp2(s - m).astype(BF16)
        v_aug = jnp.concatenate(
            [v_t[j * HEAD_DIM:(j + 1) * HEAD_DIM, i * CHUNK:(i + 3) * CHUNK], ones], axis=0)
        o_aug = _dot(v_aug, p)
        o = o_aug[:HEAD_DIM] / (o_aug[HEAD_DIM:HEAD_DIM + 1] + jnp.exp2(sink - m))
        for g in range(GQA_GROUP):
            hd = j * GQA_GROUP + g
            o_ref[0, hd * HEAD_DIM:(hd + 1) * HEAD_DIM, i * CHUNK:(i + 1) * CHUNK] = (
                o[:, g * CHUNK:(g + 1) * CHUNK].astype(o_ref.dtype))


def _attention(q_t, k_t, v_t, sink_rows):
    prev = lambda b, n: (b, 0, jnp.maximum(n * ATTN_QB - 1, 0))
    cur = lambda b, n: (b, 0, n)
    nxt = lambda b, n: (b, 0, jnp.minimum((n + 1) * ATTN_QB, N_CHUNKS - 1))
    edge = lambda im: pl.BlockSpec((1, KV_WIDTH, CHUNK), im)
    mid = pl.BlockSpec((1, KV_WIDTH, ATTN_QB * CHUNK), cur)
    return pl.pallas_call(
        _attn_kernel,
        grid=(BATCH, N_CHUNKS // ATTN_QB),
        in_specs=[pl.BlockSpec((1, ATTN_WIDTH, ATTN_QB * CHUNK), cur),
                  edge(prev), mid, edge(nxt), edge(prev), mid, edge(nxt),
                  pl.BlockSpec((N_KV_HEADS, 1, GQA_GROUP * CHUNK), lambda b, n: (0, 0, 0))],
        out_specs=pl.BlockSpec((1, ATTN_WIDTH, ATTN_QB * CHUNK), cur),
        out_shape=jax.ShapeDtypeStruct((BATCH, ATTN_WIDTH, SEQ), BF16),
        compiler_params=pltpu.CompilerParams(vmem_limit_bytes=VMEM_LIMIT),
        name="attention",
    )(q_t, k_t, k_t, k_t, v_t, v_t, v_t, sink_rows)


def _cmul(a_r, a_i, b_r, b_i):
    return a_r * b_r - a_i * b_i, a_r * b_i + a_i * b_r


def _ssm_kernel(u_ref, rowp_ref, bt_ref, cw_ref, ct_ref, drow_ref,
                y_ref, wf_ref, wb_ref, in_ref, out_ref):
    for g in range(SSM_STEP_GROUPS):
        lam_chunk = _ssm_tables(g, rowp_ref, bt_ref, cw_ref, ct_ref, wf_ref, wb_ref, in_ref, out_ref)
        _ssm_apply(g, lam_chunk, u_ref, drow_ref, y_ref, wf_ref, wb_ref, in_ref, out_ref)


def _ssm_tables(g, rowp_ref, bt_ref, cw_ref, ct_ref, wf_ref, wb_ref, in_ref, out_ref):
    half = SSM_STATE
    top_col = lax.broadcasted_iota(jnp.int32, (LANES, 1), 0) < half

    lam_r, lam_i = rowp_ref[g, 0:1, :], rowp_ref[g, 1:2, :]
    dt = jnp.exp(rowp_ref[g, 2:3, :])
    z_r, z_i = lam_r * dt, lam_i * dt
    e1 = jnp.exp(z_r)
    lb_r, lb_i = e1 * jnp.cos(z_i), e1 * jnp.sin(z_i)
    den = lam_r * lam_r + lam_i * lam_i
    cf_r = ((lb_r - 1.0) * lam_r + lb_i * lam_i) / den
    cf_i = (lb_i * lam_r - (lb_r - 1.0) * lam_i) / den
    b_r, b_i = bt_ref[g, 0:SSM_GROUP_CH, :], bt_ref[g, SSM_GROUP_CH:, :]
    bb_r = cf_r * b_r - cf_i * b_i
    bb_i = cf_r * b_i + cf_i * b_r
    c_r, c_i = cw_ref[g, 0:SSM_GROUP_CH, :], cw_ref[g, SSM_GROUP_CH:, :]

    sq_r, sq_i = [lb_r], [lb_i]
    for _ in range(LOG2_CHUNK):
        r, i = _cmul(sq_r[-1], sq_i[-1], sq_r[-1], sq_i[-1])
        sq_r.append(r)
        sq_i.append(i)
    lc_r, lc_i = sq_r[LOG2_CHUNK], sq_i[LOG2_CHUNK]
    pad = jnp.zeros((LANES - 2 * SUBLANES, LANES), F32)
    cols = jnp.concatenate(sq_r[:SUBLANES] + sq_i[:SUBLANES] + [pad], axis=0).T
    sq_rc = [cols[:, k:k + 1] for k in range(LOG2_CHUNK)]
    sq_ic = [cols[:, SUBLANES + k:SUBLANES + k + 1] for k in range(LOG2_CHUNK)]

    j_idx = lax.broadcasted_iota(jnp.int32, (LANES, CHUNK), 1)
    one, zero = jnp.ones((LANES, CHUNK), F32), jnp.zeros((LANES, CHUNK), F32)
    asc_r, asc_i, dsc_r, dsc_i = one, zero, one, zero
    for k in range(LOG2_CHUNK):
        bit = (j_idx & (1 << k)) != 0
        n_r, n_i = _cmul(asc_r, asc_i, sq_rc[k], sq_ic[k])
        asc_r, asc_i = jnp.where(bit, n_r, asc_r), jnp.where(bit, n_i, asc_i)
        n_r, n_i = _cmul(dsc_r, dsc_i, sq_rc[k], sq_ic[k])
        dsc_r, dsc_i = jnp.where(bit, dsc_r, n_r), jnp.where(bit, dsc_i, n_i)
    q_r, q_i = _cmul(jnp.where(top_col, asc_r, dsc_r), jnp.where(top_col, asc_i, dsc_i),
                     sq_rc[0], sq_ic[0])
    p_r = jnp.where(top_col, dsc_r, asc_r).T
    p_i = jnp.where(top_col, dsc_i, asc_i).T

    first = j_idx == 0
    ef_r = jnp.where(top_col, asc_r, jnp.where(first, 1.0, 0.0))
    ef_i = jnp.where(top_col, asc_i, 0.0)
    eb_r = jnp.where(top_col, 0.0, jnp.where(first, 0.0, q_r))
    eb_i = jnp.where(top_col, 0.0, jnp.where(first, 0.0, q_i))
    e_r = jnp.concatenate([eb_r, ef_r], axis=1)
    e_i = jnp.concatenate([eb_i, ef_i], axis=1)
    cb_r = jnp.concatenate([c_r * bb_r[h:h + 1] - c_i * bb_i[h:h + 1]
                            for h in range(SSM_GROUP_CH)], axis=0)
    cb_i = jnp.concatenate([c_r * bb_i[h:h + 1] + c_i * bb_r[h:h + 1]
                            for h in range(SSM_GROUP_CH)], axis=0)
    hi = lax.Precision.HIGHEST
    ktab = (jnp.dot(cb_r, e_r, precision=hi, preferred_element_type=F32)
            - jnp.dot(cb_i, e_i, precision=hi, preferred_element_type=F32))
    kb, kf = ktab[:, :CHUNK], ktab[:, CHUNK:]
    kb1 = pltpu.roll(kb, 1, 1)
    kf1 = jnp.where(lax.broadcasted_iota(jnp.int32, (1, CHUNK), 1) == 0, kb1,
                    pltpu.roll(kf, 1, 1))
    bf16_bits = lambda t: lax.bitcast_convert_type(t.astype(BF16).astype(F32), jnp.int32)
    pack = lambda even, odd: bf16_bits(odd) | lax.shift_right_logical(bf16_bits(even), 16)
    wf_ref[g] = pack(kf, kf1)
    wb_ref[g] = pack(kb, kb1)

    for h in range(SSM_GROUP_CH):
        rows = slice(h * CHUNK, (h + 1) * CHUNK)
        in_ref[g, rows, 0:LANES] = (p_r * bb_r[h:h + 1] - p_i * bb_i[h:h + 1]).astype(BF16)
        in_ref[g, rows, LANES:] = (p_r * bb_i[h:h + 1] + p_i * bb_r[h:h + 1]).astype(BF16)

    for hp in range(SSM_GROUP_CH):
        cols = slice(hp * CHUNK, (hp + 1) * CHUNK)
        cc_r = ct_ref[g, :, hp:hp + 1]
        cc_i = ct_ref[g, :, SSM_GROUP_CH + hp:SSM_GROUP_CH + hp + 1]
        out_ref[g, 0:LANES, cols] = (cc_r * q_r - cc_i * q_i).astype(BF16)
        out_ref[g, LANES:, cols] = (-(cc_r * q_i + cc_i * q_r)).astype(BF16)
    return lc_r, lc_i


def _ssm_apply(g, lam_chunk, u_ref, drow_ref, y_ref, wf_ref, wb_ref, in_ref, out_ref):
    lc_r, lc_i = lam_chunk
    lo_row = lax.broadcasted_iota(jnp.int32, (1, LANES), 1) < SSM_STATE
    row0 = g * SSM_GROUP_CH * BATCH
    a = jnp.concatenate(
        [u_ref[:, row0 + h * BATCH:row0 + (h + 1) * BATCH, :].reshape(ROWS, CHUNK)
         for h in range(SSM_GROUP_CH)], axis=1)
    ab = a.astype(BF16)

    pair = lax.broadcasted_iota(jnp.int32, (CHUNK // 2, CHUNK), 0)
    lane = lax.broadcasted_iota(jnp.int32, (CHUNK // 2, CHUNK), 1)
    upper = lane + 2 * pair < CHUNK
    per_slab = MXU_DEPTH // CHUNK

    def toeplitz_slab(slab):
        col_blocks = []
        for h in range(SSM_GROUP_CH):
            blocks = []
            for hp in range(slab * per_slab, (slab + 1) * per_slab):
                row = pl.ds(h * SSM_GROUP_CH + hp, 1)
                w = jnp.where(upper,
                              jnp.broadcast_to(wf_ref[g, row, :], (CHUNK // 2, CHUNK)),
                              jnp.broadcast_to(wb_ref[g, row, :], (CHUNK // 2, CHUNK)))
                w = pltpu.roll(w, 0, 1, stride=2, stride_axis=0)
                blocks.append(pltpu.bitcast(w, BF16))
            col_blocks.append(jnp.concatenate(blocks, axis=1))
        return jnp.concatenate(col_blocks, axis=0)

    e = _dot(ab, in_ref[g])
    s_r = jnp.zeros((BATCH, LANES), F32)
    s_i = jnp.zeros((BATCH, LANES), F32)
    hist_r, hist_i = [], []
    for i in range(N_CHUNKS):
        hist_r.append(s_r)
        hist_i.append(s_i)
        fr = slice(i * BATCH, (i + 1) * BATCH)
        br = slice((N_CHUNKS - 1 - i) * BATCH, (N_CHUNKS - i) * BATCH)
        in_r = jnp.where(lo_row, e[fr, 0:LANES], e[br, 0:LANES])
        in_i = jnp.where(lo_row, e[fr, LANES:], e[br, LANES:])
        s_r, s_i = (s_r * lc_r - s_i * lc_i + in_r, s_r * lc_i + s_i * lc_r + in_i)
    states = jnp.concatenate(
        [jnp.concatenate([jnp.where(lo_row, hist_r[c], hist_r[N_CHUNKS - 1 - c]),
                          jnp.where(lo_row, hist_i[c], hist_i[N_CHUNKS - 1 - c])], axis=1)
         for c in range(N_CHUNKS)], axis=0).astype(BF16)

    for slab in range(SSM_GROUP_CH // per_slab):
        cols = slice(slab * MXU_DEPTH, (slab + 1) * MXU_DEPTH)
        y = (_dot(ab, toeplitz_slab(slab)) + _dot(states, out_ref[g, :, cols])
             + a[:, cols] * drow_ref[g][:, cols])
        for k in range(per_slab):
            hp = slab * per_slab + k
            y_ref[:, row0 + hp * BATCH:row0 + (hp + 1) * BATCH, :] = (
                y[:, k * CHUNK:(k + 1) * CHUNK].reshape(N_CHUNKS, BATCH, CHUNK))


def _ssm(u4, rowp, bt, cw, ct, drow):
    ng = SSM_STEP_GROUPS
    grp = lambda shape: pl.BlockSpec((ng,) + shape, lambda g: (g, 0, 0))
    data = pl.BlockSpec((N_CHUNKS, ng * SSM_GROUP_CH * BATCH, CHUNK), lambda g: (0, g, 0))
    pairs = SSM_GROUP_CH * SSM_GROUP_CH
    return pl.pallas_call(
        _ssm_kernel,
        grid=(SSM_GROUPS // ng,),
        in_specs=[data, grp((SUBLANES, LANES)),
                  grp((2 * SSM_GROUP_CH, LANES)), grp((2 * SSM_GROUP_CH, LANES)),
                  grp((LANES, 2 * SSM_GROUP_CH)), grp((1, FLAT))],
        out_specs=data,
        out_shape=jax.ShapeDtypeStruct((N_CHUNKS, SSM_WIDTH * BATCH, CHUNK), F32),
        scratch_shapes=[pltpu.VMEM((ng, pairs, CHUNK), jnp.int32),
                        pltpu.VMEM((ng, pairs, CHUNK), jnp.int32),
                        pltpu.VMEM((ng, FLAT, 2 * LANES), BF16),
                        pltpu.VMEM((ng, 2 * LANES, FLAT), BF16)],
        compiler_params=pltpu.CompilerParams(vmem_limit_bytes=VMEM_LIMIT),
        name="ssm",
    )(u4, rowp, bt, cw, ct, drow)


MERGE_B = BATCH
MERGE_SPLIT = 4


def _merge_kernel(y_ref, at_ref, ga_ref, gs_ref, x_ref,
                  wglu_ref, wab_ref, wsb_ref, wout_ref, o_ref):
    b0 = pl.program_id(1) * MERGE_B
    nb = MERGE_B // MERGE_SPLIT
    tiles = [range(t * nb, (t + 1) * nb) for t in range(MERGE_SPLIT)]
    cat = lambda ref, bs: jnp.concatenate([ref[b] for b in bs], axis=1)
    y_t = [jnp.concatenate([y_ref[0, pl.ds(b0 + b, SSM_WIDTH, stride=BATCH), :] for b in bs],
                           axis=1) for bs in tiles]
    a_t = [_dot(wab_ref[...], cat(at_ref, bs)) for bs in tiles]
    yg = [0.5 * y * (1.0 + jnp.tanh(math.sqrt(2.0 / math.pi) * (y + 0.044715 * y ** 3)))
          for y in y_t]
    z_t = [_dot(wglu_ref[...], g.astype(BF16)) for g in yg]
    ssm_t = [g * _sigmoid(z) for g, z in zip(yg, z_t)]
    s_t = [_dot(wsb_ref[...], s.astype(BF16)) for s in ssm_t]
    merged = [cat(ga_ref, bs).astype(F32) * a + cat(gs_ref, bs).astype(F32) * s
              for bs, a, s in zip(tiles, a_t, s_t)]
    for bs, m in zip(tiles, merged):
        mb = m.astype(BF16)
        for r in range(0, D_MODEL, MXU_DEPTH):
            o = _dot(wout_ref[r:r + MXU_DEPTH, :], mb).T
            o_ref[bs.start:bs.stop, :, r:r + MXU_DEPTH] = (
                x_ref[bs.start:bs.stop, :, r:r + MXU_DEPTH] + o.reshape(nb, CHUNK, MXU_DEPTH))


def _merge(y4, attn_t, ga_t, gs_t, x, wglu_t, wab_t, wsb_t, wout_t):
    tspec = lambda rows: pl.BlockSpec((MERGE_B, rows, CHUNK), lambda c, hb: (hb, 0, c))
    wspec = lambda r, k: pl.BlockSpec((r, k), lambda c, hb: (0, 0), pipeline_mode=pl.Buffered(1))
    xspec = pl.BlockSpec((MERGE_B, CHUNK, D_MODEL), lambda c, hb: (hb, c, 0))
    return pl.pallas_call(
        _merge_kernel,
        grid=(N_CHUNKS, BATCH // MERGE_B),
        in_specs=[pl.BlockSpec((1, SSM_WIDTH * BATCH, CHUNK), lambda c, hb: (c, 0, 0)),
                  tspec(ATTN_WIDTH), tspec(D_MODEL), tspec(D_MODEL), xspec,
                  wspec(SSM_WIDTH, SSM_WIDTH), wspec(D_MODEL, ATTN_WIDTH),
                  wspec(D_MODEL, SSM_WIDTH), wspec(D_MODEL, D_MODEL)],
        out_specs=xspec,
        out_shape=jax.ShapeDtypeStruct((BATCH, SEQ, D_MODEL), F32),
        compiler_params=pltpu.CompilerParams(vmem_limit_bytes=VMEM_LIMIT),
        name="merge",
    )(y4, attn_t, ga_t, gs_t, x, wglu_t, wab_t, wsb_t, wout_t)


_ATTN_CASES = [(b, j) for b in range(BATCH) for j in range(N_KV_HEADS)]


def _attn_scores(q_ref, kp_ref, kc_ref, kn_ref):
    zeros = jnp.zeros((HEAD_DIM, GQA_GROUP * CHUNK), BF16)
    keys = []
    for b in range(BATCH):
        k_t = jnp.concatenate([kp_ref[b], kc_ref[b], kn_ref[b]], axis=1)
        keys.append(k_t.astype(F32).T.astype(BF16))
    scores = []
    for b, j in _ATTN_CASES:
        q_t = jnp.concatenate(
            [q_ref[b, (j * GQA_GROUP + g) * HEAD_DIM:(j * GQA_GROUP + g + 1) * HEAD_DIM, :]
             for g in range(GQA_GROUP)], axis=1)
        q_pad = jnp.concatenate([q_t, zeros] if j == 0 else [zeros, q_t], axis=0)
        scores.append(_dot(keys[b], q_pad))
    return scores


def _attn_finish(c, scores, vp_ref, vc_ref, vn_ref, sink_ref):
    kk = lax.broadcasted_iota(jnp.int32, (CHUNK, CHUNK), 0)
    qq = lax.broadcasted_iota(jnp.int32, (CHUNK, CHUNK), 1)
    prev_ok = jnp.concatenate([kk >= qq] * GQA_GROUP, axis=1) & (c > 0)
    next_ok = jnp.concatenate([kk <= qq] * GQA_GROUP, axis=1) & (c < N_CHUNKS - 1)
    ones = jnp.ones((SUM_ROWS, 3 * CHUNK), BF16)
    vals = [jnp.concatenate([vp_ref[b], vc_ref[b], vn_ref[b]], axis=1) for b in range(BATCH)]
    heads = {}
    for (b, j), s in zip(_ATTN_CASES, scores):
        s = jnp.concatenate([jnp.where(prev_ok, s[:CHUNK], NEG_INF), s[CHUNK:2 * CHUNK],
                             jnp.where(next_ok, s[2 * CHUNK:], NEG_INF)], axis=0)
        sink = sink_ref[j] * LOG2E
        m = jnp.maximum(jnp.max(s, axis=0, keepdims=True), sink)
        p = jnp.exp2(s - m).astype(BF16)
        v_aug = jnp.concatenate([vals[b][j * HEAD_DIM:(j + 1) * HEAD_DIM], ones], axis=0)
        o_aug = _dot(v_aug, p)
        o = o_aug[:HEAD_DIM] / (o_aug[HEAD_DIM:HEAD_DIM + 1] + jnp.exp2(sink - m))
        for g in range(GQA_GROUP):
            heads[b, j * GQA_GROUP + g] = o[:, g * CHUNK:(g + 1) * CHUNK].astype(BF16)
    return [jnp.concatenate([heads[b, hd] for hd in range(N_HEADS)], axis=0)
            for b in range(BATCH)]


def _mix_kernel(q_ref, kp_ref, kc_ref, kn_ref, vp_ref, vc_ref, vn_ref, sink_ref,
                y_ref, ga_ref, gs_ref, x_ref, wglu_ref, wab_ref, wsb_ref, wout_ref, o_ref):
    scores = _attn_scores(q_ref, kp_ref, kc_ref, kn_ref)
    nb = BATCH // MERGE_SPLIT
    tiles = [range(t * nb, (t + 1) * nb) for t in range(MERGE_SPLIT)]
    cat = lambda ref, bs: jnp.concatenate([ref[b] for b in bs], axis=1)
    y_t = [jnp.concatenate([y_ref[0, pl.ds(b, SSM_WIDTH, stride=BATCH), :] for b in bs],
                           axis=1) for bs in tiles]
    yg = [0.5 * y * (1.0 + jnp.tanh(math.sqrt(2.0 / math.pi) * (y + 0.044715 * y ** 3)))
          for y in y_t]
    z_t = [_dot(wglu_ref[...], g.astype(BF16)) for g in yg]
    ssm_t = [g * _sigmoid(z) for g, z in zip(yg, z_t)]
    s_t = [_dot(wsb_ref[...], s.astype(BF16)) for s in ssm_t]
    attn = _attn_finish(pl.program_id(0), scores, vp_ref, vc_ref, vn_ref, sink_ref)
    a_t = [_dot(wab_ref[...], jnp.concatenate([attn[b] for b in bs], axis=1)) for bs in tiles]
    merged = [cat(ga_ref, bs).astype(F32) * a + cat(gs_ref, bs).astype(F32) * s
              for bs, a, s in zip(tiles, a_t, s_t)]
    for bs, m in zip(tiles, merged):
        mb = m.astype(BF16)
        for r in range(0, D_MODEL, MXU_DEPTH):
            o = _dot(wout_ref[r:r + MXU_DEPTH, :], mb).T
            o_ref[bs.start:bs.stop, :, r:r + MXU_DEPTH] = (
                x_ref[bs.start:bs.stop, :, r:r + MXU_DEPTH] + o.reshape(nb, CHUNK, MXU_DEPTH))


def _mix(q_t, k_t, v_t, sink_rows, y4, ga_t, gs_t, x, wglu_t, wab_t, wsb_t, wout_t):
    prev = lambda c: (0, 0, jnp.maximum(c - 1, 0))
    cur = lambda c: (0, 0, c)
    nxt = lambda c: (0, 0, jnp.minimum(c + 1, N_CHUNKS - 1))
    tspec = lambda rows, im=cur: pl.BlockSpec((BATCH, rows, CHUNK), im)
    wspec = lambda r, k: pl.BlockSpec((r, k), lambda c: (0, 0), pipeline_mode=pl.Buffered(1))
    xspec = pl.BlockSpec((BATCH, CHUNK, D_MODEL), lambda c: (0, c, 0))
    return pl.pallas_call(
        _mix_kernel,
        grid=(N_CHUNKS,),
        in_specs=[tspec(ATTN_WIDTH),
                  tspec(KV_WIDTH, prev), tspec(KV_WIDTH), tspec(KV_WIDTH, nxt),
                  tspec(KV_WIDTH, prev), tspec(KV_WIDTH), tspec(KV_WIDTH, nxt),
                  pl.BlockSpec((N_KV_HEADS, 1, GQA_GROUP * CHUNK), lambda c: (0, 0, 0)),
                  pl.BlockSpec((1, SSM_WIDTH * BATCH, CHUNK), lambda c: (c, 0, 0)),
                  tspec(D_MODEL), tspec(D_MODEL), xspec,
                  wspec(SSM_WIDTH, SSM_WIDTH), wspec(D_MODEL, ATTN_WIDTH),
                  wspec(D_MODEL, SSM_WIDTH), wspec(D_MODEL, D_MODEL)],
        out_specs=xspec,
        out_shape=jax.ShapeDtypeStruct((BATCH, SEQ, D_MODEL), F32),
        compiler_params=pltpu.CompilerParams(vmem_limit_bytes=VMEM_LIMIT),
        name="mix",
    )(q_t, k_t, k_t, k_t, v_t, v_t, v_t, sink_rows, y4, ga_t, gs_t, x,
      wglu_t, wab_t, wsb_t, wout_t)


def _ffn_kernel(x_ref, g2_ref, wg_ref, wu_ref, wd_ref, gf_ref, o_ref):
    tile = FFN_ROWS // FFN_SPLIT
    rows = [slice(t * tile, (t + 1) * tile) for t in range(FFN_SPLIT)]
    h = [_rmsnorm(x_ref[r, :], g2_ref[...]).astype(BF16) for r in rows]
    acc = [jnp.zeros((tile, D_MODEL), F32) for _ in rows]
    for j in range(D_FF // FFN_COLS):
        cols = slice(j * FFN_COLS, (j + 1) * FFN_COLS)
        for t in range(FFN_SPLIT):
            gate = _dot(h[t], wg_ref[:, cols])
            up = _dot(h[t], wu_ref[:, cols])
            act = (gate * _sigmoid(gate) * up).astype(BF16)
            acc[t] = acc[t] + _dot(act, wd_ref[cols, :])
    for t, r in enumerate(rows):
        o_ref[r, :] = _rmsnorm(x_ref[r, :] + acc[t], gf_ref[...])


def _ffn(x, g2, wg, wu, wd, gf):
    n_tok = BATCH * SEQ
    row = pl.BlockSpec((FFN_ROWS, D_MODEL), lambda i: (i, 0))
    vec = pl.BlockSpec((1, D_MODEL), lambda i: (0, 0))
    wspec = lambda r, k: pl.BlockSpec((r, k), lambda i: (0, 0), pipeline_mode=pl.Buffered(1))
    return pl.pallas_call(
        _ffn_kernel,
        grid=(n_tok // FFN_ROWS,),
        in_specs=[row, vec, wspec(D_MODEL, D_FF), wspec(D_MODEL, D_FF),
                  wspec(D_FF, D_MODEL), vec],
        out_specs=row,
        out_shape=jax.ShapeDtypeStruct((n_tok, D_MODEL), F32),
        compiler_params=pltpu.CompilerParams(vmem_limit_bytes=VMEM_LIMIT),
        name="ffn",
    )(x, g2, wg, wu, wd, gf)


def _ssm_param_layouts(lam_re, lam_im, log_dt, b_re, b_im, c_re, c_im, d):
    both = lambda t: jnp.concatenate([t[0], t[1]], axis=-1)
    ldt = jnp.broadcast_to(log_dt[:, :, None], (2, SSM_GROUPS, SSM_STATE))
    rows = jnp.stack([both(lam_re), both(lam_im), both(ldt)], axis=1)
    rowp = jnp.pad(rows, ((0, 0), (0, SUBLANES - 3), (0, 0)))
    bt =jnp.concatenate([both(jnp.swapaxes(b_re, 2, 3)),
                          both(jnp.swapaxes(b_im, 2, 3))], axis=1)
    cw = jnp.concatenate([jnp.tile(c_re, (1, 1, 2)), jnp.tile(c_im, (1, 1, 2))], axis=1)
    ct = jnp.swapaxes(cw, 1, 2)
    drow = jnp.repeat(d, CHUNK, axis=-1)[:, None, :]
    return rowp, bt, cw, ct, drow


def kernel(x, norm1_g, w_in, attn_sink, ssm_lambda_re, ssm_lambda_im, ssm_log_dt, ssm_b_re, ssm_b_im, ssm_c_re, ssm_c_im, ssm_d, w_glu, w_attn_branch, w_ssm_branch, w_out, norm2_g, w_ffn_gate, w_ffn_up, w_ffn_down, norm_f_g):
    pos = jnp.arange(SEQ, dtype=F32)
    inv_freq = ROPE_THETA ** (-jnp.arange(0, ROPE_DIM, 2, dtype=F32) / ROPE_DIM)
    ang = inv_freq[:, None] * pos[None, :]
    cos_t, sin_t = jnp.cos(ang), jnp.sin(ang)
    sink_rows = jnp.repeat(attn_sink[0].astype(F32), CHUNK).reshape(
        N_KV_HEADS, 1, GQA_GROUP * CHUNK)

    q_t, k_t, v_t, u4, ga_t, gs_t = _in_proj(
        x, norm1_g[0][None, :], w_in[0].T.astype(BF16), cos_t, sin_t)
    y4 = _ssm(u4, *_ssm_param_layouts(
        ssm_lambda_re[0], ssm_lambda_im[0], ssm_log_dt[0], ssm_b_re[0], ssm_b_im[0],
        ssm_c_re[0], ssm_c_im[0], ssm_d[0]))
    x1 = _mix(q_t, k_t, v_t, sink_rows, y4, ga_t, gs_t, x,
              w_glu[0].T.astype(BF16), w_attn_branch[0].T.astype(BF16),
              w_ssm_branch[0].T.astype(BF16), w_out[0].T.astype(BF16))
    out = _ffn(x1.reshape(BATCH * SEQ, D_MODEL), norm2_g[0][None, :],
               w_ffn_gate[0].astype(BF16), w_ffn_up[0].astype(BF16),
               w_ffn_down[0].astype(BF16), norm_f_g[None, :])
    return out.reshape(BATCH, SEQ, D_MODEL)
```

```python
import math

import jax
import jax.numpy as jnp
from jax import lax
from jax.experimental import pallas as pl
from jax.experimental.pallas import tpu as pltpu

D_MODEL = 1024
BATCH = 8
SEQ = 4096
N_HEADS = 8
N_KV_HEADS = 2
HEAD_DIM = 64
GQA_GROUP = N_HEADS // N_KV_HEADS
ATTN_WIDTH = N_HEADS * HEAD_DIM
KV_WIDTH = N_KV_HEADS * HEAD_DIM
ROPE_DIM = HEAD_DIM // 4
ROPE_HALF = ROPE_DIM // 2
ROPE_THETA = 500000.0
WINDOW = 128
SSM_GROUP_CH = 16
SSM_WIDTH = D_MODEL // 2
SSM_GROUPS = SSM_WIDTH // SSM_GROUP_CH
SSM_STATE = 64
D_FF = 2816
IN_COLS = ATTN_WIDTH + 2 * KV_WIDTH + SSM_WIDTH + 2 * D_MODEL
RMS_EPS = 1e-6
NEG_INF = -1e30

LANES = 128
SUBLANES = 8
MXU_DEPTH = 256
CHUNK = LANES
LOG2_CHUNK = CHUNK.bit_length() - 1
N_CHUNKS = SEQ // CHUNK
ROWS = N_CHUNKS * BATCH
FLAT = SSM_GROUP_CH * CHUNK
SSM_STEP_GROUPS = 2
MERGE_SPLIT = 4
LOG2E = math.log2(math.e)
SUM_ROWS = 16
FFN_ROWS = 1024
FFN_SPLIT = 2
FFN_COLS = 256
VMEM_LIMIT = 58 * 1024 * 1024

assert BATCH == SUBLANES and WINDOW == CHUNK and D_FF % FFN_COLS == 0

_Q0, _K0, _V0, _U0, _GA0, _GS0 = 0, 512, 640, 768, 1280, 2304

BF16 = jnp.bfloat16
F32 = jnp.float32


def _sigmoid(x):
    return 0.5 * jnp.tanh(0.5 * x) + 0.5


def _rmsnorm(x, g):
    return x * lax.rsqrt(jnp.mean(x * x, axis=-1, keepdims=True) + RMS_EPS) * g


def _dot(a, b):
    return jnp.dot(a, b, preferred_element_type=F32)


def _in_proj_kernel(x_ref, g_ref, w_ref, cos_ref, sin_ref, rowp_ref, bt_ref, cw_ref, ct_ref,
                    q_ref, k_ref, v_ref, u_ref, ga_ref, gs_ref,
                    wf_ref, wb_ref, in_ref, out_ref, lc_ref):
    nb = MXU_DEPTH // CHUNK
    h = [_rmsnorm(x_ref[b:b + nb].reshape(MXU_DEPTH, D_MODEL), g_ref[...]).astype(BF16)
         for b in range(0, BATCH, nb)]

    def proj_t(r0, r1):
        return jnp.concatenate(
            [lax.dot_general(w_ref[r0:r1, :], t, (((1,), (1,)), ((), ())),
                             preferred_element_type=F32) for t in h], axis=1)

    cos = jnp.concatenate([cos_ref[...]] * BATCH, axis=1)
    sin = jnp.concatenate([sin_ref[...]] * BATCH, axis=1)

    def rope_t(t, n_heads):
        parts = []
        for hd in range(n_heads):
            r1 = t[hd * HEAD_DIM:hd * HEAD_DIM + ROPE_HALF]
            r2 = t[hd * HEAD_DIM + ROPE_HALF:hd * HEAD_DIM + ROPE_DIM]
            parts += [r1 * cos - r2 * sin, r2 * cos + r1 * sin,
                      t[hd * HEAD_DIM + ROPE_DIM:(hd + 1) * HEAD_DIM]]
        return jnp.concatenate(parts, axis=0)

    def put(ref, r0, t):
        for b in range(BATCH):
            ref[b, r0:r0 + t.shape[0], :] = t[:, b * CHUNK:(b + 1) * CHUNK].astype(ref.dtype)

    slab = MXU_DEPTH
    for r in range(0, ATTN_WIDTH, slab):
        t = proj_t(_Q0 + r, _Q0 + r + slab) * (HEAD_DIM ** -0.5 * LOG2E)
        put(q_ref, r, rope_t(t, slab // HEAD_DIM))
    kv = proj_t(_K0, _U0)
    put(k_ref, 0, rope_t(kv[:KV_WIDTH], N_KV_HEADS))
    put(v_ref, 0, kv[KV_WIDTH:])
    for r in range(0, SSM_WIDTH, slab):
        u_t = proj_t(_U0 + r, _U0 + r + slab)
        for b in range(BATCH):
            u_ref[0, pl.ds(r * BATCH + b, slab, stride=BATCH), :] = (
                u_t[:, b * CHUNK:(b + 1) * CHUNK])
    for r in range(0, D_MODEL, slab):
        put(ga_ref, r, _sigmoid(proj_t(_GA0 + r, _GA0 + r + slab)))

    lc_r, lc_i = _ssm_tables(0, rowp_ref, bt_ref, cw_ref, ct_ref, wf_ref, wb_ref, in_ref, out_ref)
    lc_ref[0] = jnp.concatenate([lc_r, lc_i, jnp.zeros((SUBLANES - 2, LANES), F32)], axis=0)

    for r in range(0, D_MODEL, slab):
        put(gs_ref, r, _sigmoid(proj_t(_GS0 + r, _GS0 + r + slab)))


_PAIRS = SSM_GROUP_CH * SSM_GROUP_CH
_TABLES = [((_PAIRS, CHUNK), jnp.int32), ((_PAIRS, CHUNK), jnp.int32),
           ((FLAT, 2 * LANES), BF16), ((2 * LANES, FLAT), BF16), ((SUBLANES, LANES), F32)]
_PARAMS = [(SUBLANES, LANES), (2 * SSM_GROUP_CH, LANES), (2 * SSM_GROUP_CH, LANES),
           (LANES, 2 * SSM_GROUP_CH)]


def _group_spec(shape, n=1):
    return pl.BlockSpec((n,) + shape, lambda g: (g, 0, 0))


def _in_proj(x, g1, w_t, cos_t, sin_t, rowp, bt, cw, ct):
    assert N_CHUNKS == SSM_GROUPS
    tok = lambda rows, dt: jax.ShapeDtypeStruct((BATCH, rows, SEQ), dt)
    tspec = lambda rows: pl.BlockSpec((BATCH, rows, CHUNK), lambda c: (0, 0, c))
    return pl.pallas_call(
        _in_proj_kernel,
        grid=(N_CHUNKS,),
        in_specs=[
            pl.BlockSpec((BATCH, CHUNK, D_MODEL), lambda c: (0, c, 0)),
            pl.BlockSpec((1, D_MODEL), lambda c: (0, 0)),
            pl.BlockSpec((IN_COLS, D_MODEL), lambda c: (0, 0), pipeline_mode=pl.Buffered(1)),
            pl.BlockSpec((ROPE_HALF, CHUNK), lambda c: (0, c)),
            pl.BlockSpec((ROPE_HALF, CHUNK), lambda c: (0, c)),
        ] + [_group_spec(s) for s in _PARAMS],
        out_specs=[
            tspec(ATTN_WIDTH), tspec(KV_WIDTH), tspec(KV_WIDTH),
            pl.BlockSpec((1, SSM_WIDTH * BATCH, CHUNK), lambda c: (c, 0, 0)),
            tspec(D_MODEL), tspec(D_MODEL),
        ] + [_group_spec(s) for s, _ in _TABLES],
        out_shape=[
            tok(ATTN_WIDTH, BF16), tok(KV_WIDTH, BF16), tok(KV_WIDTH, BF16),
            jax.ShapeDtypeStruct((N_CHUNKS, SSM_WIDTH * BATCH, CHUNK), F32),
            tok(D_MODEL, BF16), tok(D_MODEL, BF16),
        ] + [jax.ShapeDtypeStruct((SSM_GROUPS,) + s, dt) for s, dt in _TABLES],
        compiler_params=pltpu.CompilerParams(vmem_limit_bytes=VMEM_LIMIT),
        name="in_proj",
    )(x, g1, w_t, cos_t, sin_t, rowp, bt, cw, ct)


def _cmul(a_r, a_i, b_r, b_i):
    return a_r * b_r - a_i * b_i, a_r * b_i + a_i * b_r


def _ssm_kernel(u_ref, wf_ref, wb_ref, in_ref, out_ref, lc_ref, drow_ref, y_ref):
    for g in range(SSM_STEP_GROUPS):
        lam_chunk = lc_ref[g, 0:1, :], lc_ref[g, 1:2, :]
        _ssm_apply(g, lam_chunk, u_ref, drow_ref, y_ref, wf_ref, wb_ref, in_ref, out_ref)


def _ssm_tables(g, rowp_ref, bt_ref, cw_ref, ct_ref, wf_ref, wb_ref, in_ref, out_ref):
    half = SSM_STATE
    top_col = lax.broadcasted_iota(jnp.int32, (LANES, 1), 0) < half

    lam_r, lam_i = rowp_ref[g, 0:1, :], rowp_ref[g, 1:2, :]
    dt = jnp.exp(rowp_ref[g, 2:3, :])
    z_r, z_i = lam_r * dt, lam_i * dt
    e1 = jnp.exp(z_r)
    lb_r, lb_i = e1 * jnp.cos(z_i), e1 * jnp.sin(z_i)
    den = lam_r * lam_r + lam_i * lam_i
    cf_r = ((lb_r - 1.0) * lam_r + lb_i * lam_i) / den
    cf_i = (lb_i * lam_r - (lb_r - 1.0) * lam_i) / den
    b_r, b_i = bt_ref[g, 0:SSM_GROUP_CH, :], bt_ref[g, SSM_GROUP_CH:, :]
    bb_r = cf_r * b_r - cf_i * b_i
    bb_i = cf_r * b_i + cf_i * b_r
    c_r, c_i = cw_ref[g, 0:SSM_GROUP_CH, :], cw_ref[g, SSM_GROUP_CH:, :]

    sq_r, sq_i = [lb_r], [lb_i]
    for _ in range(LOG2_CHUNK):
        r, i = _cmul(sq_r[-1], sq_i[-1], sq_r[-1], sq_i[-1])
        sq_r.append(r)
        sq_i.append(i)
    lc_r, lc_i = sq_r[LOG2_CHUNK], sq_i[LOG2_CHUNK]
    pad = jnp.zeros((LANES - 2 * SUBLANES, LANES), F32)
    cols = jnp.concatenate(sq_r[:SUBLANES] + sq_i[:SUBLANES] + [pad], axis=0).T
    sq_rc = [cols[:, k:k + 1] for k in range(LOG2_CHUNK)]
    sq_ic = [cols[:, SUBLANES + k:SUBLANES + k + 1] for k in range(LOG2_CHUNK)]

    j_idx = lax.broadcasted_iota(jnp.int32, (LANES, CHUNK), 1)
    one, zero = jnp.ones((LANES, CHUNK), F32), jnp.zeros((LANES, CHUNK), F32)
    asc_r, asc_i, dsc_r, dsc_i = one, zero, one, zero
    for k in range(LOG2_CHUNK):
        bit = (j_idx & (1 << k)) != 0
        n_r, n_i = _cmul(asc_r, asc_i, sq_rc[k], sq_ic[k])
        asc_r, asc_i = jnp.where(bit, n_r, asc_r), jnp.where(bit, n_i, asc_i)
        n_r, n_i = _cmul(dsc_r, dsc_i, sq_rc[k], sq_ic[k])
        dsc_r, dsc_i = jnp.where(bit, dsc_r, n_r), jnp.where(bit, dsc_i, n_i)
    q_r, q_i = _cmul(jnp.where(top_col, asc_r, dsc_r), jnp.where(top_col, asc_i, dsc_i),
                     sq_rc[0], sq_ic[0])
    p_r = jnp.where(top_col, dsc_r, asc_r).T
    p_i = jnp.where(top_col, dsc_i, asc_i).T

    first = j_idx == 0
    ef_r = jnp.where(top_col, asc_r, jnp.where(first, 1.0, 0.0))
    ef_i = jnp.where(top_col, asc_i, 0.0)
    eb_r = jnp.where(top_col, 0.0, jnp.where(first, 0.0, q_r))
    eb_i = jnp.where(top_col, 0.0, jnp.where(first, 0.0, q_i))
    e_r = jnp.concatenate([eb_r, ef_r], axis=1)
    e_i = jnp.concatenate([eb_i, ef_i], axis=1)
    cb_r = jnp.concatenate([c_r * bb_r[h:h + 1] - c_i * bb_i[h:h + 1]
                            for h in range(SSM_GROUP_CH)], axis=0)
    cb_i = jnp.concatenate([c_r * bb_i[h:h + 1] + c_i * bb_r[h:h + 1]
                            for h in range(SSM_GROUP_CH)], axis=0)
    split = lambda t: (t.astype(BF16), (t - t.astype(BF16).astype(F32)).astype(BF16))
    (cr_h, cr_l), (ci_h, ci_l) = split(cb_r), split(-cb_i)
    (er_h, er_l), (ei_h, ei_l) = split(e_r), split(e_i)
    ktab = _dot(jnp.concatenate([cr_h, cr_h, cr_l, ci_h, ci_h, ci_l], axis=1),
                jnp.concatenate([er_h, er_l, er_h, ei_h, ei_l, ei_h], axis=0))
    kb, kf = ktab[:, :CHUNK], ktab[:, CHUNK:]
    kb1 = pltpu.roll(kb, 1, 1)
    kf1 = jnp.where(lax.broadcasted_iota(jnp.int32, (1, CHUNK), 1) == 0, kb1,
                    pltpu.roll(kf, 1, 1))
    bf16_bits = lambda t: lax.bitcast_convert_type(t.astype(BF16).astype(F32), jnp.int32)
    pack = lambda even, odd: bf16_bits(odd) | lax.shift_right_logical(bf16_bits(even), 16)
    wf_ref[g] = pack(kf, kf1)
    wb_ref[g] = pack(kb, kb1)

    for h in range(SSM_GROUP_CH):
        rows = slice(h * CHUNK, (h + 1) * CHUNK)
        in_ref[g, rows, 0:LANES] = (p_r * bb_r[h:h + 1] - p_i * bb_i[h:h + 1]).astype(BF16)
        in_ref[g, rows, LANES:] = (p_r * bb_i[h:h + 1] + p_i * bb_r[h:h + 1]).astype(BF16)

    for hp in range(SSM_GROUP_CH):
        cols = slice(hp * CHUNK, (hp + 1) * CHUNK)
        cc_r = ct_ref[g, :, hp:hp + 1]
        cc_i = ct_ref[g, :, SSM_GROUP_CH + hp:SSM_GROUP_CH + hp + 1]
        out_ref[g, 0:LANES, cols] = (cc_r * q_r - cc_i * q_i).astype(BF16)
        out_ref[g, LANES:, cols] = (-(cc_r * q_i + cc_i * q_r)).astype(BF16)
    return lc_r, lc_i


def _ssm_apply(g, lam_chunk, u_ref, drow_ref, y_ref, wf_ref, wb_ref, in_ref, out_ref):
    lc_r, lc_i = lam_chunk
    lo_row = lax.broadcasted_iota(jnp.int32, (1, LANES), 1) < SSM_STATE
    row0 = g * SSM_GROUP_CH * BATCH
    a = jnp.concatenate(
        [u_ref[:, row0 + h * BATCH:row0 + (h + 1) * BATCH, :].reshape(ROWS, CHUNK)
         for h in range(SSM_GROUP_CH)], axis=1)
    ab = a.astype(BF16)

    pair = lax.broadcasted_iota(jnp.int32, (CHUNK // 2, CHUNK), 0)
    lane = lax.broadcasted_iota(jnp.int32, (CHUNK // 2, CHUNK), 1)
    upper = lane + 2 * pair < CHUNK
    per_slab = MXU_DEPTH // CHUNK

    def toeplitz_slab(slab):
        col_blocks = []
        for h in range(SSM_GROUP_CH):
            blocks = []
            for hp in range(slab * per_slab, (slab + 1) * per_slab):
                row = pl.ds(h * SSM_GROUP_CH + hp, 1)
                w = jnp.where(upper,
                              jnp.broadcast_to(wf_ref[g, row, :], (CHUNK // 2, CHUNK)),
                              jnp.broadcast_to(wb_ref[g, row, :], (CHUNK // 2, CHUNK)))
                w = pltpu.roll(w, 0, 1, stride=2, stride_axis=0)
                blocks.append(pltpu.bitcast(w, BF16))
            col_blocks.append(jnp.concatenate(blocks, axis=1))
        return jnp.concatenate(col_blocks, axis=0)

    e = _dot(ab, in_ref[g])
    s_r = jnp.zeros((BATCH, LANES), F32)
    s_i = jnp.zeros((BATCH, LANES), F32)
    hist_r, hist_i = [], []
    for i in range(N_CHUNKS):
        hist_r.append(s_r)
        hist_i.append(s_i)
        fr = slice(i * BATCH, (i + 1) * BATCH)
        br = slice((N_CHUNKS - 1 - i) * BATCH, (N_CHUNKS - i) * BATCH)
        in_r = jnp.where(lo_row, e[fr, 0:LANES], e[br, 0:LANES])
        in_i = jnp.where(lo_row, e[fr, LANES:], e[br, LANES:])
        s_r, s_i = (s_r * lc_r - s_i * lc_i + in_r, s_r * lc_i + s_i * lc_r + in_i)
    states = jnp.concatenate(
        [jnp.concatenate([jnp.where(lo_row, hist_r[c], hist_r[N_CHUNKS - 1 - c]),
                          jnp.where(lo_row, hist_i[c], hist_i[N_CHUNKS - 1 - c])], axis=1)
         for c in range(N_CHUNKS)], axis=0).astype(BF16)

    for slab in range(SSM_GROUP_CH // per_slab):
        cols = slice(slab * MXU_DEPTH, (slab + 1) * MXU_DEPTH)
        y = (_dot(ab, toeplitz_slab(slab)) + _dot(states, out_ref[g, :, cols])
             + a[:, cols] * drow_ref[g][:, cols])
        for k in range(per_slab):
            hp = slab * per_slab + k
            y_ref[:, row0 + hp * BATCH:row0 + (hp + 1) * BATCH, :] = (
                y[:, k * CHUNK:(k + 1) * CHUNK].reshape(N_CHUNKS, BATCH, CHUNK))


def _ssm(u4, tables, drow):
    ng = SSM_STEP_GROUPS
    data = pl.BlockSpec((N_CHUNKS, ng * SSM_GROUP_CH * BATCH, CHUNK), lambda g: (0, g, 0))
    return pl.pallas_call(
        _ssm_kernel,
        grid=(SSM_GROUPS // ng,),
        in_specs=[data] + [_group_spec(s, ng) for s, _ in _TABLES] + [_group_spec((1, FLAT), ng)],
        out_specs=data,
        out_shape=jax.ShapeDtypeStruct((N_CHUNKS, SSM_WIDTH * BATCH, CHUNK), F32),
        compiler_params=pltpu.CompilerParams(vmem_limit_bytes=VMEM_LIMIT),
        name="ssm",
    )(u4, *tables, drow)


_ATTN_CASES = [(b, j) for b in range(BATCH) for j in range(N_KV_HEADS)]


def _attn_scores(q_ref, kp_ref, kc_ref, kn_ref):
    zeros = jnp.zeros((HEAD_DIM, GQA_GROUP * CHUNK), BF16)
    keys = []
    for b in range(BATCH):
        k_t = jnp.concatenate([kp_ref[b], kc_ref[b], kn_ref[b]], axis=1)
        keys.append(k_t.astype(F32).T.astype(BF16))
    scores = []
    for b, j in _ATTN_CASES:
        q_t = jnp.concatenate(
            [q_ref[b, (j * GQA_GROUP + g) * HEAD_DIM:(j * GQA_GROUP + g + 1) * HEAD_DIM, :]
             for g in range(GQA_GROUP)], axis=1)
        q_pad = jnp.concatenate([q_t, zeros] if j == 0 else [zeros, q_t], axis=0)
        scores.append(_dot(keys[b], q_pad))
    return scores


def _attn_finish(c, scores, vp_ref, vc_ref, vn_ref, sink_ref):
    kk = lax.broadcasted_iota(jnp.int32, (CHUNK, CHUNK), 0)
    qq = lax.broadcasted_iota(jnp.int32, (CHUNK, CHUNK), 1)
    prev_ok = jnp.concatenate([kk >= qq] * GQA_GROUP, axis=1) & (c > 0)
    next_ok = jnp.concatenate([kk <= qq] * GQA_GROUP, axis=1) & (c < N_CHUNKS - 1)
    ones = jnp.ones((SUM_ROWS, 3 * CHUNK), BF16)
    vals = [jnp.concatenate([vp_ref[b], vc_ref[b], vn_ref[b]], axis=1) for b in range(BATCH)]
    heads = {}
    for (b, j), s in zip(_ATTN_CASES, scores):
        s = jnp.concatenate([jnp.where(prev_ok, s[:CHUNK], NEG_INF), s[CHUNK:2 * CHUNK],
                             jnp.where(next_ok, s[2 * CHUNK:], NEG_INF)], axis=0)
        sink = sink_ref[j] * LOG2E
        m = jnp.maximum(jnp.max(s, axis=0, keepdims=True), sink)
        p = jnp.exp2(s - m).astype(BF16)
        v_aug = jnp.concatenate([vals[b][j * HEAD_DIM:(j + 1) * HEAD_DIM], ones], axis=0)
        o_aug = _dot(v_aug, p)
        o = o_aug[:HEAD_DIM] / (o_aug[HEAD_DIM:HEAD_DIM + 1] + jnp.exp2(sink - m))
        for g in range(GQA_GROUP):
            heads[b, j * GQA_GROUP + g] = o[:, g * CHUNK:(g + 1) * CHUNK].astype(BF16)
    return [jnp.concatenate([heads[b, hd] for hd in range(N_HEADS)], axis=0)
            for b in range(BATCH)]


def _mix_kernel(q_ref, kp_ref, kc_ref, kn_ref, vp_ref, vc_ref, vn_ref, sink_ref,
                y_ref, ga_ref, gs_ref, x_ref, wglu_ref, wab_ref, wsb_ref, wout_ref, o_ref):
    scores = _attn_scores(q_ref, kp_ref, kc_ref, kn_ref)
    nb = BATCH // MERGE_SPLIT
    tiles = [range(t * nb, (t + 1) * nb) for t in range(MERGE_SPLIT)]
    cat = lambda ref, bs: jnp.concatenate([ref[b] for b in bs], axis=1)
    y_t = [jnp.concatenate([y_ref[0, pl.ds(b, SSM_WIDTH, stride=BATCH), :] for b in bs],
                           axis=1) for bs in tiles]
    yg = [0.5 * y * (1.0 + jnp.tanh(math.sqrt(2.0 / math.pi) * (y + 0.044715 * y ** 3)))
          for y in y_t]
    z_t = [_dot(wglu_ref[...], g.astype(BF16)) for g in yg]
    ssm_t = [g * _sigmoid(z) for g, z in zip(yg, z_t)]
    s_t = [_dot(wsb_ref[...], s.astype(BF16)) for s in ssm_t]
    attn = _attn_finish(pl.program_id(0), scores, vp_ref, vc_ref, vn_ref, sink_ref)
    a_t = [_dot(wab_ref[...], jnp.concatenate([attn[b] for b in bs], axis=1)) for bs in tiles]
    merged = [cat(ga_ref, bs).astype(F32) * a + cat(gs_ref, bs).astype(F32) * s
              for bs, a, s in zip(tiles, a_t, s_t)]
    for bs, m in zip(tiles, merged):
        mb = m.astype(BF16)
        for r in range(0, D_MODEL, MXU_DEPTH):
            o = _dot(wout_ref[r:r + MXU_DEPTH, :], mb).T
            o_ref[bs.start:bs.stop, :, r:r + MXU_DEPTH] = (
                x_ref[bs.start:bs.stop, :, r:r + MXU_DEPTH] + o.reshape(nb, CHUNK, MXU_DEPTH))


def _mix(q_t, k_t, v_t, sink_rows, y4, ga_t, gs_t, x, wglu_t, wab_t, wsb_t, wout_t):
    prev = lambda c: (0, 0, jnp.maximum(c - 1, 0))
    cur = lambda c: (0, 0, c)
    nxt = lambda c: (0, 0, jnp.minimum(c + 1, N_CHUNKS - 1))
    tspec = lambda rows, im=cur: pl.BlockSpec((BATCH, rows, CHUNK), im)
    wspec = lambda r, k: pl.BlockSpec((r, k), lambda c: (0, 0), pipeline_mode=pl.Buffered(1))
    xspec = pl.BlockSpec((BATCH, CHUNK, D_MODEL), lambda c: (0, c, 0))
    return pl.pallas_call(
        _mix_kernel,
        grid=(N_CHUNKS,),
        in_specs=[tspec(ATTN_WIDTH),
                  tspec(KV_WIDTH, prev), tspec(KV_WIDTH), tspec(KV_WIDTH, nxt),
                  tspec(KV_WIDTH, prev), tspec(KV_WIDTH), tspec(KV_WIDTH, nxt),
                  pl.BlockSpec((N_KV_HEADS, 1, GQA_GROUP * CHUNK), lambda c: (0, 0, 0)),
                  pl.BlockSpec((1, SSM_WIDTH * BATCH, CHUNK), lambda c: (c, 0, 0)),
                  tspec(D_MODEL), tspec(D_MODEL), xspec,
                  wspec(SSM_WIDTH, SSM_WIDTH), wspec(D_MODEL, ATTN_WIDTH),
                  wspec(D_MODEL, SSM_WIDTH), wspec(D_MODEL, D_MODEL)],
        out_specs=xspec,
        out_shape=jax.ShapeDtypeStruct((BATCH, SEQ, D_MODEL), F32),
        compiler_params=pltpu.CompilerParams(vmem_limit_bytes=VMEM_LIMIT),
        name="mix",
    )(q_t, k_t, k_t, k_t, v_t, v_t, v_t, sink_rows, y4, ga_t, gs_t, x,
      wglu_t, wab_t, wsb_t, wout_t)


def _ffn_kernel(x_ref, g2_ref, wg_ref, wu_ref, wd_ref, gf_ref, o_ref):
    tile = FFN_ROWS // FFN_SPLIT
    rows = [slice(t * tile, (t + 1) * tile) for t in range(FFN_SPLIT)]
    h = [_rmsnorm(x_ref[r, :], g2_ref[...]).astype(BF16) for r in rows]
    acc = [jnp.zeros((tile, D_MODEL), F32) for _ in rows]
    for j in range(D_FF // FFN_COLS):
        cols = slice(j * FFN_COLS, (j + 1) * FFN_COLS)
        for t in range(FFN_SPLIT):
            gate = _dot(h[t], wg_ref[:, cols])
            up = _dot(h[t], wu_ref[:, cols])
            act = (gate * _sigmoid(gate) * up).astype(BF16)
            acc[t] = acc[t] + _dot(act, wd_ref[cols, :])
    for t, r in enumerate(rows):
        o_ref[r, :] = _rmsnorm(x_ref[r, :] + acc[t], gf_ref[...])


def _ffn(x, g2, wg, wu, wd, gf):
    n_tok = BATCH * SEQ
    row = pl.BlockSpec((FFN_ROWS, D_MODEL), lambda i: (i, 0))
    vec = pl.BlockSpec((1, D_MODEL), lambda i: (0, 0))
    wspec = lambda r, k: pl.BlockSpec((r, k), lambda i: (0, 0), pipeline_mode=pl.Buffered(1))
    return pl.pallas_call(
        _ffn_kernel,
        grid=(n_tok // FFN_ROWS,),
        in_specs=[row, vec, wspec(D_MODEL, D_FF), wspec(D_MODEL, D_FF),
                  wspec(D_FF, D_MODEL), vec],
        out_specs=row,
        out_shape=jax.ShapeDtypeStruct((n_tok, D_MODEL), F32),
        compiler_params=pltpu.CompilerParams(vmem_limit_bytes=VMEM_LIMIT),
        name="ffn",
    )(x, g2, wg, wu, wd, gf)


def _ssm_param_layouts(lam_re, lam_im, log_dt, b_re, b_im, c_re, c_im, d):
    both = lambda t: jnp.concatenate([t[0], t[1]], axis=-1)
    ldt = jnp.broadcast_to(log_dt[:, :, None], (2, SSM_GROUPS, SSM_STATE))
    rows = jnp.stack([both(lam_re), both(lam_im), both(ldt)], axis=1)
    rowp = jnp.pad(rows, ((0, 0), (0, SUBLANES - 3), (0, 0)))
    bt =jnp.concatenate([both(jnp.swapaxes(b_re, 2, 3)),
                          both(jnp.swapaxes(b_im, 2, 3))], axis=1)
    cw = jnp.concatenate([jnp.tile(c_re, (1, 1, 2)), jnp.tile(c_im, (1, 1, 2))], axis=1)
    ct = jnp.swapaxes(cw, 1, 2)
    drow = jnp.repeat(d, CHUNK, axis=-1)[:, None, :]
    return rowp, bt, cw, ct, drow


def kernel(x, norm1_g, w_in, attn_sink, ssm_lambda_re, ssm_lambda_im, ssm_log_dt, ssm_b_re, ssm_b_im, ssm_c_re, ssm_c_im, ssm_d, w_glu, w_attn_branch, w_ssm_branch, w_out, norm2_g, w_ffn_gate, w_ffn_up, w_ffn_down, norm_f_g):
    pos = jnp.arange(SEQ, dtype=F32)
    inv_freq = ROPE_THETA ** (-jnp.arange(0, ROPE_DIM, 2, dtype=F32) / ROPE_DIM)
    ang = inv_freq[:, None] * pos[None, :]
    cos_t, sin_t = jnp.cos(ang), jnp.sin(ang)
    sink_rows = jnp.repeat(attn_sink[0].astype(F32), CHUNK).reshape(
        N_KV_HEADS, 1, GQA_GROUP * CHUNK)

    rowp, bt, cw, ct, drow = _ssm_param_layouts(
        ssm_lambda_re[0], ssm_lambda_im[0], ssm_log_dt[0], ssm_b_re[0], ssm_b_im[0],
        ssm_c_re[0], ssm_c_im[0], ssm_d[0])
    q_t, k_t, v_t, u4, ga_t, gs_t, *tables = _in_proj(
        x, norm1_g[0][None, :], w_in[0].T.astype(BF16), cos_t, sin_t, rowp, bt, cw, ct)
    y4 = _ssm(u4, tables, drow)
    x1 = _mix(q_t, k_t, v_t, sink_rows, y4, ga_t, gs_t, x,
              w_glu[0].T.astype(BF16), w_attn_branch[0].T.astype(BF16),
              w_ssm_branch[0].T.astype(BF16), w_out[0].T.astype(BF16))
    out = _ffn(x1.reshape(BATCH * SEQ, D_MODEL), norm2_g[0][None, :],
               w_ffn_gate[0].astype(BF16), w_ffn_up[0].astype(BF16),
               w_ffn_down[0].astype(BF16), norm_f_g[None, :])
    return out.reshape(BATCH, SEQ, D_MODEL)
```

```python
import math

import jax
import jax.numpy as jnp
from jax import lax
from jax.experimental import pallas as pl
from jax.experimental.pallas import tpu as pltpu

D_MODEL = 1024
BATCH = 8
SEQ = 4096
N_HEADS = 8
N_KV_HEADS = 2
HEAD_DIM = 64
GQA_GROUP = N_HEADS // N_KV_HEADS
ATTN_WIDTH = N_HEADS * HEAD_DIM
KV_WIDTH = N_KV_HEADS * HEAD_DIM
ROPE_DIM = HEAD_DIM // 4
ROPE_HALF = ROPE_DIM // 2
ROPE_THETA = 500000.0
WINDOW = 128
SSM_GROUP_CH = 16
SSM_WIDTH = D_MODEL // 2
SSM_GROUPS = SSM_WIDTH // SSM_GROUP_CH
SSM_STATE = 64
D_FF = 2816
IN_COLS = ATTN_WIDTH + 2 * KV_WIDTH + SSM_WIDTH + 2 * D_MODEL
RMS_EPS = 1e-6
NEG_INF = -1e30

LANES = 128
SUBLANES = 8
MXU_DEPTH = 256
CHUNK = LANES
LOG2_CHUNK = CHUNK.bit_length() - 1
N_CHUNKS = SEQ // CHUNK
ROWS = N_CHUNKS * BATCH
FLAT = SSM_GROUP_CH * CHUNK
SSM_STEP_GROUPS = 4
MERGE_SPLIT = 4
LOG2E = math.log2(math.e)
SUM_ROWS = 16
FFN_ROWS = 1024
FFN_SPLIT = 2
FFN_COLS = 256
VMEM_LIMIT = 58 * 1024 * 1024

assert BATCH == SUBLANES and WINDOW == CHUNK and D_FF % FFN_COLS == 0

_Q0, _K0, _V0, _U0, _GA0, _GS0 = 0, 512, 640, 768, 1280, 2304

BF16 = jnp.bfloat16
F32 = jnp.float32


def _sigmoid(x):
    return 0.5 * jnp.tanh(0.5 * x) + 0.5


def _rmsnorm(x, g):
    return x * lax.rsqrt(jnp.mean(x * x, axis=-1, keepdims=True) + RMS_EPS) * g


def _dot(a, b):
    return jnp.dot(a, b, preferred_element_type=F32)


def _in_proj_kernel(x_ref, g_ref, w_ref, cos_ref, sin_ref, rowp_ref, bt_ref, cw_ref, ct_ref,
                    q_ref, k_ref, v_ref, u_ref, ga_ref, gs_ref,
                    wf_ref, wb_ref, in_ref, out_ref, lc_ref):
    nb = MXU_DEPTH // CHUNK
    h = [_rmsnorm(x_ref[b:b + nb].reshape(MXU_DEPTH, D_MODEL), g_ref[...]).astype(BF16)
         for b in range(0, BATCH, nb)]

    def proj_t(r0, r1):
        return jnp.concatenate(
            [lax.dot_general(w_ref[r0:r1, :], t, (((1,), (1,)), ((), ())),
                             preferred_element_type=F32) for t in h], axis=1)

    cos = jnp.concatenate([cos_ref[...]] * BATCH, axis=1)
    sin = jnp.concatenate([sin_ref[...]] * BATCH, axis=1)

    def rope_t(t, n_heads):
        parts = []
        for hd in range(n_heads):
            r1 = t[hd * HEAD_DIM:hd * HEAD_DIM + ROPE_HALF]
            r2 = t[hd * HEAD_DIM + ROPE_HALF:hd * HEAD_DIM + ROPE_DIM]
            parts += [r1 * cos - r2 * sin, r2 * cos + r1 * sin,
                      t[hd * HEAD_DIM + ROPE_DIM:(hd + 1) * HEAD_DIM]]
        return jnp.concatenate(parts, axis=0)

    def put(ref, r0, t):
        for b in range(BATCH):
            ref[b, r0:r0 + t.shape[0], :] = t[:, b * CHUNK:(b + 1) * CHUNK].astype(ref.dtype)

    slab = MXU_DEPTH
    for r in range(0, ATTN_WIDTH, slab):
        t = proj_t(_Q0 + r, _Q0 + r + slab) * (HEAD_DIM ** -0.5 * LOG2E)
        put(q_ref, r, rope_t(t, slab // HEAD_DIM))
    kv = proj_t(_K0, _U0)
    put(k_ref, 0, rope_t(kv[:KV_WIDTH], N_KV_HEADS))
    put(v_ref, 0, kv[KV_WIDTH:])
    for r in range(0, SSM_WIDTH, slab):
        u_t = proj_t(_U0 + r, _U0 + r + slab)
        for b in range(BATCH):
            u_ref[0, pl.ds(r * BATCH + b, slab, stride=BATCH), :] = (
                u_t[:, b * CHUNK:(b + 1) * CHUNK])
    for r in range(0, D_MODEL, slab):
        put(ga_ref, r, _sigmoid(proj_t(_GA0 + r, _GA0 + r + slab)))

    lc_r, lc_i = _ssm_tables(0, rowp_ref, bt_ref, cw_ref, ct_ref, wf_ref, wb_ref, in_ref, out_ref)
    lc_ref[0] = jnp.concatenate([lc_r, lc_i, jnp.zeros((SUBLANES - 2, LANES), F32)], axis=0)

    for r in range(0, D_MODEL, slab):
        put(gs_ref, r, _sigmoid(proj_t(_GS0 + r, _GS0 + r + slab)))


_PAIRS = SSM_GROUP_CH * SSM_GROUP_CH
_TABLES = [((_PAIRS, CHUNK), jnp.int32), ((_PAIRS, CHUNK), jnp.int32),
           ((FLAT, 2 * LANES), BF16), ((2 * LANES, FLAT), BF16), ((SUBLANES, LANES), F32)]
_PARAMS = [(SUBLANES, LANES), (2 * SSM_GROUP_CH, LANES), (2 * SSM_GROUP_CH, LANES),
           (LANES, 2 * SSM_GROUP_CH)]


def _group_spec(shape, n=1):
    return pl.BlockSpec((n,) + shape, lambda g: (g, 0, 0))


def _in_proj(x, g1, w_t, cos_t, sin_t, rowp, bt, cw, ct):
    assert N_CHUNKS == SSM_GROUPS
    tok = lambda rows, dt: jax.ShapeDtypeStruct((BATCH, rows, SEQ), dt)
    tspec = lambda rows: pl.BlockSpec((BATCH, rows, CHUNK), lambda c: (0, 0, c))
    return pl.pallas_call(
        _in_proj_kernel,
        grid=(N_CHUNKS,),
        in_specs=[
            pl.BlockSpec((BATCH, CHUNK, D_MODEL), lambda c: (0, c, 0)),
            pl.BlockSpec((1, D_MODEL), lambda c: (0, 0)),
            pl.BlockSpec((IN_COLS, D_MODEL), lambda c: (0, 0), pipeline_mode=pl.Buffered(1)),
            pl.BlockSpec((ROPE_HALF, CHUNK), lambda c: (0, c)),
            pl.BlockSpec((ROPE_HALF, CHUNK), lambda c: (0, c)),
        ] + [_group_spec(s) for s in _PARAMS],
        out_specs=[
            tspec(ATTN_WIDTH), tspec(KV_WIDTH), tspec(KV_WIDTH),
            pl.BlockSpec((1, SSM_WIDTH * BATCH, CHUNK), lambda c: (c, 0, 0)),
            tspec(D_MODEL), tspec(D_MODEL),
        ] + [_group_spec(s) for s, _ in _TABLES],
        out_shape=[
            tok(ATTN_WIDTH, BF16), tok(KV_WIDTH, BF16), tok(KV_WIDTH, BF16),
            jax.ShapeDtypeStruct((N_CHUNKS, SSM_WIDTH * BATCH, CHUNK), F32),
            tok(D_MODEL, BF16), tok(D_MODEL, BF16),
        ] + [jax.ShapeDtypeStruct((SSM_GROUPS,) + s, dt) for s, dt in _TABLES],
        compiler_params=pltpu.CompilerParams(vmem_limit_bytes=VMEM_LIMIT),
        name="in_proj",
    )(x, g1, w_t, cos_t, sin_t, rowp, bt, cw, ct)


def _cmul(a_r, a_i, b_r, b_i):
    return a_r * b_r - a_i * b_i, a_r * b_i + a_i * b_r


def _ssm_kernel(u_ref, wf_ref, wb_ref, in_ref, out_ref, lc_ref, drow_ref, y_ref):
    groups = range(SSM_STEP_GROUPS)
    data = [_ssm_states(g, (lc_ref[g, 0:1, :], lc_ref[g, 1:2, :]), u_ref, in_ref) for g in groups]
    for g in groups:
        _ssm_outputs(g, *data[g], drow_ref, y_ref, wf_ref, wb_ref, out_ref)


def _ssm_tables(g, rowp_ref, bt_ref, cw_ref, ct_ref, wf_ref, wb_ref, in_ref, out_ref):
    half = SSM_STATE
    top_col = lax.broadcasted_iota(jnp.int32, (LANES, 1), 0) < half

    lam_r, lam_i = rowp_ref[g, 0:1, :], rowp_ref[g, 1:2, :]
    dt = jnp.exp(rowp_ref[g, 2:3, :])
    z_r, z_i = lam_r * dt, lam_i * dt
    e1 = jnp.exp(z_r)
    lb_r, lb_i = e1 * jnp.cos(z_i), e1 * jnp.sin(z_i)
    den = lam_r * lam_r + lam_i * lam_i
    cf_r = ((lb_r - 1.0) * lam_r + lb_i * lam_i) / den
    cf_i = (lb_i * lam_r - (lb_r - 1.0) * lam_i) / den
    b_r, b_i = bt_ref[g, 0:SSM_GROUP_CH, :], bt_ref[g, SSM_GROUP_CH:, :]
    bb_r = cf_r * b_r - cf_i * b_i
    bb_i = cf_r * b_i + cf_i * b_r
    c_r, c_i = cw_ref[g, 0:SSM_GROUP_CH, :], cw_ref[g, SSM_GROUP_CH:, :]

    sq_r, sq_i = [lb_r], [lb_i]
    for _ in range(LOG2_CHUNK):
        r, i = _cmul(sq_r[-1], sq_i[-1], sq_r[-1], sq_i[-1])
        sq_r.append(r)
        sq_i.append(i)
    lc_r, lc_i = sq_r[LOG2_CHUNK], sq_i[LOG2_CHUNK]
    pad = jnp.zeros((LANES - 2 * SUBLANES, LANES), F32)
    cols = jnp.concatenate(sq_r[:SUBLANES] + sq_i[:SUBLANES] + [pad], axis=0).T
    sq_rc = [cols[:, k:k + 1] for k in range(LOG2_CHUNK)]
    sq_ic = [cols[:, SUBLANES + k:SUBLANES + k + 1] for k in range(LOG2_CHUNK)]

    j_idx = lax.broadcasted_iota(jnp.int32, (LANES, CHUNK), 1)
    one, zero = jnp.ones((LANES, CHUNK), F32), jnp.zeros((LANES, CHUNK), F32)
    asc_r, asc_i, dsc_r, dsc_i = one, zero, one, zero
    for k in range(LOG2_CHUNK):
        bit = (j_idx & (1 << k)) != 0
        n_r, n_i = _cmul(asc_r, asc_i, sq_rc[k], sq_ic[k])
        asc_r, asc_i = jnp.where(bit, n_r, asc_r), jnp.where(bit, n_i, asc_i)
        n_r, n_i = _cmul(dsc_r, dsc_i, sq_rc[k], sq_ic[k])
        dsc_r, dsc_i = jnp.where(bit, dsc_r, n_r), jnp.where(bit, dsc_i, n_i)
    q_r, q_i = _cmul(jnp.where(top_col, asc_r, dsc_r), jnp.where(top_col, asc_i, dsc_i),
                     sq_rc[0], sq_ic[0])
    p_r = jnp.where(top_col, dsc_r, asc_r).T
    p_i = jnp.where(top_col, dsc_i, asc_i).T

    first = j_idx == 0
    ef_r = jnp.where(top_col, asc_r, jnp.where(first, 1.0, 0.0))
    ef_i = jnp.where(top_col, asc_i, 0.0)
    eb_r = jnp.where(top_col, 0.0, jnp.where(first, 0.0, q_r))
    eb_i = jnp.where(top_col, 0.0, jnp.where(first, 0.0, q_i))
    e_r = jnp.concatenate([eb_r, ef_r], axis=1)
    e_i = jnp.concatenate([eb_i, ef_i], axis=1)
    cb_r = jnp.concatenate([c_r * bb_r[h:h + 1] - c_i * bb_i[h:h + 1]
                            for h in range(SSM_GROUP_CH)], axis=0)
    cb_i = jnp.concatenate([c_r * bb_i[h:h + 1] + c_i * bb_r[h:h + 1]
                            for h in range(SSM_GROUP_CH)], axis=0)
    split = lambda t: (t.astype(BF16), (t - t.astype(BF16).astype(F32)).astype(BF16))
    (cr_h, cr_l), (ci_h, ci_l) = split(cb_r), split(-cb_i)
    (er_h, er_l), (ei_h, ei_l) = split(e_r), split(e_i)
    ktab = _dot(jnp.concatenate([cr_h, cr_h, cr_l, ci_h, ci_h, ci_l], axis=1),
                jnp.concatenate([er_h, er_l, er_h, ei_h, ei_l, ei_h], axis=0))
    kb, kf = ktab[:, :CHUNK], ktab[:, CHUNK:]
    kb1 = pltpu.roll(kb, 1, 1)
    kf1 = jnp.where(lax.broadcasted_iota(jnp.int32, (1, CHUNK), 1) == 0, kb1,
                    pltpu.roll(kf, 1, 1))
    bf16_bits = lambda t: lax.bitcast_convert_type(t.astype(BF16).astype(F32), jnp.int32)
    pack = lambda even, odd: bf16_bits(odd) | lax.shift_right_logical(bf16_bits(even), 16)
    wf_ref[g] = pack(kf, kf1)
    wb_ref[g] = pack(kb, kb1)

    for h in range(SSM_GROUP_CH):
        rows = slice(h * CHUNK, (h + 1) * CHUNK)
        in_ref[g, rows, 0:LANES] = (p_r * bb_r[h:h + 1] - p_i * bb_i[h:h + 1]).astype(BF16)
        in_ref[g, rows, LANES:] = (p_r * bb_i[h:h + 1] + p_i * bb_r[h:h + 1]).astype(BF16)

    for hp in range(SSM_GROUP_CH):
        cols = slice(hp * CHUNK, (hp + 1) * CHUNK)
        cc_r = ct_ref[g, :, hp:hp + 1]
        cc_i = ct_ref[g, :, SSM_GROUP_CH + hp:SSM_GROUP_CH + hp + 1]
        out_ref[g, 0:LANES, cols] = (cc_r * q_r - cc_i * q_i).astype(BF16)
        out_ref[g, LANES:, cols] = (-(cc_r * q_i + cc_i * q_r)).astype(BF16)
    return lc_r, lc_i


def _ssm_states(g, lam_chunk, u_ref, in_ref):
    lc_r, lc_i = lam_chunk
    lo_row = lax.broadcasted_iota(jnp.int32, (1, LANES), 1) < SSM_STATE
    row0 = g * SSM_GROUP_CH * BATCH
    a = jnp.concatenate(
        [u_ref[:, row0 + h * BATCH:row0 + (h + 1) * BATCH, :].reshape(ROWS, CHUNK)
         for h in range(SSM_GROUP_CH)], axis=1)
    ab = a.astype(BF16)
    e = _dot(ab, in_ref[g])
    s_r = jnp.zeros((BATCH, LANES), F32)
    s_i = jnp.zeros((BATCH, LANES), F32)
    hist_r, hist_i = [], []
    for i in range(N_CHUNKS):
        hist_r.append(s_r)
        hist_i.append(s_i)
        fr = slice(i * BATCH, (i + 1) * BATCH)
        br = slice((N_CHUNKS - 1 - i) * BATCH, (N_CHUNKS - i) * BATCH)
        in_r = jnp.where(lo_row, e[fr, 0:LANES], e[br, 0:LANES])
        in_i = jnp.where(lo_row, e[fr, LANES:], e[br, LANES:])
        s_r, s_i = (s_r * lc_r - s_i * lc_i + in_r, s_r * lc_i + s_i * lc_r + in_i)
    states = jnp.concatenate(
        [jnp.concatenate([jnp.where(lo_row, hist_r[c], hist_r[N_CHUNKS - 1 - c]),
                          jnp.where(lo_row, hist_i[c], hist_i[N_CHUNKS - 1 - c])], axis=1)
         for c in range(N_CHUNKS)], axis=0).astype(BF16)
    return a, ab, states


def _ssm_outputs(g, a, ab, states, drow_ref, y_ref, wf_ref, wb_ref, out_ref):
    row0 = g * SSM_GROUP_CH * BATCH

    pair = lax.broadcasted_iota(jnp.int32, (CHUNK // 2, CHUNK), 0)
    lane = lax.broadcasted_iota(jnp.int32, (CHUNK // 2, CHUNK), 1)
    upper = lane + 2 * pair < CHUNK
    per_slab = MXU_DEPTH // CHUNK

    def toeplitz_slab(slab):
        col_blocks = []
        for h in range(SSM_GROUP_CH):
            blocks = []
            for hp in range(slab * per_slab, (slab + 1) * per_slab):
                row = pl.ds(h * SSM_GROUP_CH + hp, 1)
                w = jnp.where(upper,
                              jnp.broadcast_to(wf_ref[g, row, :], (CHUNK // 2, CHUNK)),
                              jnp.broadcast_to(wb_ref[g, row, :], (CHUNK // 2, CHUNK)))
                w = pltpu.roll(w, 0, 1, stride=2, stride_axis=0)
                blocks.append(pltpu.bitcast(w, BF16))
            col_blocks.append(jnp.concatenate(blocks, axis=1))
        return jnp.concatenate(col_blocks, axis=0)

    for slab in range(SSM_GROUP_CH // per_slab):
        cols = slice(slab * MXU_DEPTH, (slab + 1) * MXU_DEPTH)
        y = (_dot(ab, toeplitz_slab(slab)) + _dot(states, out_ref[g, :, cols])
             + a[:, cols] * drow_ref[g][:, cols])
        for k in range(per_slab):
            hp = slab * per_slab + k
            y_ref[:, row0 + hp * BATCH:row0 + (hp + 1) * BATCH, :] = (
                y[:, k * CHUNK:(k + 1) * CHUNK].reshape(N_CHUNKS, BATCH, CHUNK))


def _ssm(u4, tables, drow):
    ng = SSM_STEP_GROUPS
    data = pl.BlockSpec((N_CHUNKS, ng * SSM_GROUP_CH * BATCH, CHUNK), lambda g: (0, g, 0))
    return pl.pallas_call(
        _ssm_kernel,
        grid=(SSM_GROUPS // ng,),
        in_specs=[data] + [_group_spec(s, ng) for s, _ in _TABLES] + [_group_spec((1, FLAT), ng)],
        out_specs=data,
        out_shape=jax.ShapeDtypeStruct((N_CHUNKS, SSM_WIDTH * BATCH, CHUNK), F32),
        compiler_params=pltpu.CompilerParams(vmem_limit_bytes=VMEM_LIMIT),
        name="ssm",
    )(u4, *tables, drow)


_ATTN_CASES = [(b, j) for b in range(BATCH) for j in range(N_KV_HEADS)]


def _attn_scores(q_ref, kp_ref, kc_ref, kn_ref):
    zeros = jnp.zeros((HEAD_DIM, GQA_GROUP * CHUNK), BF16)
    keys = []
    for b in range(BATCH):
        k_t = jnp.concatenate([kp_ref[b], kc_ref[b], kn_ref[b]], axis=1)
        keys.append(k_t.astype(F32).T.astype(BF16))
    scores = []
    for b, j in _ATTN_CASES:
        q_t = jnp.concatenate(
            [q_ref[b, (j * GQA_GROUP + g) * HEAD_DIM:(j * GQA_GROUP + g + 1) * HEAD_DIM, :]
             for g in range(GQA_GROUP)], axis=1)
        q_pad = jnp.concatenate([q_t, zeros] if j == 0 else [zeros, q_t], axis=0)
        scores.append(_dot(keys[b], q_pad))
    return scores


def _attn_finish(c, scores, vp_ref, vc_ref, vn_ref, sink_ref):
    kk = lax.broadcasted_iota(jnp.int32, (CHUNK, CHUNK), 0)
    qq = lax.broadcasted_iota(jnp.int32, (CHUNK, CHUNK), 1)
    prev_ok = jnp.concatenate([kk >= qq] * GQA_GROUP, axis=1) & (c > 0)
    next_ok = jnp.concatenate([kk <= qq] * GQA_GROUP, axis=1) & (c < N_CHUNKS - 1)
    ones = jnp.ones((SUM_ROWS, 3 * CHUNK), BF16)
    vals = [jnp.concatenate([vp_ref[b], vc_ref[b], vn_ref[b]], axis=1) for b in range(BATCH)]
    heads = {}
    for (b, j), s in zip(_ATTN_CASES, scores):
        s = jnp.concatenate([jnp.where(prev_ok, s[:CHUNK], NEG_INF), s[CHUNK:2 * CHUNK],
                             jnp.where(next_ok, s[2 * CHUNK:], NEG_INF)], axis=0)
        sink = sink_ref[j] * LOG2E
        m = jnp.maximum(jnp.max(s, axis=0, keepdims=True), sink)
        p = jnp.exp2(s - m).astype(BF16)
        v_aug = jnp.concatenate([vals[b][j * HEAD_DIM:(j + 1) * HEAD_DIM], ones], axis=0)
        o_aug = _dot(v_aug, p)
        o = o_aug[:HEAD_DIM] / (o_aug[HEAD_DIM:HEAD_DIM + 1] + jnp.exp2(sink - m))
        for g in range(GQA_GROUP):
            heads[b, j * GQA_GROUP + g] = o[:, g * CHUNK:(g + 1) * CHUNK].astype(BF16)
    return [jnp.concatenate([heads[b, hd] for hd in range(N_HEADS)], axis=0)
            for b in range(BATCH)]


def _mix_kernel(q_ref, kp_ref, kc_ref, kn_ref, vp_ref, vc_ref, vn_ref, sink_ref,
                y_ref, ga_ref, gs_ref, x_ref, wglu_ref, wab_ref, wsb_ref, wout_ref, o_ref):
    scores = _attn_scores(q_ref, kp_ref, kc_ref, kn_ref)
    nb = BATCH // MERGE_SPLIT
    tiles = [range(t * nb, (t + 1) * nb) for t in range(MERGE_SPLIT)]
    cat = lambda ref, bs: jnp.concatenate([ref[b] for b in bs], axis=1)
    y_t = [jnp.concatenate([y_ref[0, pl.ds(b, SSM_WIDTH, stride=BATCH), :] for b in bs],
                           axis=1) for bs in tiles]
    yg = [0.5 * y * (1.0 + jnp.tanh(math.sqrt(2.0 / math.pi) * (y + 0.044715 * y ** 3)))
          for y in y_t]
    z_t = [_dot(wglu_ref[...], g.astype(BF16)) for g in yg]
    ssm_t = [g * _sigmoid(z) for g, z in zip(yg, z_t)]
    s_t = [_dot(wsb_ref[...], t.astype(BF16)) for t in ssm_t]
    attn = _attn_finish(pl.program_id(0), scores, vp_ref, vc_ref, vn_ref, sink_ref)
    a_t = [_dot(wab_ref[...], jnp.concatenate([attn[b] for b in bs], axis=1)) for bs in tiles]
    merged = [cat(ga_ref, bs).astype(F32) * a + cat(gs_ref, bs).astype(F32) * sb
              for bs, a, sb in zip(tiles, a_t, s_t)]
    for bs, m in zip(tiles, merged):
        mb = m.astype(BF16)
        for r in range(0, D_MODEL, MXU_DEPTH):
            o = _dot(wout_ref[r:r + MXU_DEPTH, :], mb).T
            o_ref[bs.start:bs.stop, :, r:r + MXU_DEPTH] = (
                x_ref[bs.start:bs.stop, :, r:r + MXU_DEPTH] + o.reshape(nb, CHUNK, MXU_DEPTH))


def _mix(q_t, k_t, v_t, sink_rows, y4, ga_t, gs_t, x, wglu_t, wab_t, wsb_t, wout_t):
    prev = lambda c: (0, 0, jnp.maximum(c - 1, 0))
    cur = lambda c: (0, 0, c)
    nxt = lambda c: (0, 0, jnp.minimum(c + 1, N_CHUNKS - 1))
    tspec = lambda rows, im=cur: pl.BlockSpec((BATCH, rows, CHUNK), im)
    wspec = lambda r, k: pl.BlockSpec((r, k), lambda c: (0, 0), pipeline_mode=pl.Buffered(1))
    xspec = pl.BlockSpec((BATCH, CHUNK, D_MODEL), lambda c: (0, c, 0))
    return pl.pallas_call(
        _mix_kernel,
        grid=(N_CHUNKS,),
        in_specs=[tspec(ATTN_WIDTH),
                  tspec(KV_WIDTH, prev), tspec(KV_WIDTH), tspec(KV_WIDTH, nxt),
                  tspec(KV_WIDTH, prev), tspec(KV_WIDTH), tspec(KV_WIDTH, nxt),
                  pl.BlockSpec((N_KV_HEADS, 1, GQA_GROUP * CHUNK), lambda c: (0, 0, 0)),
                  pl.BlockSpec((1, SSM_WIDTH * BATCH, CHUNK), lambda c: (c, 0, 0)),
                  tspec(D_MODEL), tspec(D_MODEL), xspec,
                  wspec(SSM_WIDTH, SSM_WIDTH), wspec(D_MODEL, ATTN_WIDTH),
                  wspec(D_MODEL, SSM_WIDTH), wspec(D_MODEL, D_MODEL)],
        out_specs=xspec,
        out_shape=jax.ShapeDtypeStruct((BATCH, SEQ, D_MODEL), F32),
        compiler_params=pltpu.CompilerParams(vmem_limit_bytes=VMEM_LIMIT),
        name="mix",
    )(q_t, k_t, k_t, k_t, v_t, v_t, v_t, sink_rows, y4, ga_t, gs_t, x,
      wglu_t, wab_t, wsb_t, wout_t)


def _ffn_kernel(x_ref, g2_ref, wg_ref, wu_ref, wd_ref, gf_ref, o_ref):
    tile = FFN_ROWS // FFN_SPLIT
    rows = [slice(t * tile, (t + 1) * tile) for t in range(FFN_SPLIT)]
    h = [_rmsnorm(x_ref[r, :], g2_ref[...]).astype(BF16) for r in rows]
    acc = [jnp.zeros((tile, D_MODEL), F32) for _ in rows]
    for j in range(D_FF // FFN_COLS):
        cols = slice(j * FFN_COLS, (j + 1) * FFN_COLS)
        for t in range(FFN_SPLIT):
            gate = _dot(h[t], wg_ref[:, cols])
            up = _dot(h[t], wu_ref[:, cols])
            act = (gate * _sigmoid(gate) * up).astype(BF16)
            acc[t] = acc[t] + _dot(act, wd_ref[cols, :])
    for t, r in enumerate(rows):
        o_ref[r, :] = _rmsnorm(x_ref[r, :] + acc[t], gf_ref[...])


def _ffn(x, g2, wg, wu, wd, gf):
    n_tok = BATCH * SEQ
    row = pl.BlockSpec((FFN_ROWS, D_MODEL), lambda i: (i, 0))
    vec = pl.BlockSpec((1, D_MODEL), lambda i: (0, 0))
    wspec = lambda r, k: pl.BlockSpec((r, k), lambda i: (0, 0), pipeline_mode=pl.Buffered(1))
    return pl.pallas_call(
        _ffn_kernel,
        grid=(n_tok // FFN_ROWS,),
        in_specs=[row, vec, wspec(D_MODEL, D_FF), wspec(D_MODEL, D_FF),
                  wspec(D_FF, D_MODEL), vec],
        out_specs=row,
        out_shape=jax.ShapeDtypeStruct((n_tok, D_MODEL), F32),
        compiler_params=pltpu.CompilerParams(vmem_limit_bytes=VMEM_LIMIT),
        name="ffn",
    )(x, g2, wg, wu, wd, gf)


def _ssm_param_layouts(lam_re, lam_im, log_dt, b_re, b_im, c_re, c_im, d):
    both = lambda t: jnp.concatenate([t[0], t[1]], axis=-1)
    ldt = jnp.broadcast_to(log_dt[:, :, None], (2, SSM_GROUPS, SSM_STATE))
    rows = jnp.stack([both(lam_re), both(lam_im), both(ldt)], axis=1)
    rowp = jnp.pad(rows, ((0, 0), (0, SUBLANES - 3), (0, 0)))
    bt =jnp.concatenate([both(jnp.swapaxes(b_re, 2, 3)),
                          both(jnp.swapaxes(b_im, 2, 3))], axis=1)
    cw = jnp.concatenate([jnp.tile(c_re, (1, 1, 2)), jnp.tile(c_im, (1, 1, 2))], axis=1)
    ct = jnp.swapaxes(cw, 1, 2)
    drow = jnp.repeat(d, CHUNK, axis=-1)[:, None, :]
    return rowp, bt, cw, ct, drow


def kernel(x, norm1_g, w_in, attn_sink, ssm_lambda_re, ssm_lambda_im, ssm_log_dt, ssm_b_re, ssm_b_im, ssm_c_re, ssm_c_im, ssm_d, w_glu, w_attn_branch, w_ssm_branch, w_out, norm2_g, w_ffn_gate, w_ffn_up, w_ffn_down, norm_f_g):
    pos = jnp.arange(SEQ, dtype=F32)
    inv_freq = ROPE_THETA ** (-jnp.arange(0, ROPE_DIM, 2, dtype=F32) / ROPE_DIM)
    ang = inv_freq[:, None] * pos[None, :]
    cos_t, sin_t = jnp.cos(ang), jnp.sin(ang)
    sink_rows = jnp.repeat(attn_sink[0].astype(F32), CHUNK).reshape(
        N_KV_HEADS, 1, GQA_GROUP * CHUNK)

    rowp, bt, cw, ct, drow = _ssm_param_layouts(
        ssm_lambda_re[0], ssm_lambda_im[0], ssm_log_dt[0], ssm_b_re[0], ssm_b_im[0],
        ssm_c_re[0], ssm_c_im[0], ssm_d[0])
    q_t, k_t, v_t, u4, ga_t, gs_t, *tables = _in_proj(
        x, norm1_g[0][None, :], w_in[0].T.astype(BF16), cos_t, sin_t, rowp, bt, cw, ct)
    y4 = _ssm(u4, tables, drow)
    x1 = _mix(q_t, k_t, v_t, sink_rows, y4, ga_t, gs_t, x,
              w_glu[0].T.astype(BF16), w_attn_branch[0].T.astype(BF16),
              w_ssm_branch[0].T.astype(BF16), w_out[0].T.astype(BF16))
    out = _ffn(x1.reshape(BATCH * SEQ, D_MODEL), norm2_g[0][None, :],
               w_ffn_gate[0].astype(BF16), w_ffn_up[0].astype(BF16),
               w_ffn_down[0].astype(BF16), norm_f_g[None, :])
    return out.reshape(BATCH, SEQ, D_MODEL)
```

```python
import math

import jax
import jax.numpy as jnp
from jax import lax
from jax.experimental import pallas as pl
from jax.experimental.pallas import tpu as pltpu

D_MODEL = 1024
BATCH = 8
SEQ = 4096
N_HEADS = 8
N_KV_HEADS = 2
HEAD_DIM = 64
GQA_GROUP = N_HEADS // N_KV_HEADS
ATTN_WIDTH = N_HEADS * HEAD_DIM
KV_WIDTH = N_KV_HEADS * HEAD_DIM
ROPE_DIM = HEAD_DIM // 4
ROPE_HALF = ROPE_DIM // 2
ROPE_THETA = 500000.0
WINDOW = 128
SSM_GROUP_CH = 16
SSM_WIDTH = D_MODEL // 2
SSM_GROUPS = SSM_WIDTH // SSM_GROUP_CH
SSM_STATE = 64
D_FF = 2816
IN_COLS = ATTN_WIDTH + 2 * KV_WIDTH + SSM_WIDTH + 2 * D_MODEL
RMS_EPS = 1e-6
NEG_INF = -1e30

LANES = 128
SUBLANES = 8
MXU_DEPTH = 256
CHUNK = LANES
LOG2_CHUNK = CHUNK.bit_length() - 1
N_CHUNKS = SEQ // CHUNK
ROWS = N_CHUNKS * BATCH
FLAT = SSM_GROUP_CH * CHUNK
SSM_STEP_GROUPS = 4
MERGE_SPLIT = 4
LOG2E = math.log2(math.e)
SUM_ROWS = 16
FFN_ROWS = 1024
FFN_SPLIT = 2
FFN_COLS = 256
VMEM_LIMIT = 58 * 1024 * 1024

assert BATCH == SUBLANES and WINDOW == CHUNK and D_FF % FFN_COLS == 0

_Q0 = 0
_K0 = _Q0 + ATTN_WIDTH
_V0 = _K0 + KV_WIDTH
_U0 = _V0 + KV_WIDTH
_GA0 = _U0 + SSM_WIDTH
_GS0 = _GA0 + D_MODEL
assert _GS0 + D_MODEL == IN_COLS

BF16 = jnp.bfloat16
F32 = jnp.float32


def _sigmoid(x):
    return 0.5 * jnp.tanh(0.5 * x) + 0.5


def _rmsnorm(x, g):
    return x * lax.rsqrt(jnp.mean(x * x, axis=-1, keepdims=True) + RMS_EPS) * g


def _dot(a, b):
    return jnp.dot(a, b, preferred_element_type=F32)


def _in_proj_kernel(x_ref, g_ref, w_ref, cos_ref, sin_ref, rowp_ref, bt_ref, cw_ref, ct_ref,
                    q_ref, k_ref, v_ref, u_ref, ga_ref, gs_ref,
                    wf_ref, wb_ref, in_ref, out_ref, lc_ref):
    h = _rmsnorm(x_ref[...].reshape(BATCH * CHUNK, D_MODEL), g_ref[...]).astype(BF16)

    def proj_t(r0, r1):
        return lax.dot_general(w_ref[r0:r1, :], h, (((1,), (1,)), ((), ())),
                               preferred_element_type=F32)

    cos = jnp.concatenate([cos_ref[...]] * BATCH, axis=1)
    sin = jnp.concatenate([sin_ref[...]] * BATCH, axis=1)

    def rope_t(t, n_heads):
        parts = []
        for hd in range(n_heads):
            r1 = t[hd * HEAD_DIM:hd * HEAD_DIM + ROPE_HALF]
            r2 = t[hd * HEAD_DIM + ROPE_HALF:hd * HEAD_DIM + ROPE_DIM]
            parts += [r1 * cos - r2 * sin, r2 * cos + r1 * sin,
                      t[hd * HEAD_DIM + ROPE_DIM:(hd + 1) * HEAD_DIM]]
        return jnp.concatenate(parts, axis=0)

    def put(ref, r0, t):
        for b in range(BATCH):
            ref[b, r0:r0 + t.shape[0], :] = t[:, b * CHUNK:(b + 1) * CHUNK].astype(ref.dtype)

    slab = MXU_DEPTH
    for r in range(0, ATTN_WIDTH, slab):
        t = proj_t(_Q0 + r, _Q0 + r + slab) * (HEAD_DIM ** -0.5 * LOG2E)
        put(q_ref, r, rope_t(t, slab // HEAD_DIM))
    kv = proj_t(_K0, _U0)
    put(k_ref, 0, rope_t(kv[:KV_WIDTH], N_KV_HEADS))
    put(v_ref, 0, kv[KV_WIDTH:])
    for r in range(0, SSM_WIDTH, slab):
        u_t = proj_t(_U0 + r, _U0 + r + slab)
        for b in range(BATCH):
            u_ref[0, pl.ds(r * BATCH + b, slab, stride=BATCH), :] = (
                u_t[:, b * CHUNK:(b + 1) * CHUNK])
    for r in range(0, D_MODEL, slab):
        put(ga_ref, r, _sigmoid(proj_t(_GA0 + r, _GA0 + r + slab)))

    lc_r, lc_i = _ssm_tables(0, rowp_ref, bt_ref, cw_ref, ct_ref, wf_ref, wb_ref, in_ref, out_ref)
    lc_ref[0] = jnp.concatenate([lc_r, lc_i, jnp.zeros((SUBLANES - 2, LANES), F32)], axis=0)

    for r in range(0, D_MODEL, slab):
        put(gs_ref, r, _sigmoid(proj_t(_GS0 + r, _GS0 + r + slab)))


_PAIRS = SSM_GROUP_CH * SSM_GROUP_CH
_TABLES = [((_PAIRS, CHUNK), jnp.int32), ((_PAIRS, CHUNK), jnp.int32),
           ((FLAT, 2 * LANES), BF16), ((2 * LANES, FLAT), BF16), ((SUBLANES, LANES), F32)]
_PARAMS = [(SUBLANES, LANES), (2 * SSM_GROUP_CH, LANES), (2 * SSM_GROUP_CH, LANES),
           (LANES, 2 * SSM_GROUP_CH)]


def _group_spec(shape, n=1):
    return pl.BlockSpec((n,) + shape, lambda g: (g, 0, 0))


def _in_proj(x, g1, w_t, cos_t, sin_t, rowp, bt, cw, ct):
    assert N_CHUNKS == SSM_GROUPS
    tok = lambda rows, dt: jax.ShapeDtypeStruct((BATCH, rows, SEQ), dt)
    tspec = lambda rows: pl.BlockSpec((BATCH, rows, CHUNK), lambda c: (0, 0, c))
    return pl.pallas_call(
        _in_proj_kernel,
        grid=(N_CHUNKS,),
        in_specs=[
            pl.BlockSpec((BATCH, CHUNK, D_MODEL), lambda c: (0, c, 0)),
            pl.BlockSpec((1, D_MODEL), lambda c: (0, 0)),
            pl.BlockSpec((IN_COLS, D_MODEL), lambda c: (0, 0), pipeline_mode=pl.Buffered(1)),
            pl.BlockSpec((ROPE_HALF, CHUNK), lambda c: (0, c)),
            pl.BlockSpec((ROPE_HALF, CHUNK), lambda c: (0, c)),
        ] + [_group_spec(s) for s in _PARAMS],
        out_specs=[
            tspec(ATTN_WIDTH), tspec(KV_WIDTH), tspec(KV_WIDTH),
            pl.BlockSpec((1, SSM_WIDTH * BATCH, CHUNK), lambda c: (c, 0, 0)),
            tspec(D_MODEL), tspec(D_MODEL),
        ] + [_group_spec(s) for s, _ in _TABLES],
        out_shape=[
            tok(ATTN_WIDTH, BF16), tok(KV_WIDTH, BF16), tok(KV_WIDTH, BF16),
            jax.ShapeDtypeStruct((N_CHUNKS, SSM_WIDTH * BATCH, CHUNK), F32),
            tok(D_MODEL, BF16), tok(D_MODEL, BF16),
        ] + [jax.ShapeDtypeStruct((SSM_GROUPS,) + s, dt) for s, dt in _TABLES],
        compiler_params=pltpu.CompilerParams(vmem_limit_bytes=VMEM_LIMIT),
        name="in_proj",
    )(x, g1, w_t, cos_t, sin_t, rowp, bt, cw, ct)


def _cmul(a_r, a_i, b_r, b_i):
    return a_r * b_r - a_i * b_i, a_r * b_i + a_i * b_r


def _ssm_kernel(u_ref, wf_ref, wb_ref, in_ref, out_ref, lc_ref, drow_ref, y_ref):
    groups = range(SSM_STEP_GROUPS)
    data = [_ssm_states(g, (lc_ref[g, 0:1, :], lc_ref[g, 1:2, :]), u_ref, in_ref) for g in groups]
    for g in groups:
        _ssm_outputs(g, *data[g], drow_ref, y_ref, wf_ref, wb_ref, out_ref)


def _ssm_tables(g, rowp_ref, bt_ref, cw_ref, ct_ref, wf_ref, wb_ref, in_ref, out_ref):
    half = SSM_STATE
    top_col = lax.broadcasted_iota(jnp.int32, (LANES, 1), 0) < half

    lam_r, lam_i = rowp_ref[g, 0:1, :], rowp_ref[g, 1:2, :]
    dt = jnp.exp(rowp_ref[g, 2:3, :])
    z_r, z_i = lam_r * dt, lam_i * dt
    e1 = jnp.exp(z_r)
    lb_r, lb_i = e1 * jnp.cos(z_i), e1 * jnp.sin(z_i)
    den = lam_r * lam_r + lam_i * lam_i
    cf_r = ((lb_r - 1.0) * lam_r + lb_i * lam_i) / den
    cf_i = (lb_i * lam_r - (lb_r - 1.0) * lam_i) / den
    b_r, b_i = bt_ref[g, 0:SSM_GROUP_CH, :], bt_ref[g, SSM_GROUP_CH:, :]
    bb_r = cf_r * b_r - cf_i * b_i
    bb_i = cf_r * b_i + cf_i * b_r
    c_r, c_i = cw_ref[g, 0:SSM_GROUP_CH, :], cw_ref[g, SSM_GROUP_CH:, :]

    sq_r, sq_i = [lb_r], [lb_i]
    for _ in range(LOG2_CHUNK):
        r, i = _cmul(sq_r[-1], sq_i[-1], sq_r[-1], sq_i[-1])
        sq_r.append(r)
        sq_i.append(i)
    lc_r, lc_i = sq_r[LOG2_CHUNK], sq_i[LOG2_CHUNK]
    pad = jnp.zeros((LANES - 2 * SUBLANES, LANES), F32)
    cols = jnp.concatenate(sq_r[:SUBLANES] + sq_i[:SUBLANES] + [pad], axis=0).T
    sq_rc = [cols[:, k:k + 1] for k in range(LOG2_CHUNK)]
    sq_ic = [cols[:, SUBLANES + k:SUBLANES + k + 1] for k in range(LOG2_CHUNK)]

    j_idx = lax.broadcasted_iota(jnp.int32, (LANES, CHUNK), 1)
    one, zero = jnp.ones((LANES, CHUNK), F32), jnp.zeros((LANES, CHUNK), F32)
    asc_r, asc_i, dsc_r, dsc_i = one, zero, one, zero
    for k in range(LOG2_CHUNK):
        bit = (j_idx & (1 << k)) != 0
        n_r, n_i = _cmul(asc_r, asc_i, sq_rc[k], sq_ic[k])
        asc_r, asc_i = jnp.where(bit, n_r, asc_r), jnp.where(bit, n_i, asc_i)
        n_r, n_i = _cmul(dsc_r, dsc_i, sq_rc[k], sq_ic[k])
        dsc_r, dsc_i = jnp.where(bit, dsc_r, n_r), jnp.where(bit, dsc_i, n_i)
    q_r, q_i = _cmul(jnp.where(top_col, asc_r, dsc_r), jnp.where(top_col, asc_i, dsc_i),
                     sq_rc[0], sq_ic[0])
    p_r = jnp.where(top_col, dsc_r, asc_r).T
    p_i = jnp.where(top_col, dsc_i, asc_i).T

    first = j_idx == 0
    ef_r = jnp.where(top_col, asc_r, jnp.where(first, 1.0, 0.0))
    ef_i = jnp.where(top_col, asc_i, 0.0)
    eb_r = jnp.where(top_col, 0.0, jnp.where(first, 0.0, q_r))
    eb_i = jnp.where(top_col, 0.0, jnp.where(first, 0.0, q_i))
    e_r = jnp.concatenate([eb_r, ef_r], axis=1)
    e_i = jnp.concatenate([eb_i, ef_i], axis=1)
    cb_r = jnp.concatenate([c_r * bb_r[h:h + 1] - c_i * bb_i[h:h + 1]
                            for h in range(SSM_GROUP_CH)], axis=0)
    cb_i = jnp.concatenate([c_r * bb_i[h:h + 1] + c_i * bb_r[h:h + 1]
                            for h in range(SSM_GROUP_CH)], axis=0)
    split = lambda t: (t.astype(BF16), (t - t.astype(BF16).astype(F32)).astype(BF16))
    (cr_h, cr_l), (ci_h, ci_l) = split(cb_r), split(-cb_i)
    (er_h, er_l), (ei_h, ei_l) = split(e_r), split(e_i)
    ktab = _dot(jnp.concatenate([cr_h, cr_h, cr_l, ci_h, ci_h, ci_l], axis=1),
                jnp.concatenate([er_h, er_l, er_h, ei_h, ei_l, ei_h], axis=0))
    kb, kf = ktab[:, :CHUNK], ktab[:, CHUNK:]
    kb1 = pltpu.roll(kb, 1, 1)
    kf1 = jnp.where(lax.broadcasted_iota(jnp.int32, (1, CHUNK), 1) == 0, kb1,
                    pltpu.roll(kf, 1, 1))
    bf16_bits = lambda t: lax.bitcast_convert_type(t.astype(BF16).astype(F32), jnp.int32)
    pack = lambda even, odd: bf16_bits(odd) | lax.shift_right_logical(bf16_bits(even), 16)
    wf_ref[g] = pack(kf, kf1)
    wb_ref[g] = pack(kb, kb1)

    for h in range(SSM_GROUP_CH):
        rows = slice(h * CHUNK, (h + 1) * CHUNK)
        in_ref[g, rows, 0:LANES] = (p_r * bb_r[h:h + 1] - p_i * bb_i[h:h + 1]).astype(BF16)
        in_ref[g, rows, LANES:] = (p_r * bb_i[h:h + 1] + p_i * bb_r[h:h + 1]).astype(BF16)

    for hp in range(SSM_GROUP_CH):
        cols = slice(hp * CHUNK, (hp + 1) * CHUNK)
        cc_r = ct_ref[g, :, hp:hp + 1]
        cc_i = ct_ref[g, :, SSM_GROUP_CH + hp:SSM_GROUP_CH + hp + 1]
        out_ref[g, 0:LANES, cols] = (cc_r * q_r - cc_i * q_i).astype(BF16)
        out_ref[g, LANES:, cols] = (-(cc_r * q_i + cc_i * q_r)).astype(BF16)
    return lc_r, lc_i


def _ssm_states(g, lam_chunk, u_ref, in_ref):
    lc_r, lc_i = lam_chunk
    lo_row = lax.broadcasted_iota(jnp.int32, (1, LANES), 1) < SSM_STATE
    row0 = g * SSM_GROUP_CH * BATCH
    a = jnp.concatenate(
        [u_ref[:, row0 + h * BATCH:row0 + (h + 1) * BATCH, :].reshape(ROWS, CHUNK)
         for h in range(SSM_GROUP_CH)], axis=1)
    ab = a.astype(BF16)
    e = _dot(ab, in_ref[g])
    s_r = jnp.zeros((BATCH, LANES), F32)
    s_i = jnp.zeros((BATCH, LANES), F32)
    hist_r, hist_i = [], []
    for i in range(N_CHUNKS):
        hist_r.append(s_r)
        hist_i.append(s_i)
        fr = slice(i * BATCH, (i + 1) * BATCH)
        br = slice((N_CHUNKS - 1 - i) * BATCH, (N_CHUNKS - i) * BATCH)
        in_r = jnp.where(lo_row, e[fr, 0:LANES], e[br, 0:LANES])
        in_i = jnp.where(lo_row, e[fr, LANES:], e[br, LANES:])
        s_r, s_i = (s_r * lc_r - s_i * lc_i + in_r, s_r * lc_i + s_i * lc_r + in_i)
    states = jnp.concatenate(
        [jnp.concatenate([jnp.where(lo_row, hist_r[c], hist_r[N_CHUNKS - 1 - c]),
                          jnp.where(lo_row, hist_i[c], hist_i[N_CHUNKS - 1 - c])], axis=1)
         for c in range(N_CHUNKS)], axis=0).astype(BF16)
    return a, ab, states


def _ssm_outputs(g, a, ab, states, drow_ref, y_ref, wf_ref, wb_ref, out_ref):
    row0 = g * SSM_GROUP_CH * BATCH

    pair = lax.broadcasted_iota(jnp.int32, (CHUNK // 2, CHUNK), 0)
    lane = lax.broadcasted_iota(jnp.int32, (CHUNK // 2, CHUNK), 1)
    upper = lane + 2 * pair < CHUNK
    per_slab = MXU_DEPTH // CHUNK

    def toeplitz_slab(slab):
        col_blocks = []
        for h in range(SSM_GROUP_CH):
            blocks = []
            for hp in range(slab * per_slab, (slab + 1) * per_slab):
                row = pl.ds(h * SSM_GROUP_CH + hp, 1)
                w = jnp.where(upper,
                              jnp.broadcast_to(wf_ref[g, row, :], (CHUNK // 2, CHUNK)),
                              jnp.broadcast_to(wb_ref[g, row, :], (CHUNK // 2, CHUNK)))
                w = pltpu.roll(w, 0, 1, stride=2, stride_axis=0)
                blocks.append(pltpu.bitcast(w, BF16))
            col_blocks.append(jnp.concatenate(blocks, axis=1))
        return jnp.concatenate(col_blocks, axis=0)

    for slab in range(SSM_GROUP_CH // per_slab):
        cols = slice(slab * MXU_DEPTH, (slab + 1) * MXU_DEPTH)
        y = (_dot(ab, toeplitz_slab(slab)) + _dot(states, out_ref[g, :, cols])
             + a[:, cols] * drow_ref[g][:, cols])
        for k in range(per_slab):
            hp = slab * per_slab + k
            y_ref[:, row0 + hp * BATCH:row0 + (hp + 1) * BATCH, :] = (
                y[:, k * CHUNK:(k + 1) * CHUNK].reshape(N_CHUNKS, BATCH, CHUNK))


def _ssm(u4, tables, drow):
    ng = SSM_STEP_GROUPS
    data = pl.BlockSpec((N_CHUNKS, ng * SSM_GROUP_CH * BATCH, CHUNK), lambda g: (0, g, 0))
    return pl.pallas_call(
        _ssm_kernel,
        grid=(SSM_GROUPS // ng,),
        in_specs=[data] + [_group_spec(s, ng) for s, _ in _TABLES] + [_group_spec((1, FLAT), ng)],
        out_specs=data,
        out_shape=jax.ShapeDtypeStruct((N_CHUNKS, SSM_WIDTH * BATCH, CHUNK), F32),
        compiler_params=pltpu.CompilerParams(vmem_limit_bytes=VMEM_LIMIT),
        name="ssm",
    )(u4, *tables, drow)


_ATTN_CASES = [(b, j) for b in range(BATCH) for j in range(N_KV_HEADS)]


def _attn_scores(q_ref, kp_ref, kc_ref, kn_ref):
    zeros = jnp.zeros((HEAD_DIM, GQA_GROUP * CHUNK), BF16)
    keys = []
    for b in range(BATCH):
        k_t = jnp.concatenate([kp_ref[b], kc_ref[b], kn_ref[b]], axis=1)
        keys.append(k_t.T)
    scores = []
    for b, j in _ATTN_CASES:
        q_t = jnp.concatenate(
            [q_ref[b, (j * GQA_GROUP + g) * HEAD_DIM:(j * GQA_GROUP + g + 1) * HEAD_DIM, :]
             for g in range(GQA_GROUP)], axis=1)
        q_pad = jnp.concatenate([q_t, zeros] if j == 0 else [zeros, q_t], axis=0)
        scores.append(_dot(keys[b], q_pad))
    return scores


def _attn_finish(batches, c, scores, vp_ref, vc_ref, vn_ref, sink_ref):
    kk = lax.broadcasted_iota(jnp.int32, (CHUNK, CHUNK), 0)
    qq = lax.broadcasted_iota(jnp.int32, (CHUNK, CHUNK), 1)
    prev_ok = jnp.concatenate([kk >= qq] * GQA_GROUP, axis=1) & (c > 0)
    next_ok = jnp.concatenate([kk <= qq] * GQA_GROUP, axis=1) & (c < N_CHUNKS - 1)
    ones = jnp.ones((SUM_ROWS, 3 * CHUNK), BF16)
    heads = {}
    cases = [(b, j) for b in batches for j in range(N_KV_HEADS)]
    sinks = [sink_ref[j] * LOG2E for _, j in cases]
    masked = []
    for b, j in cases:
        s = scores[_ATTN_CASES.index((b, j))]
        masked.append(jnp.concatenate(
            [jnp.where(prev_ok, s[:CHUNK], NEG_INF), s[CHUNK:2 * CHUNK],
             jnp.where(next_ok, s[2 * CHUNK:], NEG_INF)], axis=0))
    maxes = [jnp.maximum(jnp.max(s, axis=0, keepdims=True), sk) for s, sk in zip(masked, sinks)]
    probs = [jnp.exp2(s - m).astype(BF16) for s, m in zip(masked, maxes)]
    for (b, j), p, m, sk in zip(cases, probs, maxes, sinks):
        v_t = jnp.concatenate([vp_ref[b], vc_ref[b], vn_ref[b]], axis=1)
        v_aug = jnp.concatenate([v_t[j * HEAD_DIM:(j + 1) * HEAD_DIM], ones], axis=0)
        o_aug = _dot(v_aug, p)
        o = o_aug[:HEAD_DIM] / (o_aug[HEAD_DIM:HEAD_DIM + 1] + jnp.exp2(sk - m))
        for g in range(GQA_GROUP):
            heads[b, j * GQA_GROUP + g] = o[:, g * CHUNK:(g + 1) * CHUNK].astype(BF16)
    return jnp.concatenate(
        [jnp.concatenate([heads[b, hd] for hd in range(N_HEADS)], axis=0) for b in batches],
        axis=1)


def _mix_kernel(q_ref, kp_ref, kc_ref, kn_ref, vp_ref, vc_ref, vn_ref, sink_ref,
                y_ref, ga_ref, gs_ref, x_ref, wglu_ref, wab_ref, wsb_ref, wout_ref, o_ref):
    scores = _attn_scores(q_ref, kp_ref, kc_ref, kn_ref)
    nb = BATCH // MERGE_SPLIT
    tiles = [range(t * nb, (t + 1) * nb) for t in range(MERGE_SPLIT)]
    cat = lambda ref, bs: jnp.concatenate([ref[b] for b in bs], axis=1)
    y_t = [jnp.concatenate([y_ref[0, pl.ds(b, SSM_WIDTH, stride=BATCH), :] for b in bs],
                           axis=1) for bs in tiles]
    yg = [0.5 * y * (1.0 + jnp.tanh(math.sqrt(2.0 / math.pi) * (y + 0.044715 * y ** 3)))
          for y in y_t]
    z_t = [_dot(wglu_ref[...], g.astype(BF16)) for g in yg]
    ssm_t = [g * _sigmoid(z) for g, z in zip(yg, z_t)]
    s_t = [_dot(wsb_ref[...], t.astype(BF16)) for t in ssm_t]
    attn = [_attn_finish(bs, pl.program_id(0), scores, vp_ref, vc_ref, vn_ref, sink_ref)
            for bs in tiles]
    a_t = [_dot(wab_ref[...], t) for t in attn]
    merged = [cat(ga_ref, bs).astype(F32) * a + cat(gs_ref, bs).astype(F32) * sb
              for bs, a, sb in zip(tiles, a_t, s_t)]
    for bs, m in zip(tiles, merged):
        mb = m.astype(BF16)
        for r in range(0, D_MODEL, MXU_DEPTH):
            o = _dot(wout_ref[r:r + MXU_DEPTH, :], mb).T
            o_ref[bs.start:bs.stop, :, r:r + MXU_DEPTH] = (
                x_ref[bs.start:bs.stop, :, r:r + MXU_DEPTH] + o.reshape(nb, CHUNK, MXU_DEPTH))


def _mix(q_t, k_t, v_t, sink_rows, y4, ga_t, gs_t, x, wglu_t, wab_t, wsb_t, wout_t):
    prev = lambda c: (0, 0, jnp.maximum(c - 1, 0))
    cur = lambda c: (0, 0, c)
    nxt = lambda c: (0, 0, jnp.minimum(c + 1, N_CHUNKS - 1))
    tspec = lambda rows, im=cur: pl.BlockSpec((BATCH, rows, CHUNK), im)
    wspec = lambda r, k: pl.BlockSpec((r, k), lambda c: (0, 0), pipeline_mode=pl.Buffered(1))
    xspec = pl.BlockSpec((BATCH, CHUNK, D_MODEL), lambda c: (0, c, 0))
    return pl.pallas_call(
        _mix_kernel,
        grid=(N_CHUNKS,),
        in_specs=[tspec(ATTN_WIDTH),
                  tspec(KV_WIDTH, prev), tspec(KV_WIDTH), tspec(KV_WIDTH, nxt),
                  tspec(KV_WIDTH, prev), tspec(KV_WIDTH), tspec(KV_WIDTH, nxt),
                  pl.BlockSpec((N_KV_HEADS, 1, GQA_GROUP * CHUNK), lambda c: (0, 0, 0)),
                  pl.BlockSpec((1, SSM_WIDTH * BATCH, CHUNK), lambda c: (c, 0, 0)),
                  tspec(D_MODEL), tspec(D_MODEL), xspec,
                  wspec(SSM_WIDTH, SSM_WIDTH), wspec(D_MODEL, ATTN_WIDTH),
                  wspec(D_MODEL, SSM_WIDTH), wspec(D_MODEL, D_MODEL)],
        out_specs=xspec,
        out_shape=jax.ShapeDtypeStruct((BATCH, SEQ, D_MODEL), F32),
        compiler_params=pltpu.CompilerParams(vmem_limit_bytes=VMEM_LIMIT),
        name="mix",
    )(q_t, k_t, k_t, k_t, v_t, v_t, v_t, sink_rows, y4, ga_t, gs_t, x,
      wglu_t, wab_t, wsb_t, wout_t)


def _ffn_kernel(x_ref, g2_ref, wg_ref, wu_ref, wd_ref, gf_ref, o_ref):
    tile = FFN_ROWS // FFN_SPLIT
    rows = [slice(t * tile, (t + 1) * tile) for t in range(FFN_SPLIT)]
    h = [_rmsnorm(x_ref[r, :], g2_ref[...]).astype(BF16) for r in rows]
    acc = [jnp.zeros((tile, D_MODEL), F32) for _ in rows]
    for j in range(D_FF // FFN_COLS):
        cols = slice(j * FFN_COLS, (j + 1) * FFN_COLS)
        for t in range(FFN_SPLIT):
            gate = _dot(h[t], wg_ref[:, cols])
            up = _dot(h[t], wu_ref[:, cols])
            act = (gate * _sigmoid(gate) * up).astype(BF16)
            acc[t] = acc[t] + _dot(act, wd_ref[cols, :])
    for t, r in enumerate(rows):
        o_ref[r, :] = _rmsnorm(x_ref[r, :] + acc[t], gf_ref[...])


def _ffn(x, g2, wg, wu, wd, gf):
    n_tok = BATCH * SEQ
    row = pl.BlockSpec((FFN_ROWS, D_MODEL), lambda i: (i, 0))
    vec = pl.BlockSpec((1, D_MODEL), lambda i: (0, 0))
    wspec = lambda r, k: pl.BlockSpec((r, k), lambda i: (0, 0), pipeline_mode=pl.Buffered(1))
    return pl.pallas_call(
        _ffn_kernel,
        grid=(n_tok // FFN_ROWS,),
        in_specs=[row, vec, wspec(D_MODEL, D_FF), wspec(D_MODEL, D_FF),
                  wspec(D_FF, D_MODEL), vec],
        out_specs=row,
        out_shape=jax.ShapeDtypeStruct((n_tok, D_MODEL), F32),
        compiler_params=pltpu.CompilerParams(vmem_limit_bytes=VMEM_LIMIT),
        name="ffn",
    )(x, g2, wg, wu, wd, gf)


def _ssm_param_layouts(lam_re, lam_im, log_dt, b_re, b_im, c_re, c_im, d):
    both = lambda t: jnp.concatenate([t[0], t[1]], axis=-1)
    ldt = jnp.broadcast_to(log_dt[:, :, None], (2, SSM_GROUPS, SSM_STATE))
    rows = jnp.stack([both(lam_re), both(lam_im), both(ldt)], axis=1)
    rowp = jnp.pad(rows, ((0, 0), (0, SUBLANES - 3), (0, 0)))
    bt =jnp.concatenate([both(jnp.swapaxes(b_re, 2, 3)),
                          both(jnp.swapaxes(b_im, 2, 3))], axis=1)
    cw = jnp.concatenate([jnp.tile(c_re, (1, 1, 2)), jnp.tile(c_im, (1, 1, 2))], axis=1)
    ct = jnp.swapaxes(cw, 1, 2)
    drow = jnp.repeat(d, CHUNK, axis=-1)[:, None, :]
    return rowp, bt, cw, ct, drow


def kernel(x, norm1_g, w_in, attn_sink, ssm_lambda_re, ssm_lambda_im, ssm_log_dt, ssm_b_re, ssm_b_im, ssm_c_re, ssm_c_im, ssm_d, w_glu, w_attn_branch, w_ssm_branch, w_out, norm2_g, w_ffn_gate, w_ffn_up, w_ffn_down, norm_f_g):
    pos = jnp.arange(SEQ, dtype=F32)
    inv_freq = ROPE_THETA ** (-jnp.arange(0, ROPE_DIM, 2, dtype=F32) / ROPE_DIM)
    ang = inv_freq[:, None] * pos[None, :]
    cos_t, sin_t = jnp.cos(ang), jnp.sin(ang)
    sink_rows = jnp.repeat(attn_sink[0], CHUNK).reshape(
        N_KV_HEADS, 1, GQA_GROUP * CHUNK)

    rowp, bt, cw, ct, drow = _ssm_param_layouts(
        ssm_lambda_re[0], ssm_lambda_im[0], ssm_log_dt[0], ssm_b_re[0], ssm_b_im[0],
        ssm_c_re[0], ssm_c_im[0], ssm_d[0])
    q_t, k_t, v_t, u4, ga_t, gs_t, *tables = _in_proj(
        x, norm1_g[0][None, :], w_in[0].T.astype(BF16), cos_t, sin_t, rowp, bt, cw, ct)
    y4 = _ssm(u4, tables, drow)
    x1 = _mix(q_t, k_t, v_t, sink_rows, y4, ga_t, gs_t, x,
              w_glu[0].T.astype(BF16), w_attn_branch[0].T.astype(BF16),
              w_ssm_branch[0].T.astype(BF16), w_out[0].T.astype(BF16))
    out = _ffn(x1.reshape(BATCH * SEQ, D_MODEL), norm2_g[0][None, :],
               w_ffn_gate[0].astype(BF16), w_ffn_up[0].astype(BF16),
               w_ffn_down[0].astype(BF16), norm_f_g[None, :])
    return out.reshape(BATCH, SEQ, D_MODEL)
```

```python
import math

import jax
import jax.numpy as jnp
from jax import lax
from jax.experimental import pallas as pl
from jax.experimental.pallas import tpu as pltpu

D_MODEL = 1024
BATCH = 8
SEQ = 4096
N_HEADS = 8
N_KV_HEADS = 2
HEAD_DIM = 64
GQA_GROUP = N_HEADS // N_KV_HEADS
ATTN_WIDTH = N_HEADS * HEAD_DIM
KV_WIDTH = N_KV_HEADS * HEAD_DIM
ROPE_DIM = HEAD_DIM // 4
ROPE_HALF = ROPE_DIM // 2
ROPE_THETA = 500000.0
WINDOW = 128
SSM_GROUP_CH = 16
SSM_WIDTH = D_MODEL // 2
SSM_GROUPS = SSM_WIDTH // SSM_GROUP_CH
SSM_STATE = 64
D_FF = 2816
IN_COLS = ATTN_WIDTH + 2 * KV_WIDTH + SSM_WIDTH + 2 * D_MODEL
RMS_EPS = 1e-6
NEG_INF = -1e30

LANES = 128
SUBLANES = 8
MXU_DEPTH = 256
CHUNK = LANES
LOG2_CHUNK = CHUNK.bit_length() - 1
N_CHUNKS = SEQ // CHUNK
ROWS = N_CHUNKS * BATCH
FLAT = SSM_GROUP_CH * CHUNK
SSM_STEP_GROUPS = 4
MERGE_SPLIT = 4
LOG2E = math.log2(math.e)
SUM_ROWS = 16
FFN_ROWS = 1024
FFN_SPLIT = 2
FFN_COLS = 256
VMEM_LIMIT = 58 * 1024 * 1024

assert BATCH == SUBLANES and WINDOW == CHUNK and D_FF % FFN_COLS == 0

_Q0 = 0
_K0 = _Q0 + ATTN_WIDTH
_V0 = _K0 + KV_WIDTH
_U0 = _V0 + KV_WIDTH
_GA0 = _U0 + SSM_WIDTH
_GS0 = _GA0 + D_MODEL
assert _GS0 + D_MODEL == IN_COLS

BF16 = jnp.bfloat16
F32 = jnp.float32


def _sigmoid(x):
    return 0.5 * jnp.tanh(0.5 * x) + 0.5


def _rmsnorm(x, g):
    return x * lax.rsqrt(jnp.mean(x * x, axis=-1, keepdims=True) + RMS_EPS) * g


def _dot(a, b):
    return jnp.dot(a, b, preferred_element_type=F32)


def _in_proj_kernel(x_ref, g_ref, w_ref, cos_ref, sin_ref, rowp_ref, bt_ref, cw_ref, ct_ref,
                    q_ref, k_ref, v_ref, u_ref, ga_ref, gs_ref,
                    wf_ref, wb_ref, in_ref, out_ref, lc_ref):
    h = _rmsnorm(x_ref[...].reshape(BATCH * CHUNK, D_MODEL), g_ref[...]).astype(BF16)

    def proj_t(r0, r1):
        return lax.dot_general(w_ref[r0:r1, :], h, (((1,), (1,)), ((), ())),
                               preferred_element_type=F32)

    cos = jnp.concatenate([cos_ref[...]] * BATCH, axis=1)
    sin = jnp.concatenate([sin_ref[...]] * BATCH, axis=1)

    def rope_t(t, n_heads):
        parts = []
        for hd in range(n_heads):
            r1 = t[hd * HEAD_DIM:hd * HEAD_DIM + ROPE_HALF]
            r2 = t[hd * HEAD_DIM + ROPE_HALF:hd * HEAD_DIM + ROPE_DIM]
            parts += [r1 * cos - r2 * sin, r2 * cos + r1 * sin,
                      t[hd * HEAD_DIM + ROPE_DIM:(hd + 1) * HEAD_DIM]]
        return jnp.concatenate(parts, axis=0)

    def put(ref, r0, t):
        for b in range(BATCH):
            ref[b, r0:r0 + t.shape[0], :] = t[:, b * CHUNK:(b + 1) * CHUNK].astype(ref.dtype)

    slab = MXU_DEPTH
    for r in range(0, ATTN_WIDTH, slab):
        t = proj_t(_Q0 + r, _Q0 + r + slab) * (HEAD_DIM ** -0.5 * LOG2E)
        put(q_ref, r, rope_t(t, slab // HEAD_DIM))
    kv = proj_t(_K0, _U0)
    put(k_ref, 0, rope_t(kv[:KV_WIDTH], N_KV_HEADS))
    put(v_ref, 0, kv[KV_WIDTH:])
    for r in range(0, SSM_WIDTH, slab):
        u_t = proj_t(_U0 + r, _U0 + r + slab)
        for b in range(BATCH):
            u_ref[0, pl.ds(r * BATCH + b, slab, stride=BATCH), :] = (
                u_t[:, b * CHUNK:(b + 1) * CHUNK])
    for r in range(0, D_MODEL, slab):
        put(ga_ref, r, _sigmoid(proj_t(_GA0 + r, _GA0 + r + slab)))

    lc_r, lc_i = _ssm_tables(0, rowp_ref, bt_ref, cw_ref, ct_ref, wf_ref, wb_ref, in_ref, out_ref)
    lc_ref[0] = jnp.concatenate([lc_r, lc_i, jnp.zeros((SUBLANES - 2, LANES), F32)], axis=0)

    for r in range(0, D_MODEL, slab):
        put(gs_ref, r, _sigmoid(proj_t(_GS0 + r, _GS0 + r + slab)))


_PAIRS = SSM_GROUP_CH * SSM_GROUP_CH
_TABLES = [((_PAIRS, CHUNK), jnp.int32), ((_PAIRS, CHUNK), jnp.int32),
           ((FLAT, 2 * LANES), BF16), ((2 * LANES, FLAT), BF16), ((SUBLANES, LANES), F32)]
_PARAMS = [(SUBLANES, LANES), (2 * SSM_GROUP_CH, LANES), (2 * SSM_GROUP_CH, LANES),
           (LANES, 2 * SSM_GROUP_CH)]


def _group_spec(shape, n=1):
    return pl.BlockSpec((n,) + shape, lambda g: (g, 0, 0))


def _in_proj(x, g1, w_t, cos_t, sin_t, rowp, bt, cw, ct):
    assert N_CHUNKS == SSM_GROUPS
    tok = lambda rows, dt: jax.ShapeDtypeStruct((BATCH, rows, SEQ), dt)
    tspec = lambda rows: pl.BlockSpec((BATCH, rows, CHUNK), lambda c: (0, 0, c))
    return pl.pallas_call(
        _in_proj_kernel,
        grid=(N_CHUNKS,),
        in_specs=[
            pl.BlockSpec((BATCH, CHUNK, D_MODEL), lambda c: (0, c, 0)),
            pl.BlockSpec((1, D_MODEL), lambda c: (0, 0)),
            pl.BlockSpec((IN_COLS, D_MODEL), lambda c: (0, 0), pipeline_mode=pl.Buffered(1)),
            pl.BlockSpec((ROPE_HALF, CHUNK), lambda c: (0, c)),
            pl.BlockSpec((ROPE_HALF, CHUNK), lambda c: (0, c)),
        ] + [_group_spec(s) for s in _PARAMS],
        out_specs=[
            tspec(ATTN_WIDTH), tspec(KV_WIDTH), tspec(KV_WIDTH),
            pl.BlockSpec((1, SSM_WIDTH * BATCH, CHUNK), lambda c: (c, 0, 0)),
            tspec(D_MODEL), tspec(D_MODEL),
        ] + [_group_spec(s) for s, _ in _TABLES],
        out_shape=[
            tok(ATTN_WIDTH, BF16), tok(KV_WIDTH, BF16), tok(KV_WIDTH, BF16),
            jax.ShapeDtypeStruct((N_CHUNKS, SSM_WIDTH * BATCH, CHUNK), F32),
            tok(D_MODEL, BF16), tok(D_MODEL, BF16),
        ] + [jax.ShapeDtypeStruct((SSM_GROUPS,) + s, dt) for s, dt in _TABLES],
        compiler_params=pltpu.CompilerParams(vmem_limit_bytes=VMEM_LIMIT),
        name="in_proj",
    )(x, g1, w_t, cos_t, sin_t, rowp, bt, cw, ct)


def _cmul(a_r, a_i, b_r, b_i):
    return a_r * b_r - a_i * b_i, a_r * b_i + a_i * b_r


def _ssm_kernel(u_ref, wf_ref, wb_ref, in_ref, out_ref, lc_ref, drow_ref, y_ref):
    groups = range(SSM_STEP_GROUPS)
    data = [_ssm_states(g, (lc_ref[g, 0:1, :], lc_ref[g, 1:2, :]), u_ref, in_ref) for g in groups]
    for g in groups:
        _ssm_outputs(g, *data[g], drow_ref, y_ref, wf_ref, wb_ref, out_ref)


def _ssm_tables(g, rowp_ref, bt_ref, cw_ref, ct_ref, wf_ref, wb_ref, in_ref, out_ref):
    half = SSM_STATE
    top_col = lax.broadcasted_iota(jnp.int32, (LANES, 1), 0) < half

    lam_r, lam_i = rowp_ref[g, 0:1, :], rowp_ref[g, 1:2, :]
    dt = jnp.exp(rowp_ref[g, 2:3, :])
    z_r, z_i = lam_r * dt, lam_i * dt
    e1 = jnp.exp(z_r)
    lb_r, lb_i = e1 * jnp.cos(z_i), e1 * jnp.sin(z_i)
    den = lam_r * lam_r + lam_i * lam_i
    cf_r = ((lb_r - 1.0) * lam_r + lb_i * lam_i) / den
    cf_i = (lb_i * lam_r - (lb_r - 1.0) * lam_i) / den
    b_r, b_i = bt_ref[g, 0:SSM_GROUP_CH, :], bt_ref[g, SSM_GROUP_CH:, :]
    bb_r = cf_r * b_r - cf_i * b_i
    bb_i = cf_r * b_i + cf_i * b_r
    c_r, c_i = cw_ref[g, 0:SSM_GROUP_CH, :], cw_ref[g, SSM_GROUP_CH:, :]

    sq_r, sq_i = [lb_r], [lb_i]
    for _ in range(LOG2_CHUNK):
        r, i = _cmul(sq_r[-1], sq_i[-1], sq_r[-1], sq_i[-1])
        sq_r.append(r)
        sq_i.append(i)
    lc_r, lc_i = sq_r[LOG2_CHUNK], sq_i[LOG2_CHUNK]
    pad = jnp.zeros((LANES - 2 * SUBLANES, LANES), F32)
    cols = jnp.concatenate(sq_r[:SUBLANES] + sq_i[:SUBLANES] + [pad], axis=0).T
    sq_rc = [cols[:, k:k + 1] for k in range(LOG2_CHUNK)]
    sq_ic = [cols[:, SUBLANES + k:SUBLANES + k + 1] for k in range(LOG2_CHUNK)]

    j_idx = lax.broadcasted_iota(jnp.int32, (LANES, CHUNK), 1)
    one, zero = jnp.ones((LANES, CHUNK), F32), jnp.zeros((LANES, CHUNK), F32)
    asc_r, asc_i, dsc_r, dsc_i = one, zero, one, zero
    for k in range(LOG2_CHUNK):
        bit = (j_idx & (1 << k)) != 0
        n_r, n_i = _cmul(asc_r, asc_i, sq_rc[k], sq_ic[k])
        asc_r, asc_i = jnp.where(bit, n_r, asc_r), jnp.where(bit, n_i, asc_i)
        n_r, n_i = _cmul(dsc_r, dsc_i, sq_rc[k], sq_ic[k])
        dsc_r, dsc_i = jnp.where(bit, dsc_r, n_r), jnp.where(bit, dsc_i, n_i)
    q_r, q_i = _cmul(jnp.where(top_col, asc_r, dsc_r), jnp.where(top_col, asc_i, dsc_i),
                     sq_rc[0], sq_ic[0])
    p_r = jnp.where(top_col, dsc_r, asc_r).T
    p_i = jnp.where(top_col, dsc_i, asc_i).T

    first = j_idx == 0
    ef_r = jnp.where(top_col, asc_r, jnp.where(first, 1.0, 0.0))
    ef_i = jnp.where(top_col, asc_i, 0.0)
    eb_r = jnp.where(top_col, 0.0, jnp.where(first, 0.0, q_r))
    eb_i = jnp.where(top_col, 0.0, jnp.where(first, 0.0, q_i))
    e_r = jnp.concatenate([eb_r, ef_r], axis=1)
    e_i = jnp.concatenate([eb_i, ef_i], axis=1)
    cb_r = jnp.concatenate([c_r * bb_r[h:h + 1] - c_i * bb_i[h:h + 1]
                            for h in range(SSM_GROUP_CH)], axis=0)
    cb_i = jnp.concatenate([c_r * bb_i[h:h + 1] + c_i * bb_r[h:h + 1]
                            for h in range(SSM_GROUP_CH)], axis=0)
    split = lambda t: (t.astype(BF16), (t - t.astype(BF16).astype(F32)).astype(BF16))
    (cr_h, cr_l), (ci_h, ci_l) = split(cb_r), split(-cb_i)
    (er_h, er_l), (ei_h, ei_l) = split(e_r), split(e_i)
    ktab = _dot(jnp.concatenate([cr_h, cr_h, cr_l, ci_h, ci_h, ci_l], axis=1),
                jnp.concatenate([er_h, er_l, er_h, ei_h, ei_l, ei_h], axis=0))
    kb, kf = ktab[:, :CHUNK], ktab[:, CHUNK:]
    kb1 = pltpu.roll(kb, 1, 1)
    kf1 = jnp.where(lax.broadcasted_iota(jnp.int32, (1, CHUNK), 1) == 0, kb1,
                    pltpu.roll(kf, 1, 1))
    bf16_bits = lambda t: lax.bitcast_convert_type(t.astype(BF16).astype(F32), jnp.int32)
    pack = lambda even, odd: bf16_bits(odd) | lax.shift_right_logical(bf16_bits(even), 16)
    wf_ref[g] = pack(kf, kf1)
    wb_ref[g] = pack(kb, kb1)

    for h in range(SSM_GROUP_CH):
        rows = slice(h * CHUNK, (h + 1) * CHUNK)
        in_ref[g, rows, 0:LANES] = (p_r * bb_r[h:h + 1] - p_i * bb_i[h:h + 1]).astype(BF16)
        in_ref[g, rows, LANES:] = (p_r * bb_i[h:h + 1] + p_i * bb_r[h:h + 1]).astype(BF16)

    for hp in range(SSM_GROUP_CH):
        cols = slice(hp * CHUNK, (hp + 1) * CHUNK)
        cc_r = ct_ref[g, :, hp:hp + 1]
        cc_i = ct_ref[g, :, SSM_GROUP_CH + hp:SSM_GROUP_CH + hp + 1]
        out_ref[g, 0:LANES, cols] = (cc_r * q_r - cc_i * q_i).astype(BF16)
        out_ref[g, LANES:, cols] = (-(cc_r * q_i + cc_i * q_r)).astype(BF16)
    return lc_r, lc_i


def _ssm_states(g, lam_chunk, u_ref, in_ref):
    lc_r, lc_i = lam_chunk
    lo_row = lax.broadcasted_iota(jnp.int32, (1, LANES), 1) < SSM_STATE
    row0 = g * SSM_GROUP_CH * BATCH
    a = jnp.concatenate(
        [u_ref[:, row0 + h * BATCH:row0 + (h + 1) * BATCH, :].reshape(ROWS, CHUNK)
         for h in range(SSM_GROUP_CH)], axis=1)
    ab = a.astype(BF16)
    e = _dot(ab, in_ref[g])
    s_r = jnp.zeros((BATCH, LANES), F32)
    s_i = jnp.zeros((BATCH, LANES), F32)
    hist_r, hist_i = [], []
    for i in range(N_CHUNKS):
        hist_r.append(s_r)
        hist_i.append(s_i)
        fr = slice(i * BATCH, (i + 1) * BATCH)
        br = slice((N_CHUNKS - 1 - i) * BATCH, (N_CHUNKS - i) * BATCH)
        in_r = jnp.where(lo_row, e[fr, 0:LANES], e[br, 0:LANES])
        in_i = jnp.where(lo_row, e[fr, LANES:], e[br, LANES:])
        s_r, s_i = (s_r * lc_r - s_i * lc_i + in_r, s_r * lc_i + s_i * lc_r + in_i)
    states = jnp.concatenate(
        [jnp.concatenate([jnp.where(lo_row, hist_r[c], hist_r[N_CHUNKS - 1 - c]),
                          jnp.where(lo_row, hist_i[c], hist_i[N_CHUNKS - 1 - c])], axis=1)
         for c in range(N_CHUNKS)], axis=0).astype(BF16)
    return a, ab, states


def _ssm_outputs(g, a, ab, states, drow_ref, y_ref, wf_ref, wb_ref, out_ref):
    row0 = g * SSM_GROUP_CH * BATCH

    pair = lax.broadcasted_iota(jnp.int32, (CHUNK // 2, CHUNK), 0)
    lane = lax.broadcasted_iota(jnp.int32, (CHUNK // 2, CHUNK), 1)
    upper = lane + 2 * pair < CHUNK
    per_slab = MXU_DEPTH // CHUNK

    def toeplitz_slab(slab):
        col_blocks = []
        for h in range(SSM_GROUP_CH):
            blocks = []
            for hp in range(slab * per_slab, (slab + 1) * per_slab):
                row = pl.ds(h * SSM_GROUP_CH + hp, 1)
                w = jnp.where(upper,
                              jnp.broadcast_to(wf_ref[g, row, :], (CHUNK // 2, CHUNK)),
                              jnp.broadcast_to(wb_ref[g, row, :], (CHUNK // 2, CHUNK)))
                w = pltpu.roll(w, 0, 1, stride=2, stride_axis=0)
                blocks.append(pltpu.bitcast(w, BF16))
            col_blocks.append(jnp.concatenate(blocks, axis=1))
        return jnp.concatenate(col_blocks, axis=0)

    for slab in range(SSM_GROUP_CH // per_slab):
        cols = slice(slab * MXU_DEPTH, (slab + 1) * MXU_DEPTH)
        y = (_dot(ab, toeplitz_slab(slab)) + _dot(states, out_ref[g, :, cols])
             + a[:, cols] * drow_ref[g][:, cols])
        for k in range(per_slab):
            hp = slab * per_slab + k
            y_ref[:, row0 + hp * BATCH:row0 + (hp + 1) * BATCH, :] = (
                y[:, k * CHUNK:(k + 1) * CHUNK].reshape(N_CHUNKS, BATCH, CHUNK))


def _ssm(u4, tables, drow):
    ng = SSM_STEP_GROUPS
    data = pl.BlockSpec((N_CHUNKS, ng * SSM_GROUP_CH * BATCH, CHUNK), lambda g: (0, g, 0))
    return pl.pallas_call(
        _ssm_kernel,
        grid=(SSM_GROUPS // ng,),
        in_specs=[data] + [_group_spec(s, ng) for s, _ in _TABLES] + [_group_spec((1, FLAT), ng)],
        out_specs=data,
        out_shape=jax.ShapeDtypeStruct((N_CHUNKS, SSM_WIDTH * BATCH, CHUNK), F32),
        compiler_params=pltpu.CompilerParams(vmem_limit_bytes=VMEM_LIMIT),
        name="ssm",
    )(u4, *tables, drow)


_ATTN_CASES = [(b, j) for b in range(BATCH) for j in range(N_KV_HEADS)]


def _attn_scores(q_ref, kp_ref, kc_ref, kn_ref):
    zeros = jnp.zeros((HEAD_DIM, GQA_GROUP * CHUNK), BF16)
    keys = []
    for b in range(BATCH):
        k_t = jnp.concatenate([kp_ref[b], kc_ref[b], kn_ref[b]], axis=1)
        keys.append(k_t.astype(F32).T.astype(BF16))
    scores = []
    for b, j in _ATTN_CASES:
        q_t = jnp.concatenate(
            [q_ref[b, (j * GQA_GROUP + g) * HEAD_DIM:(j * GQA_GROUP + g + 1) * HEAD_DIM, :]
             for g in range(GQA_GROUP)], axis=1)
        q_pad = jnp.concatenate([q_t, zeros] if j == 0 else [zeros, q_t], axis=0)
        scores.append(_dot(keys[b], q_pad))
    return scores


def _attn_finish(batches, c, scores, vp_ref, vc_ref, vn_ref, sink_ref):
    kk = lax.broadcasted_iota(jnp.int32, (CHUNK, CHUNK), 0)
    qq = lax.broadcasted_iota(jnp.int32, (CHUNK, CHUNK), 1)
    prev_ok = jnp.concatenate([kk >= qq] * GQA_GROUP, axis=1) & (c > 0)
    next_ok = jnp.concatenate([kk <= qq] * GQA_GROUP, axis=1) & (c < N_CHUNKS - 1)
    ones = jnp.ones((SUM_ROWS, 3 * CHUNK), BF16)
    heads = {}
    for b, j in [(b, j) for b in batches for j in range(N_KV_HEADS)]:
        s = scores[_ATTN_CASES.index((b, j))]
        v_t = jnp.concatenate([vp_ref[b], vc_ref[b], vn_ref[b]], axis=1)
        s = jnp.concatenate([jnp.where(prev_ok, s[:CHUNK], NEG_INF), s[CHUNK:2 * CHUNK],
                             jnp.where(next_ok, s[2 * CHUNK:], NEG_INF)], axis=0)
        sink = sink_ref[j] * LOG2E
        m = jnp.maximum(jnp.max(s, axis=0, keepdims=True), sink)
        p = jnp.exp2(s - m).astype(BF16)
        v_aug = jnp.concatenate([v_t[j * HEAD_DIM:(j + 1) * HEAD_DIM], ones], axis=0)
        o_aug = _dot(v_aug, p)
        o = o_aug[:HEAD_DIM] / (o_aug[HEAD_DIM:HEAD_DIM + 1] + jnp.exp2(sink - m))
        for g in range(GQA_GROUP):
            heads[b, j * GQA_GROUP + g] = o[:, g * CHUNK:(g + 1) * CHUNK].astype(BF16)
    return jnp.concatenate(
        [jnp.concatenate([heads[b, hd] for hd in range(N_HEADS)], axis=0) for b in batches],
        axis=1)


def _mix_kernel(q_ref, kp_ref, kc_ref, kn_ref, vp_ref, vc_ref, vn_ref, sink_ref,
                y_ref, ga_ref, gs_ref, x_ref, wglu_ref, wab_ref, wsb_ref, wout_ref, o_ref):
    scores = _attn_scores(q_ref, kp_ref, kc_ref, kn_ref)
    nb = BATCH // MERGE_SPLIT
    tiles = [range(t * nb, (t + 1) * nb) for t in range(MERGE_SPLIT)]
    cat = lambda ref, bs: jnp.concatenate([ref[b] for b in bs], axis=1)
    y_t = [jnp.concatenate([y_ref[0, pl.ds(b, SSM_WIDTH, stride=BATCH), :] for b in bs],
                           axis=1) for bs in tiles]
    yg = [0.5 * y * (1.0 + jnp.tanh(math.sqrt(2.0 / math.pi) * (y + 0.044715 * y ** 3)))
          for y in y_t]
    z_t = [_dot(wglu_ref[...], g.astype(BF16)) for g in yg]
    ssm_t = [g * _sigmoid(z) for g, z in zip(yg, z_t)]
    s_t = [_dot(wsb_ref[...], t.astype(BF16)) for t in ssm_t]
    attn = [_attn_finish(bs, pl.program_id(0), scores, vp_ref, vc_ref, vn_ref, sink_ref)
            for bs in tiles]
    a_t = [_dot(wab_ref[...], t) for t in attn]
    merged = [cat(ga_ref, bs).astype(F32) * a + cat(gs_ref, bs).astype(F32) * sb
              for bs, a, sb in zip(tiles, a_t, s_t)]
    for bs, m in zip(tiles, merged):
        mb = m.astype(BF16)
        for r in range(0, D_MODEL, MXU_DEPTH):
            o = _dot(wout_ref[r:r + MXU_DEPTH, :], mb).T
            o_ref[bs.start:bs.stop, :, r:r + MXU_DEPTH] = (
                x_ref[bs.start:bs.stop, :, r:r + MXU_DEPTH] + o.reshape(nb, CHUNK, MXU_DEPTH))


def _mix(q_t, k_t, v_t, sink_rows, y4, ga_t, gs_t, x, wglu_t, wab_t, wsb_t, wout_t):
    prev = lambda c: (0, 0, jnp.maximum(c - 1, 0))
    cur = lambda c: (0, 0, c)
    nxt = lambda c: (0, 0, jnp.minimum(c + 1, N_CHUNKS - 1))
    tspec = lambda rows, im=cur: pl.BlockSpec((BATCH, rows, CHUNK), im)
    wspec = lambda r, k: pl.BlockSpec((r, k), lambda c: (0, 0), pipeline_mode=pl.Buffered(1))
    xspec = pl.BlockSpec((BATCH, CHUNK, D_MODEL), lambda c: (0, c, 0))
    return pl.pallas_call(
        _mix_kernel,
        grid=(N_CHUNKS,),
        in_specs=[tspec(ATTN_WIDTH),
                  tspec(KV_WIDTH, prev), tspec(KV_WIDTH), tspec(KV_WIDTH, nxt),
                  tspec(KV_WIDTH, prev), tspec(KV_WIDTH), tspec(KV_WIDTH, nxt),
                  pl.BlockSpec((N_KV_HEADS, 1, GQA_GROUP * CHUNK), lambda c: (0, 0, 0)),
                  pl.BlockSpec((1, SSM_WIDTH * BATCH, CHUNK), lambda c: (c, 0, 0)),
                  tspec(D_MODEL), tspec(D_MODEL), xspec,
                  wspec(SSM_WIDTH, SSM_WIDTH), wspec(D_MODEL, ATTN_WIDTH),
                  wspec(D_MODEL, SSM_WIDTH), wspec(D_MODEL, D_MODEL)],
        out_specs=xspec,
        out_shape=jax.ShapeDtypeStruct((BATCH, SEQ, D_MODEL), F32),
        compiler_params=pltpu.CompilerParams(vmem_limit_bytes=VMEM_LIMIT),
        name="mix",
    )(q_t, k_t, k_t, k_t, v_t, v_t, v_t, sink_rows, y4, ga_t, gs_t, x,
      wglu_t, wab_t, wsb_t, wout_t)


def _ffn_kernel(x_ref, g2_ref, wg_ref, wu_ref, wd_ref, gf_ref, o_ref):
    tile = FFN_ROWS // FFN_SPLIT
    rows = [slice(t * tile, (t + 1) * tile) for t in range(FFN_SPLIT)]
    h = [_rmsnorm(x_ref[r, :], g2_ref[...]).astype(BF16) for r in rows]
    acc = [jnp.zeros((tile, D_MODEL), F32) for _ in rows]
    for j in range(D_FF // FFN_COLS):
        cols = slice(j * FFN_COLS, (j + 1) * FFN_COLS)
        for t in range(FFN_SPLIT):
            gate = _dot(h[t], wg_ref[:, cols])
            up = _dot(h[t], wu_ref[:, cols])
            act = (gate * _sigmoid(gate) * up).astype(BF16)
            acc[t] = acc[t] + _dot(act, wd_ref[cols, :])
    for t, r in enumerate(rows):
        o_ref[r, :] = _rmsnorm(x_ref[r, :] + acc[t], gf_ref[...])


def _ffn(x, g2, wg, wu, wd, gf):
    n_tok = BATCH * SEQ
    row = pl.BlockSpec((FFN_ROWS, D_MODEL), lambda i: (i, 0))
    vec = pl.BlockSpec((1, D_MODEL), lambda i: (0, 0))
    wspec = lambda r, k: pl.BlockSpec((r, k), lambda i: (0, 0), pipeline_mode=pl.Buffered(1))
    return pl.pallas_call(
        _ffn_kernel,
        grid=(n_tok // FFN_ROWS,),
        in_specs=[row, vec, wspec(D_MODEL, D_FF), wspec(D_MODEL, D_FF),
                  wspec(D_FF, D_MODEL), vec],
        out_specs=row,
        out_shape=jax.ShapeDtypeStruct((n_tok, D_MODEL), F32),
        compiler_params=pltpu.CompilerParams(vmem_limit_bytes=VMEM_LIMIT),
        name="ffn",
    )(x, g2, wg, wu, wd, gf)


def _ssm_param_layouts(lam_re, lam_im, log_dt, b_re, b_im, c_re, c_im, d):
    both = lambda t: jnp.concatenate([t[0], t[1]], axis=-1)
    ldt = jnp.broadcast_to(log_dt[:, :, None], (2, SSM_GROUPS, SSM_STATE))
    rows = jnp.stack([both(lam_re), both(lam_im), both(ldt)], axis=1)
    rowp = jnp.pad(rows, ((0, 0), (0, SUBLANES - 3), (0, 0)))
    bt =jnp.concatenate([both(jnp.swapaxes(b_re, 2, 3)),
                          both(jnp.swapaxes(b_im, 2, 3))], axis=1)
    cw = jnp.concatenate([jnp.tile(c_re, (1, 1, 2)), jnp.tile(c_im, (1, 1, 2))], axis=1)
    ct = jnp.swapaxes(cw, 1, 2)
    drow = jnp.repeat(d, CHUNK, axis=-1)[:, None, :]
    return rowp, bt, cw, ct, drow


def kernel(x, norm1_g, w_in, attn_sink, ssm_lambda_re, ssm_lambda_im, ssm_log_dt, ssm_b_re, ssm_b_im, ssm_c_re, ssm_c_im, ssm_d, w_glu, w_attn_branch, w_ssm_branch, w_out, norm2_g, w_ffn_gate, w_ffn_up, w_ffn_down, norm_f_g):
    pos = jnp.arange(SEQ, dtype=F32)
    inv_freq = ROPE_THETA ** (-jnp.arange(0, ROPE_DIM, 2, dtype=F32) / ROPE_DIM)
    ang = inv_freq[:, None] * pos[None, :]
    cos_t, sin_t = jnp.cos(ang), jnp.sin(ang)
    sink_rows = jnp.repeat(attn_sink[0], CHUNK).reshape(
        N_KV_HEADS, 1, GQA_GROUP * CHUNK)

    rowp, bt, cw, ct, drow = _ssm_param_layouts(
        ssm_lambda_re[0], ssm_lambda_im[0], ssm_log_dt[0], ssm_b_re[0], ssm_b_im[0],
        ssm_c_re[0], ssm_c_im[0], ssm_d[0])
    q_t, k_t, v_t, u4, ga_t, gs_t, *tables = _in_proj(
        x, norm1_g[0][None, :], w_in[0].T.astype(BF16), cos_t, sin_t, rowp, bt, cw, ct)
    y4 = _ssm(u4, tables, drow)
    x1 = _mix(q_t, k_t, v_t, sink_rows, y4, ga_t, gs_t, x,
              w_glu[0].T.astype(BF16), w_attn_branch[0].T.astype(BF16),
              w_ssm_branch[0].T.astype(BF16), w_out[0].T.astype(BF16))
    out = _ffn(x1.reshape(BATCH * SEQ, D_MODEL), norm2_g[0][None, :],
               w_ffn_gate[0].astype(BF16), w_ffn_up[0].astype(BF16),
               w_ffn_down[0].astype(BF16), norm_f_g[None, :])
    return out.reshape(BATCH, SEQ, D_MODEL)
```

```python
import math

import jax
import jax.numpy as jnp
from jax import lax
from jax.experimental import pallas as pl
from jax.experimental.pallas import tpu as pltpu

D_MODEL = 1024
BATCH = 8
SEQ = 4096
N_HEADS = 8
N_KV_HEADS = 2
HEAD_DIM = 64
GQA_GROUP = N_HEADS // N_KV_HEADS
ATTN_WIDTH = N_HEADS * HEAD_DIM
KV_WIDTH = N_KV_HEADS * HEAD_DIM
ROPE_DIM = HEAD_DIM // 4
ROPE_HALF = ROPE_DIM // 2
ROPE_THETA = 500000.0
WINDOW = 128
SSM_GROUP_CH = 16
SSM_WIDTH = D_MODEL // 2
SSM_GROUPS = SSM_WIDTH // SSM_GROUP_CH
SSM_STATE = 64
D_FF = 2816
IN_COLS = ATTN_WIDTH + 2 * KV_WIDTH + SSM_WIDTH + 2 * D_MODEL
RMS_EPS = 1e-6
NEG_INF = -1e30

LANES = 128
SUBLANES = 8
MXU_DEPTH = 256
CHUNK = LANES
LOG2_CHUNK = CHUNK.bit_length() - 1
N_CHUNKS = SEQ // CHUNK
ROWS = N_CHUNKS * BATCH
FLAT = SSM_GROUP_CH * CHUNK
SSM_STEP_GROUPS = 4
MERGE_SPLIT = 4
LOG2E = math.log2(math.e)
SUM_ROWS = 16
FFN_ROWS = 1024
FFN_SPLIT = 2
FFN_COLS = 256
VMEM_LIMIT = 58 * 1024 * 1024

assert BATCH == SUBLANES and WINDOW == CHUNK and D_FF % FFN_COLS == 0

_Q0 = 0
_K0 = _Q0 + ATTN_WIDTH
_V0 = _K0 + KV_WIDTH
_U0 = _V0 + KV_WIDTH
_GA0 = _U0 + SSM_WIDTH
_GS0 = _GA0 + D_MODEL
assert _GS0 + D_MODEL == IN_COLS

BF16 = jnp.bfloat16
F32 = jnp.float32


def _sigmoid(x):
    return 0.5 * jnp.tanh(0.5 * x) + 0.5


def _rmsnorm(x, g):
    return x * lax.rsqrt(jnp.mean(x * x, axis=-1, keepdims=True) + RMS_EPS) * g


def _dot(a, b):
    return jnp.dot(a, b, preferred_element_type=F32)


def _in_proj_kernel(x_ref, g_ref, w_ref, cos_ref, sin_ref, rowp_ref, bt_ref, cw_ref, ct_ref,
                    wg32_ref, wu32_ref, wd32_ref,
                    q_ref, k_ref, v_ref, u_ref, ga_ref, gs_ref,
                    wf_ref, wb_ref, in_ref, out_ref, lc_ref, wg_ref, wu_ref, wd_ref):
    for src, dst in ((wg32_ref, wg_ref), (wu32_ref, wu_ref), (wd32_ref, wd_ref)):
        dst[...] = src[...].astype(BF16)

    h = _rmsnorm(x_ref[...].reshape(BATCH * CHUNK, D_MODEL), g_ref[...]).astype(BF16)

    def proj_t(r0, r1):
        return lax.dot_general(w_ref[r0:r1, :], h, (((1,), (1,)), ((), ())),
                               preferred_element_type=F32)

    cos = jnp.concatenate([cos_ref[...]] * BATCH, axis=1)
    sin = jnp.concatenate([sin_ref[...]] * BATCH, axis=1)

    def rope_t(t, n_heads):
        parts = []
        for hd in range(n_heads):
            r1 = t[hd * HEAD_DIM:hd * HEAD_DIM + ROPE_HALF]
            r2 = t[hd * HEAD_DIM + ROPE_HALF:hd * HEAD_DIM + ROPE_DIM]
            parts += [r1 * cos - r2 * sin, r2 * cos + r1 * sin,
                      t[hd * HEAD_DIM + ROPE_DIM:(hd + 1) * HEAD_DIM]]
        return jnp.concatenate(parts, axis=0)

    def put(ref, r0, t):
        for b in range(BATCH):
            ref[b, r0:r0 + t.shape[0], :] = t[:, b * CHUNK:(b + 1) * CHUNK].astype(ref.dtype)

    slab = MXU_DEPTH
    for r in range(0, ATTN_WIDTH, slab):
        t = proj_t(_Q0 + r, _Q0 + r + slab) * (HEAD_DIM ** -0.5 * LOG2E)
        put(q_ref, r, rope_t(t, slab // HEAD_DIM))
    kv = proj_t(_K0, _U0)
    put(k_ref, 0, rope_t(kv[:KV_WIDTH], N_KV_HEADS))
    put(v_ref, 0, kv[KV_WIDTH:])
    for r in range(0, SSM_WIDTH, slab):
        u_t = proj_t(_U0 + r, _U0 + r + slab)
        for b in range(BATCH):
            u_ref[0, pl.ds(r * BATCH + b, slab, stride=BATCH), :] = (
                u_t[:, b * CHUNK:(b + 1) * CHUNK])
    for r in range(0, D_MODEL, slab):
        put(ga_ref, r, _sigmoid(proj_t(_GA0 + r, _GA0 + r + slab)))

    lc_r, lc_i = _ssm_tables(0, rowp_ref, bt_ref, cw_ref, ct_ref, wf_ref, wb_ref, in_ref, out_ref)
    lc_ref[0] = jnp.concatenate([lc_r, lc_i, jnp.zeros((SUBLANES - 2, LANES), F32)], axis=0)

    for r in range(0, D_MODEL, slab):
        put(gs_ref, r, _sigmoid(proj_t(_GS0 + r, _GS0 + r + slab)))


_PAIRS = SSM_GROUP_CH * SSM_GROUP_CH
_TABLES = [((_PAIRS, CHUNK), jnp.int32), ((_PAIRS, CHUNK), jnp.int32),
           ((FLAT, 2 * LANES), BF16), ((2 * LANES, FLAT), BF16), ((SUBLANES, LANES), F32)]
_PARAMS = [(SUBLANES, LANES), (2 * SSM_GROUP_CH, LANES), (2 * SSM_GROUP_CH, LANES),
           (LANES, 2 * SSM_GROUP_CH)]


def _group_spec(shape, n=1):
    return pl.BlockSpec((n,) + shape, lambda g: (g, 0, 0))


def _row_blocks(shape, rows):
    n = shape[0] // rows
    assert shape[0] % rows == 0 and n <= N_CHUNKS and rows % (2 * SUBLANES) == 0
    return pl.BlockSpec((rows, shape[1]), lambda c: (jnp.minimum(c, n - 1), 0))


def _in_proj(x, g1, w_t, cos_t, sin_t, rowp, bt, cw, ct, wg, wu, wd):
    assert N_CHUNKS == SSM_GROUPS
    tok = lambda rows, dt: jax.ShapeDtypeStruct((BATCH, rows, SEQ), dt)
    tspec = lambda rows: pl.BlockSpec((BATCH, rows, CHUNK), lambda c: (0, 0, c))
    casts = [_row_blocks(wg.shape, D_MODEL // N_CHUNKS), _row_blocks(wu.shape, D_MODEL // N_CHUNKS),
             _row_blocks(wd.shape, CHUNK)]
    return pl.pallas_call(
        _in_proj_kernel,
        grid=(N_CHUNKS,),
        in_specs=[
            pl.BlockSpec((BATCH, CHUNK, D_MODEL), lambda c: (0, c, 0)),
            pl.BlockSpec((1, D_MODEL), lambda c: (0, 0)),
            pl.BlockSpec((IN_COLS, D_MODEL), lambda c: (0, 0), pipeline_mode=pl.Buffered(1)),
            pl.BlockSpec((ROPE_HALF, CHUNK), lambda c: (0, c)),
            pl.BlockSpec((ROPE_HALF, CHUNK), lambda c: (0, c)),
        ] + [_group_spec(s) for s in _PARAMS] + casts,
        out_specs=[
            tspec(ATTN_WIDTH), tspec(KV_WIDTH), tspec(KV_WIDTH),
            pl.BlockSpec((1, SSM_WIDTH * BATCH, CHUNK), lambda c: (c, 0, 0)),
            tspec(D_MODEL), tspec(D_MODEL),
        ] + [_group_spec(s) for s, _ in _TABLES] + casts,
        out_shape=[
            tok(ATTN_WIDTH, BF16), tok(KV_WIDTH, BF16), tok(KV_WIDTH, BF16),
            jax.ShapeDtypeStruct((N_CHUNKS, SSM_WIDTH * BATCH, CHUNK), F32),
            tok(D_MODEL, BF16), tok(D_MODEL, BF16),
        ] + [jax.ShapeDtypeStruct((SSM_GROUPS,) + s, dt) for s, dt in _TABLES]
        + [jax.ShapeDtypeStruct(w.shape, BF16) for w in (wg, wu, wd)],
        compiler_params=pltpu.CompilerParams(vmem_limit_bytes=VMEM_LIMIT),
        name="in_proj",
    )(x, g1, w_t, cos_t, sin_t, rowp, bt, cw, ct, wg, wu, wd)


def _cmul(a_r, a_i, b_r, b_i):
    return a_r * b_r - a_i * b_i, a_r * b_i + a_i * b_r


def _ssm_kernel(u_ref, wf_ref, wb_ref, in_ref, out_ref, lc_ref, drow_ref, y_ref):
    groups = range(SSM_STEP_GROUPS)
    data = [_ssm_states(g, (lc_ref[g, 0:1, :], lc_ref[g, 1:2, :]), u_ref, in_ref) for g in groups]
    for g in groups:
        _ssm_outputs(g, *data[g], drow_ref, y_ref, wf_ref, wb_ref, out_ref)


def _ssm_tables(g, rowp_ref, bt_ref, cw_ref, ct_ref, wf_ref, wb_ref, in_ref, out_ref):
    half = SSM_STATE
    top_col = lax.broadcasted_iota(jnp.int32, (LANES, 1), 0) < half

    lam_r, lam_i = rowp_ref[g, 0:1, :], rowp_ref[g, 1:2, :]
    dt = jnp.exp(rowp_ref[g, 2:3, :])
    z_r, z_i = lam_r * dt, lam_i * dt
    e1 = jnp.exp(z_r)
    lb_r, lb_i = e1 * jnp.cos(z_i), e1 * jnp.sin(z_i)
    den = lam_r * lam_r + lam_i * lam_i
    cf_r = ((lb_r - 1.0) * lam_r + lb_i * lam_i) / den
    cf_i = (lb_i * lam_r - (lb_r - 1.0) * lam_i) / den
    b_r, b_i = bt_ref[g, 0:SSM_GROUP_CH, :], bt_ref[g, SSM_GROUP_CH:, :]
    bb_r = cf_r * b_r - cf_i * b_i
    bb_i = cf_r * b_i + cf_i * b_r
    c_r, c_i = cw_ref[g, 0:SSM_GROUP_CH, :], cw_ref[g, SSM_GROUP_CH:, :]

    sq_r, sq_i = [lb_r], [lb_i]
    for _ in range(LOG2_CHUNK):
        r, i = _cmul(sq_r[-1], sq_i[-1], sq_r[-1], sq_i[-1])
        sq_r.append(r)
        sq_i.append(i)
    lc_r, lc_i = sq_r[LOG2_CHUNK], sq_i[LOG2_CHUNK]
    pad = jnp.zeros((LANES - 2 * SUBLANES, LANES), F32)
    cols = jnp.concatenate(sq_r[:SUBLANES] + sq_i[:SUBLANES] + [pad], axis=0).T
    sq_rc = [cols[:, k:k + 1] for k in range(LOG2_CHUNK)]
    sq_ic = [cols[:, SUBLANES + k:SUBLANES + k + 1] for k in range(LOG2_CHUNK)]

    j_idx = lax.broadcasted_iota(jnp.int32, (LANES, CHUNK), 1)
    one, zero = jnp.ones((LANES, CHUNK), F32), jnp.zeros((LANES, CHUNK), F32)
    asc_r, asc_i, dsc_r, dsc_i = one, zero, one, zero
    for k in range(LOG2_CHUNK):
        bit = (j_idx & (1 << k)) != 0
        n_r, n_i = _cmul(asc_r, asc_i, sq_rc[k], sq_ic[k])
        asc_r, asc_i = jnp.where(bit, n_r, asc_r), jnp.where(bit, n_i, asc_i)
        n_r, n_i = _cmul(dsc_r, dsc_i, sq_rc[k], sq_ic[k])
        dsc_r, dsc_i = jnp.where(bit, dsc_r, n_r), jnp.where(bit, dsc_i, n_i)
    q_r, q_i = _cmul(jnp.where(top_col, asc_r, dsc_r), jnp.where(top_col, asc_i, dsc_i),
                     sq_rc[0], sq_ic[0])
    p_r = jnp.where(top_col, dsc_r, asc_r).T
    p_i = jnp.where(top_col, dsc_i, asc_i).T

    first = j_idx == 0
    ef_r = jnp.where(top_col, asc_r, jnp.where(first, 1.0, 0.0))
    ef_i = jnp.where(top_col, asc_i, 0.0)
    eb_r = jnp.where(top_col, 0.0, jnp.where(first, 0.0, q_r))
    eb_i = jnp.where(top_col, 0.0, jnp.where(first, 0.0, q_i))
    e_r = jnp.concatenate([eb_r, ef_r], axis=1)
    e_i = jnp.concatenate([eb_i, ef_i], axis=1)
    cb_r = jnp.concatenate([c_r * bb_r[h:h + 1] - c_i * bb_i[h:h + 1]
                            for h in range(SSM_GROUP_CH)], axis=0)
    cb_i = jnp.concatenate([c_r * bb_i[h:h + 1] + c_i * bb_r[h:h + 1]
                            for h in range(SSM_GROUP_CH)], axis=0)
    split = lambda t: (t.astype(BF16), (t - t.astype(BF16).astype(F32)).astype(BF16))
    (cr_h, cr_l), (ci_h, ci_l) = split(cb_r), split(-cb_i)
    (er_h, er_l), (ei_h, ei_l) = split(e_r), split(e_i)
    ktab = _dot(jnp.concatenate([cr_h, cr_h, cr_l, ci_h, ci_h, ci_l], axis=1),
                jnp.concatenate([er_h, er_l, er_h, ei_h, ei_l, ei_h], axis=0))
    kb, kf = ktab[:, :CHUNK], ktab[:, CHUNK:]
    kb1 = pltpu.roll(kb, 1, 1)
    kf1 = jnp.where(lax.broadcasted_iota(jnp.int32, (1, CHUNK), 1) == 0, kb1,
                    pltpu.roll(kf, 1, 1))
    bf16_bits = lambda t: lax.bitcast_convert_type(t.astype(BF16).astype(F32), jnp.int32)
    pack = lambda even, odd: bf16_bits(odd) | lax.shift_right_logical(bf16_bits(even), 16)
    wf_ref[g] = pack(kf, kf1)
    wb_ref[g] = pack(kb, kb1)

    for h in range(SSM_GROUP_CH):
        rows = slice(h * CHUNK, (h + 1) * CHUNK)
        in_ref[g, rows, 0:LANES] = (p_r * bb_r[h:h + 1] - p_i * bb_i[h:h + 1]).astype(BF16)
        in_ref[g, rows, LANES:] = (p_r * bb_i[h:h + 1] + p_i * bb_r[h:h + 1]).astype(BF16)

    for hp in range(SSM_GROUP_CH):
        cols = slice(hp * CHUNK, (hp + 1) * CHUNK)
        cc_r = ct_ref[g, :, hp:hp + 1]
        cc_i = ct_ref[g, :, SSM_GROUP_CH + hp:SSM_GROUP_CH + hp + 1]
        out_ref[g, 0:LANES, cols] = (cc_r * q_r - cc_i * q_i).astype(BF16)
        out_ref[g, LANES:, cols] = (-(cc_r * q_i + cc_i * q_r)).astype(BF16)
    return lc_r, lc_i


def _ssm_states(g, lam_chunk, u_ref, in_ref):
    lc_r, lc_i = lam_chunk
    lo_row = lax.broadcasted_iota(jnp.int32, (1, LANES), 1) < SSM_STATE
    row0 = g * SSM_GROUP_CH * BATCH
    a = jnp.concatenate(
        [u_ref[:, row0 + h * BATCH:row0 + (h + 1) * BATCH, :].reshape(ROWS, CHUNK)
         for h in range(SSM_GROUP_CH)], axis=1)
    ab = a.astype(BF16)
    e = _dot(ab, in_ref[g])
    s_r = jnp.zeros((BATCH, LANES), F32)
    s_i = jnp.zeros((BATCH, LANES), F32)
    hist_r, hist_i = [], []
    for i in range(N_CHUNKS):
        hist_r.append(s_r)
        hist_i.append(s_i)
        fr = slice(i * BATCH, (i + 1) * BATCH)
        br = slice((N_CHUNKS - 1 - i) * BATCH, (N_CHUNKS - i) * BATCH)
        in_r = jnp.where(lo_row, e[fr, 0:LANES], e[br, 0:LANES])
        in_i = jnp.where(lo_row, e[fr, LANES:], e[br, LANES:])
        s_r, s_i = (s_r * lc_r - s_i * lc_i + in_r, s_r * lc_i + s_i * lc_r + in_i)
    states = jnp.concatenate(
        [jnp.concatenate([jnp.where(lo_row, hist_r[c], hist_r[N_CHUNKS - 1 - c]),
                          jnp.where(lo_row, hist_i[c], hist_i[N_CHUNKS - 1 - c])], axis=1)
         for c in range(N_CHUNKS)], axis=0).astype(BF16)
    return a, ab, states


def _ssm_outputs(g, a, ab, states, drow_ref, y_ref, wf_ref, wb_ref, out_ref):
    row0 = g * SSM_GROUP_CH * BATCH

    pair = lax.broadcasted_iota(jnp.int32, (CHUNK // 2, CHUNK), 0)
    lane = lax.broadcasted_iota(jnp.int32, (CHUNK // 2, CHUNK), 1)
    upper = lane + 2 * pair < CHUNK
    per_slab = MXU_DEPTH // CHUNK

    def toeplitz_slab(slab):
        col_blocks = []
        for h in range(SSM_GROUP_CH):
            blocks = []
            for hp in range(slab * per_slab, (slab + 1) * per_slab):
                row = pl.ds(h * SSM_GROUP_CH + hp, 1)
                w = jnp.where(upper,
                              jnp.broadcast_to(wf_ref[g, row, :], (CHUNK // 2, CHUNK)),
                              jnp.broadcast_to(wb_ref[g, row, :], (CHUNK // 2, CHUNK)))
                w = pltpu.roll(w, 0, 1, stride=2, stride_axis=0)
                blocks.append(pltpu.bitcast(w, BF16))
            col_blocks.append(jnp.concatenate(blocks, axis=1))
        return jnp.concatenate(col_blocks, axis=0)

    for slab in range(SSM_GROUP_CH // per_slab):
        cols = slice(slab * MXU_DEPTH, (slab + 1) * MXU_DEPTH)
        y = (_dot(ab, toeplitz_slab(slab)) + _dot(states, out_ref[g, :, cols])
             + a[:, cols] * drow_ref[g][:, cols])
        for k in range(per_slab):
            hp = slab * per_slab + k
            y_ref[:, row0 + hp * BATCH:row0 + (hp + 1) * BATCH, :] = (
                y[:, k * CHUNK:(k + 1) * CHUNK].reshape(N_CHUNKS, BATCH, CHUNK))


def _ssm(u4, tables, drow):
    ng = SSM_STEP_GROUPS
    data = pl.BlockSpec((N_CHUNKS, ng * SSM_GROUP_CH * BATCH, CHUNK), lambda g: (0, g, 0))
    return pl.pallas_call(
        _ssm_kernel,
        grid=(SSM_GROUPS // ng,),
        in_specs=[data] + [_group_spec(s, ng) for s, _ in _TABLES] + [_group_spec((1, FLAT), ng)],
        out_specs=data,
        out_shape=jax.ShapeDtypeStruct((N_CHUNKS, SSM_WIDTH * BATCH, CHUNK), F32),
        compiler_params=pltpu.CompilerParams(vmem_limit_bytes=VMEM_LIMIT),
        name="ssm",
    )(u4, *tables, drow)


_ATTN_CASES = [(b, j) for b in range(BATCH) for j in range(N_KV_HEADS)]


def _attn_scores(q_ref, kp_ref, kc_ref, kn_ref):
    zeros = jnp.zeros((HEAD_DIM, GQA_GROUP * CHUNK), BF16)
    keys = []
    for b in range(BATCH):
        k_t = jnp.concatenate([kp_ref[b], kc_ref[b], kn_ref[b]], axis=1)
        keys.append(k_t.astype(F32).T.astype(BF16))
    scores = []
    for b, j in _ATTN_CASES:
        q_t = jnp.concatenate(
            [q_ref[b, (j * GQA_GROUP + g) * HEAD_DIM:(j * GQA_GROUP + g + 1) * HEAD_DIM, :]
             for g in range(GQA_GROUP)], axis=1)
        q_pad = jnp.concatenate([q_t, zeros] if j == 0 else [zeros, q_t], axis=0)
        scores.append(_dot(keys[b], q_pad))
    return scores


def _attn_finish(batches, c, scores, vp_ref, vc_ref, vn_ref, sink_ref):
    kk = lax.broadcasted_iota(jnp.int32, (CHUNK, CHUNK), 0)
    qq = lax.broadcasted_iota(jnp.int32, (CHUNK, CHUNK), 1)
    prev_ok = jnp.concatenate([kk >= qq] * GQA_GROUP, axis=1) & (c > 0)
    next_ok = jnp.concatenate([kk <= qq] * GQA_GROUP, axis=1) & (c < N_CHUNKS - 1)
    ones = jnp.ones((SUM_ROWS, 3 * CHUNK), BF16)
    heads = {}
    for b, j in [(b, j) for b in batches for j in range(N_KV_HEADS)]:
        s = scores[_ATTN_CASES.index((b, j))]
        v_t = jnp.concatenate([vp_ref[b], vc_ref[b], vn_ref[b]], axis=1)
        s = jnp.concatenate([jnp.where(prev_ok, s[:CHUNK], NEG_INF), s[CHUNK:2 * CHUNK],
                             jnp.where(next_ok, s[2 * CHUNK:], NEG_INF)], axis=0)
        sink = sink_ref[j] * LOG2E
        m = jnp.maximum(jnp.max(s, axis=0, keepdims=True), sink)
        p = jnp.exp2(s - m).astype(BF16)
        v_aug = jnp.concatenate([v_t[j * HEAD_DIM:(j + 1) * HEAD_DIM], ones], axis=0)
        o_aug = _dot(v_aug, p)
        o = o_aug[:HEAD_DIM] / (o_aug[HEAD_DIM:HEAD_DIM + 1] + jnp.exp2(sink - m))
        for g in range(GQA_GROUP):
            heads[b, j * GQA_GROUP + g] = o[:, g * CHUNK:(g + 1) * CHUNK].astype(BF16)
    return jnp.concatenate(
        [jnp.concatenate([heads[b, hd] for hd in range(N_HEADS)], axis=0) for b in batches],
        axis=1)


def _mix_kernel(q_ref, kp_ref, kc_ref, kn_ref, vp_ref, vc_ref, vn_ref, sink_ref,
                y_ref, ga_ref, gs_ref, wglu_ref, wab_ref, wsb_ref, wout_ref, o_ref):
    scores = _attn_scores(q_ref, kp_ref, kc_ref, kn_ref)
    nb = BATCH // MERGE_SPLIT
    tiles = [range(t * nb, (t + 1) * nb) for t in range(MERGE_SPLIT)]
    cat = lambda ref, bs: jnp.concatenate([ref[b] for b in bs], axis=1)
    y_t = [jnp.concatenate([y_ref[0, pl.ds(b, SSM_WIDTH, stride=BATCH), :] for b in bs],
                           axis=1) for bs in tiles]
    yg = [0.5 * y * (1.0 + jnp.tanh(math.sqrt(2.0 / math.pi) * (y + 0.044715 * y ** 3)))
          for y in y_t]
    z_t = [_dot(wglu_ref[...], g.astype(BF16)) for g in yg]
    ssm_t = [g * _sigmoid(z) for g, z in zip(yg, z_t)]
    s_t = [_dot(wsb_ref[...], t.astype(BF16)) for t in ssm_t]
    attn = [_attn_finish(bs, pl.program_id(0), scores, vp_ref, vc_ref, vn_ref, sink_ref)
            for bs in tiles]
    a_t = [_dot(wab_ref[...], t) for t in attn]
    merged = [cat(ga_ref, bs).astype(F32) * a + cat(gs_ref, bs).astype(F32) * sb
              for bs, a, sb in zip(tiles, a_t, s_t)]
    for bs, m in zip(tiles, merged):
        mb = m.astype(BF16)
        for r in range(0, D_MODEL, MXU_DEPTH):
            o = _dot(wout_ref[r:r + MXU_DEPTH, :], mb).T
            o_ref[bs.start:bs.stop, :, r:r + MXU_DEPTH] = o.reshape(nb, CHUNK, MXU_DEPTH)


def _mix(q_t, k_t, v_t, sink_rows, y4, ga_t, gs_t, wglu_t, wab_t, wsb_t, wout_t):
    prev = lambda c: (0, 0, jnp.maximum(c - 1, 0))
    cur = lambda c: (0, 0, c)
    nxt = lambda c: (0, 0, jnp.minimum(c + 1, N_CHUNKS - 1))
    tspec = lambda rows, im=cur: pl.BlockSpec((BATCH, rows, CHUNK), im)
    wspec = lambda r, k: pl.BlockSpec((r, k), lambda c: (0, 0), pipeline_mode=pl.Buffered(1))
    xspec = pl.BlockSpec((BATCH, CHUNK, D_MODEL), lambda c: (0, c, 0))
    return pl.pallas_call(
        _mix_kernel,
        grid=(N_CHUNKS,),
        in_specs=[tspec(ATTN_WIDTH),
                  tspec(KV_WIDTH, prev), tspec(KV_WIDTH), tspec(KV_WIDTH, nxt),
                  tspec(KV_WIDTH, prev), tspec(KV_WIDTH), tspec(KV_WIDTH, nxt),
                  pl.BlockSpec((N_KV_HEADS, 1, GQA_GROUP * CHUNK), lambda c: (0, 0, 0)),
                  pl.BlockSpec((1, SSM_WIDTH * BATCH, CHUNK), lambda c: (c, 0, 0)),
                  tspec(D_MODEL), tspec(D_MODEL),
                  wspec(SSM_WIDTH, SSM_WIDTH), wspec(D_MODEL, ATTN_WIDTH),
                  wspec(D_MODEL, SSM_WIDTH), wspec(D_MODEL, D_MODEL)],
        out_specs=xspec,
        out_shape=jax.ShapeDtypeStruct((BATCH, SEQ, D_MODEL), F32),
        compiler_params=pltpu.CompilerParams(vmem_limit_bytes=VMEM_LIMIT),
        name="mix",
    )(q_t, k_t, k_t, k_t, v_t, v_t, v_t, sink_rows, y4, ga_t, gs_t,
      wglu_t, wab_t, wsb_t, wout_t)


def _ffn_kernel(x_ref, d_ref, g2_ref, wg_ref, wu_ref, wd_ref, gf_ref, o_ref):
    tile = FFN_ROWS // FFN_SPLIT
    rows = [slice(t * tile, (t + 1) * tile) for t in range(FFN_SPLIT)]
    h = [_rmsnorm(x_ref[r, :] + d_ref[r, :], g2_ref[...]).astype(BF16) for r in rows]
    acc = [jnp.zeros((tile, D_MODEL), F32) for _ in rows]
    for j in range(D_FF // FFN_COLS):
        cols = slice(j * FFN_COLS, (j + 1) * FFN_COLS)
        for t in range(FFN_SPLIT):
            gate = _dot(h[t], wg_ref[:, cols])
            up = _dot(h[t], wu_ref[:, cols])
            act = (gate * _sigmoid(gate) * up).astype(BF16)
            acc[t] = acc[t] + _dot(act, wd_ref[cols, :])
    for t, r in enumerate(rows):
        o_ref[r, :] = _rmsnorm(x_ref[r, :] + d_ref[r, :] + acc[t], gf_ref[...])


def _ffn(x, delta, g2, wg, wu, wd, gf):
    n_tok = BATCH * SEQ
    row = pl.BlockSpec((FFN_ROWS, D_MODEL), lambda i: (i, 0))
    vec = pl.BlockSpec((1, D_MODEL), lambda i: (0, 0))
    wspec = lambda r, k: pl.BlockSpec((r, k), lambda i: (0, 0), pipeline_mode=pl.Buffered(1))
    return pl.pallas_call(
        _ffn_kernel,
        grid=(n_tok // FFN_ROWS,),
        in_specs=[row, row, vec, wspec(D_MODEL, D_FF), wspec(D_MODEL, D_FF),
                  wspec(D_FF, D_MODEL), vec],
        out_specs=row,
        out_shape=jax.ShapeDtypeStruct((n_tok, D_MODEL), F32),
        compiler_params=pltpu.CompilerParams(vmem_limit_bytes=VMEM_LIMIT),
        name="ffn",
    )(x, delta, g2, wg, wu, wd, gf)


def _ssm_param_layouts(lam_re, lam_im, log_dt, b_re, b_im, c_re, c_im, d):
    both = lambda t: jnp.concatenate([t[0], t[1]], axis=-1)
    ldt = jnp.broadcast_to(log_dt[:, :, None], (2, SSM_GROUPS, SSM_STATE))
    rows = jnp.stack([both(lam_re), both(lam_im), both(ldt)], axis=1)
    rowp = jnp.pad(rows, ((0, 0), (0, SUBLANES - 3), (0, 0)))
    bt =jnp.concatenate([both(jnp.swapaxes(b_re, 2, 3)),
                          both(jnp.swapaxes(b_im, 2, 3))], axis=1)
    cw = jnp.concatenate([jnp.tile(c_re, (1, 1, 2)), jnp.tile(c_im, (1, 1, 2))], axis=1)
    ct = jnp.swapaxes(cw, 1, 2)
    drow = jnp.repeat(d, CHUNK, axis=-1)[:, None, :]
    return rowp, bt, cw, ct, drow


def kernel(x, norm1_g, w_in, attn_sink, ssm_lambda_re, ssm_lambda_im, ssm_log_dt, ssm_b_re, ssm_b_im, ssm_c_re, ssm_c_im, ssm_d, w_glu, w_attn_branch, w_ssm_branch, w_out, norm2_g, w_ffn_gate, w_ffn_up, w_ffn_down, norm_f_g):
    pos = jnp.arange(SEQ, dtype=F32)
    inv_freq = ROPE_THETA ** (-jnp.arange(0, ROPE_DIM, 2, dtype=F32) / ROPE_DIM)
    ang = inv_freq[:, None] * pos[None, :]
    cos_t, sin_t = jnp.cos(ang), jnp.sin(ang)
    sink_rows = jnp.repeat(attn_sink[0], CHUNK).reshape(
        N_KV_HEADS, 1, GQA_GROUP * CHUNK)

    rowp, bt, cw, ct, drow = _ssm_param_layouts(
        ssm_lambda_re[0], ssm_lambda_im[0], ssm_log_dt[0], ssm_b_re[0], ssm_b_im[0],
        ssm_c_re[0], ssm_c_im[0], ssm_d[0])
    q_t, k_t, v_t, u4, ga_t, gs_t, *rest = _in_proj(
        x, norm1_g[0][None, :], w_in[0].T.astype(BF16), cos_t, sin_t, rowp, bt, cw, ct,
        w_ffn_gate[0], w_ffn_up[0], w_ffn_down[0])
    tables, (wg, wu, wd) = rest[:len(_TABLES)], rest[len(_TABLES):]
    y4 = _ssm(u4, tables, drow)
    delta = _mix(q_t, k_t, v_t, sink_rows, y4, ga_t, gs_t,
                 w_glu[0].T.astype(BF16), w_attn_branch[0].T.astype(BF16),
                 w_ssm_branch[0].T.astype(BF16), w_out[0].T.astype(BF16))
    flat = lambda t: t.reshape(BATCH * SEQ, D_MODEL)
    out = _ffn(flat(x), flat(delta), norm2_g[0][None, :], wg, wu, wd, norm_f_g[None, :])
    return out.reshape(BATCH, SEQ, D_MODEL)
```

```python
import math

import jax
import jax.numpy as jnp
from jax import lax
from jax.experimental import pallas as pl
from jax.experimental.pallas import tpu as pltpu

D_MODEL = 1024
BATCH = 8
SEQ = 4096
N_HEADS = 8
N_KV_HEADS = 2
HEAD_DIM = 64
GQA_GROUP = N_HEADS // N_KV_HEADS
ATTN_WIDTH = N_HEADS * HEAD_DIM
KV_WIDTH = N_KV_HEADS * HEAD_DIM
ROPE_DIM = HEAD_DIM // 4
ROPE_HALF = ROPE_DIM // 2
ROPE_THETA = 500000.0
WINDOW = 128
SSM_GROUP_CH = 16
SSM_WIDTH = D_MODEL // 2
SSM_GROUPS = SSM_WIDTH // SSM_GROUP_CH
SSM_STATE = 64
D_FF = 2816
IN_COLS = ATTN_WIDTH + 2 * KV_WIDTH + SSM_WIDTH + 2 * D_MODEL
RMS_EPS = 1e-6
NEG_INF = -1e30

LANES = 128
SUBLANES = 8
MXU_DEPTH = 256
CHUNK = LANES
LOG2_CHUNK = CHUNK.bit_length() - 1
N_CHUNKS = SEQ // CHUNK
ROWS = N_CHUNKS * BATCH
FLAT = SSM_GROUP_CH * CHUNK
SSM_STEP_GROUPS = 4
MERGE_SPLIT = 4
LOG2E = math.log2(math.e)
SUM_ROWS = 16
FFN_ROWS = 1024
FFN_SPLIT = 2
FFN_COLS = 256
VMEM_LIMIT = 58 * 1024 * 1024

assert BATCH == SUBLANES and WINDOW == CHUNK and D_FF % FFN_COLS == 0

_Q0 = 0
_K0 = _Q0 + ATTN_WIDTH
_V0 = _K0 + KV_WIDTH
_U0 = _V0 + KV_WIDTH
_GA0 = _U0 + SSM_WIDTH
_GS0 = _GA0 + D_MODEL
assert _GS0 + D_MODEL == IN_COLS

BF16 = jnp.bfloat16
F32 = jnp.float32


def _sigmoid(x):
    return 0.5 * jnp.tanh(0.5 * x) + 0.5


def _rmsnorm(x, g):
    return x * lax.rsqrt(jnp.mean(x * x, axis=-1, keepdims=True) + RMS_EPS) * g


def _dot(a, b):
    return jnp.dot(a, b, preferred_element_type=F32)


def _in_proj_kernel(x_ref, g_ref, w_ref, cos_ref, sin_ref, rowp_ref, bt_ref, cw_ref, ct_ref,
                    wg32_ref, wu32_ref, wd32_ref, wglu32_ref, wab32_ref, wsb32_ref, wout32_ref,
                    q_ref, k_ref, v_ref, u_ref, ga_ref, gs_ref,
                    wf_ref, wb_ref, in_ref, out_ref, lc_ref, wg_ref, wu_ref, wd_ref,
                    wglu_ref, wab_ref, wsb_ref, wout_ref):
    for src, dst in ((wg32_ref, wg_ref), (wu32_ref, wu_ref), (wd32_ref, wd_ref)):
        dst[...] = src[...].astype(BF16)
    pairs = ((wglu32_ref, wglu_ref), (wab32_ref, wab_ref), (wsb32_ref, wsb_ref),
             (wout32_ref, wout_ref))
    for (src, dst), cols in zip(pairs, _MIX_WEIGHT_COLS):
        @pl.when(pl.program_id(0) < cols // LANES)
        def _(src=src, dst=dst):
            dst[...] = src[...].T.astype(BF16)

    h = _rmsnorm(x_ref[...].reshape(BATCH * CHUNK, D_MODEL), g_ref[...]).astype(BF16)

    def proj_t(r0, r1):
        return lax.dot_general(w_ref[r0:r1, :], h, (((1,), (1,)), ((), ())),
                               preferred_element_type=F32)

    cos = jnp.concatenate([cos_ref[...]] * BATCH, axis=1)
    sin = jnp.concatenate([sin_ref[...]] * BATCH, axis=1)

    def rope_t(t, n_heads):
        parts = []
        for hd in range(n_heads):
            r1 = t[hd * HEAD_DIM:hd * HEAD_DIM + ROPE_HALF]
            r2 = t[hd * HEAD_DIM + ROPE_HALF:hd * HEAD_DIM + ROPE_DIM]
            parts += [r1 * cos - r2 * sin, r2 * cos + r1 * sin,
                      t[hd * HEAD_DIM + ROPE_DIM:(hd + 1) * HEAD_DIM]]
        return jnp.concatenate(parts, axis=0)

    def put(ref, r0, t):
        for b in range(BATCH):
            ref[b, r0:r0 + t.shape[0], :] = t[:, b * CHUNK:(b + 1) * CHUNK].astype(ref.dtype)

    slab = MXU_DEPTH
    for r in range(0, ATTN_WIDTH, slab):
        t = proj_t(_Q0 + r, _Q0 + r + slab) * (HEAD_DIM ** -0.5 * LOG2E)
        put(q_ref, r, rope_t(t, slab // HEAD_DIM))
    kv = proj_t(_K0, _U0)
    put(k_ref, 0, rope_t(kv[:KV_WIDTH], N_KV_HEADS))
    put(v_ref, 0, kv[KV_WIDTH:])
    for r in range(0, SSM_WIDTH, slab):
        u_t = proj_t(_U0 + r, _U0 + r + slab)
        for b in range(BATCH):
            u_ref[0, pl.ds(r * BATCH + b, slab, stride=BATCH), :] = (
                u_t[:, b * CHUNK:(b + 1) * CHUNK])
    for r in range(0, D_MODEL, slab):
        put(ga_ref, r, _sigmoid(proj_t(_GA0 + r, _GA0 + r + slab)))

    lc_r, lc_i = _ssm_tables(0, rowp_ref, bt_ref, cw_ref, ct_ref, wf_ref, wb_ref, in_ref, out_ref)
    lc_ref[0] = jnp.concatenate([lc_r, lc_i, jnp.zeros((SUBLANES - 2, LANES), F32)], axis=0)

    for r in range(0, D_MODEL, slab):
        put(gs_ref, r, _sigmoid(proj_t(_GS0 + r, _GS0 + r + slab)))


_PAIRS = SSM_GROUP_CH * SSM_GROUP_CH
_TABLES = [((_PAIRS, CHUNK), jnp.int32), ((_PAIRS, CHUNK), jnp.int32),
           ((FLAT, 2 * LANES), BF16), ((2 * LANES, FLAT), BF16), ((SUBLANES, LANES), F32)]
_PARAMS = [(SUBLANES, LANES), (2 * SSM_GROUP_CH, LANES), (2 * SSM_GROUP_CH, LANES),
           (LANES, 2 * SSM_GROUP_CH)]


def _group_spec(shape, n=1):
    return pl.BlockSpec((n,) + shape, lambda g: (g, 0, 0))


def _row_blocks(shape, rows):
    n = shape[0] // rows
    assert shape[0] % rows == 0 and n <= N_CHUNKS and rows % (2 * SUBLANES) == 0
    return pl.BlockSpec((rows, shape[1]), lambda c: (jnp.minimum(c, n - 1), 0))


_MIX_WEIGHT_COLS = (SSM_WIDTH, D_MODEL, D_MODEL, D_MODEL)


def _col_blocks(shape):
    n = shape[1] // LANES
    assert shape[1] % LANES == 0 and n <= N_CHUNKS
    return (pl.BlockSpec((shape[0], LANES), lambda c: (0, jnp.minimum(c, n - 1))),
            pl.BlockSpec((LANES, shape[0]), lambda c: (jnp.minimum(c, n - 1), 0)))


def _in_proj(x, g1, w_t, cos_t, sin_t, rowp, bt, cw, ct, wg, wu, wd, transposed):
    assert N_CHUNKS == SSM_GROUPS
    assert tuple(w.shape[1] for w in transposed) == _MIX_WEIGHT_COLS
    tok = lambda rows, dt: jax.ShapeDtypeStruct((BATCH, rows, SEQ), dt)
    tspec = lambda rows: pl.BlockSpec((BATCH, rows, CHUNK), lambda c: (0, 0, c))
    casts = [_row_blocks(wg.shape, D_MODEL // N_CHUNKS), _row_blocks(wu.shape, D_MODEL // N_CHUNKS),
             _row_blocks(wd.shape, CHUNK)]
    t_in, t_out = zip(*[_col_blocks(w.shape) for w in transposed])
    return pl.pallas_call(
        _in_proj_kernel,
        grid=(N_CHUNKS,),
        in_specs=[
            pl.BlockSpec((BATCH, CHUNK, D_MODEL), lambda c: (0, c, 0)),
            pl.BlockSpec((1, D_MODEL), lambda c: (0, 0)),
            pl.BlockSpec((IN_COLS, D_MODEL), lambda c: (0, 0), pipeline_mode=pl.Buffered(1)),
            pl.BlockSpec((ROPE_HALF, CHUNK), lambda c: (0, c)),
            pl.BlockSpec((ROPE_HALF, CHUNK), lambda c: (0, c)),
        ] + [_group_spec(s) for s in _PARAMS] + casts + list(t_in),
        out_specs=[
            tspec(ATTN_WIDTH), tspec(KV_WIDTH), tspec(KV_WIDTH),
            pl.BlockSpec((1, SSM_WIDTH * BATCH, CHUNK), lambda c: (c, 0, 0)),
            tspec(D_MODEL), tspec(D_MODEL),
        ] + [_group_spec(s) for s, _ in _TABLES] + casts + list(t_out),
        out_shape=[
            tok(ATTN_WIDTH, BF16), tok(KV_WIDTH, BF16), tok(KV_WIDTH, BF16),
            jax.ShapeDtypeStruct((N_CHUNKS, SSM_WIDTH * BATCH, CHUNK), F32),
            tok(D_MODEL, BF16), tok(D_MODEL, BF16),
        ] + [jax.ShapeDtypeStruct((SSM_GROUPS,) + s, dt) for s, dt in _TABLES]
        + [jax.ShapeDtypeStruct(w.shape, BF16) for w in (wg, wu, wd)]
        + [jax.ShapeDtypeStruct(w.shape[::-1], BF16) for w in transposed],
        compiler_params=pltpu.CompilerParams(vmem_limit_bytes=VMEM_LIMIT),
        name="in_proj",
    )(x, g1, w_t, cos_t, sin_t, rowp, bt, cw, ct, wg, wu, wd, *transposed)


def _cmul(a_r, a_i, b_r, b_i):
    return a_r * b_r - a_i * b_i, a_r * b_i + a_i * b_r


def _ssm_kernel(u_ref, wf_ref, wb_ref, in_ref, out_ref, lc_ref, drow_ref, y_ref):
    groups = range(SSM_STEP_GROUPS)
    data = [_ssm_states(g, (lc_ref[g, 0:1, :], lc_ref[g, 1:2, :]), u_ref, in_ref) for g in groups]
    for g in groups:
        _ssm_outputs(g, *data[g], drow_ref, y_ref, wf_ref, wb_ref, out_ref)


def _ssm_tables(g, rowp_ref, bt_ref, cw_ref, ct_ref, wf_ref, wb_ref, in_ref, out_ref):
    half = SSM_STATE
    top_col = lax.broadcasted_iota(jnp.int32, (LANES, 1), 0) < half

    lam_r, lam_i = rowp_ref[g, 0:1, :], rowp_ref[g, 1:2, :]
    dt = jnp.exp(rowp_ref[g, 2:3, :])
    z_r, z_i = lam_r * dt, lam_i * dt
    e1 = jnp.exp(z_r)
    lb_r, lb_i = e1 * jnp.cos(z_i), e1 * jnp.sin(z_i)
    den = lam_r * lam_r + lam_i * lam_i
    cf_r = ((lb_r - 1.0) * lam_r + lb_i * lam_i) / den
    cf_i = (lb_i * lam_r - (lb_r - 1.0) * lam_i) / den
    b_r, b_i = bt_ref[g, 0:SSM_GROUP_CH, :], bt_ref[g, SSM_GROUP_CH:, :]
    bb_r = cf_r * b_r - cf_i * b_i
    bb_i = cf_r * b_i + cf_i * b_r
    c_r, c_i = cw_ref[g, 0:SSM_GROUP_CH, :], cw_ref[g, SSM_GROUP_CH:, :]

    sq_r, sq_i = [lb_r], [lb_i]
    for _ in range(LOG2_CHUNK):
        r, i = _cmul(sq_r[-1], sq_i[-1], sq_r[-1], sq_i[-1])
        sq_r.append(r)
        sq_i.append(i)
    lc_r, lc_i = sq_r[LOG2_CHUNK], sq_i[LOG2_CHUNK]
    pad = jnp.zeros((LANES - 2 * SUBLANES, LANES), F32)
    cols = jnp.concatenate(sq_r[:SUBLANES] + sq_i[:SUBLANES] + [pad], axis=0).T
    sq_rc = [cols[:, k:k + 1] for k in range(LOG2_CHUNK)]
    sq_ic = [cols[:, SUBLANES + k:SUBLANES + k + 1] for k in range(LOG2_CHUNK)]

    j_idx = lax.broadcasted_iota(jnp.int32, (LANES, CHUNK), 1)
    one, zero = jnp.ones((LANES, CHUNK), F32), jnp.zeros((LANES, CHUNK), F32)
    asc_r, asc_i, dsc_r, dsc_i = one, zero, one, zero
    for k in range(LOG2_CHUNK):
        bit = (j_idx & (1 << k)) != 0
        n_r, n_i = _cmul(asc_r, asc_i, sq_rc[k], sq_ic[k])
        asc_r, asc_i = jnp.where(bit, n_r, asc_r), jnp.where(bit, n_i, asc_i)
        n_r, n_i = _cmul(dsc_r, dsc_i, sq_rc[k], sq_ic[k])
        dsc_r, dsc_i = jnp.where(bit, dsc_r, n_r), jnp.where(bit, dsc_i, n_i)
    q_r, q_i = _cmul(jnp.where(top_col, asc_r, dsc_r), jnp.where(top_col, asc_i, dsc_i),
                     sq_rc[0], sq_ic[0])
    p_r = jnp.where(top_col, dsc_r, asc_r).T
    p_i = jnp.where(top_col, dsc_i, asc_i).T

    first = j_idx == 0
    ef_r = jnp.where(top_col, asc_r, jnp.where(first, 1.0, 0.0))
    ef_i = jnp.where(top_col, asc_i, 0.0)
    eb_r = jnp.where(top_col, 0.0, jnp.where(first, 0.0, q_r))
    eb_i = jnp.where(top_col, 0.0, jnp.where(first, 0.0, q_i))
    e_r = jnp.concatenate([eb_r, ef_r], axis=1)
    e_i = jnp.concatenate([eb_i, ef_i], axis=1)
    cb_r = jnp.concatenate([c_r * bb_r[h:h + 1] - c_i * bb_i[h:h + 1]
                            for h in range(SSM_GROUP_CH)], axis=0)
    cb_i = jnp.concatenate([c_r * bb_i[h:h + 1] + c_i * bb_r[h:h + 1]
                            for h in range(SSM_GROUP_CH)], axis=0)
    split = lambda t: (t.astype(BF16), (t - t.astype(BF16).astype(F32)).astype(BF16))
    (cr_h, cr_l), (ci_h, ci_l) = split(cb_r), split(-cb_i)
    (er_h, er_l), (ei_h, ei_l) = split(e_r), split(e_i)
    ktab = _dot(jnp.concatenate([cr_h, cr_h, cr_l, ci_h, ci_h, ci_l], axis=1),
                jnp.concatenate([er_h, er_l, er_h, ei_h, ei_l, ei_h], axis=0))
    kb, kf = ktab[:, :CHUNK], ktab[:, CHUNK:]
    kb1 = pltpu.roll(kb, 1, 1)
    kf1 = jnp.where(lax.broadcasted_iota(jnp.int32, (1, CHUNK), 1) == 0, kb1,
                    pltpu.roll(kf, 1, 1))
    bf16_bits = lambda t: lax.bitcast_convert_type(t.astype(BF16).astype(F32), jnp.int32)
    pack = lambda even, odd: bf16_bits(odd) | lax.shift_right_logical(bf16_bits(even), 16)
    wf_ref[g] = pack(kf, kf1)
    wb_ref[g] = pack(kb, kb1)

    for h in range(SSM_GROUP_CH):
        rows = slice(h * CHUNK, (h + 1) * CHUNK)
        in_ref[g, rows, 0:LANES] = (p_r * bb_r[h:h + 1] - p_i * bb_i[h:h + 1]).astype(BF16)
        in_ref[g, rows, LANES:] = (p_r * bb_i[h:h + 1] + p_i * bb_r[h:h + 1]).astype(BF16)

    for hp in range(SSM_GROUP_CH):
        cols = slice(hp * CHUNK, (hp + 1) * CHUNK)
        cc_r = ct_ref[g, :, hp:hp + 1]
        cc_i = ct_ref[g, :, SSM_GROUP_CH + hp:SSM_GROUP_CH + hp + 1]
        out_ref[g, 0:LANES, cols] = (cc_r * q_r - cc_i * q_i).astype(BF16)
        out_ref[g, LANES:, cols] = (-(cc_r * q_i + cc_i * q_r)).astype(BF16)
    return lc_r, lc_i


def _ssm_states(g, lam_chunk, u_ref, in_ref):
    lc_r, lc_i = lam_chunk
    lo_row = lax.broadcasted_iota(jnp.int32, (1, LANES), 1) < SSM_STATE
    row0 = g * SSM_GROUP_CH * BATCH
    a = jnp.concatenate(
        [u_ref[:, row0 + h * BATCH:row0 + (h + 1) * BATCH, :].reshape(ROWS, CHUNK)
         for h in range(SSM_GROUP_CH)], axis=1)
    ab = a.astype(BF16)
    e = _dot(ab, in_ref[g])
    s_r = jnp.zeros((BATCH, LANES), F32)
    s_i = jnp.zeros((BATCH, LANES), F32)
    hist_r, hist_i = [], []
    for i in range(N_CHUNKS):
        hist_r.append(s_r)
        hist_i.append(s_i)
        fr = slice(i * BATCH, (i + 1) * BATCH)
        br = slice((N_CHUNKS - 1 - i) * BATCH, (N_CHUNKS - i) * BATCH)
        in_r = jnp.where(lo_row, e[fr, 0:LANES], e[br, 0:LANES])
        in_i = jnp.where(lo_row, e[fr, LANES:], e[br, LANES:])
        s_r, s_i = (s_r * lc_r - s_i * lc_i + in_r, s_r * lc_i + s_i * lc_r + in_i)
    states = jnp.concatenate(
        [jnp.concatenate([jnp.where(lo_row, hist_r[c], hist_r[N_CHUNKS - 1 - c]),
                          jnp.where(lo_row, hist_i[c], hist_i[N_CHUNKS - 1 - c])], axis=1)
         for c in range(N_CHUNKS)], axis=0).astype(BF16)
    return a, ab, states


def _ssm_outputs(g, a, ab, states, drow_ref, y_ref, wf_ref, wb_ref, out_ref):
    row0 = g * SSM_GROUP_CH * BATCH

    pair = lax.broadcasted_iota(jnp.int32, (CHUNK // 2, CHUNK), 0)
    lane = lax.broadcasted_iota(jnp.int32, (CHUNK // 2, CHUNK), 1)
    upper = lane + 2 * pair < CHUNK
    per_slab = MXU_DEPTH // CHUNK

    def toeplitz_slab(slab):
        col_blocks = []
        for h in range(SSM_GROUP_CH):
            blocks = []
            for hp in range(slab * per_slab, (slab + 1) * per_slab):
                row = pl.ds(h * SSM_GROUP_CH + hp, 1)
                w = jnp.where(upper,
                              jnp.broadcast_to(wf_ref[g, row, :], (CHUNK // 2, CHUNK)),
                              jnp.broadcast_to(wb_ref[g, row, :], (CHUNK // 2, CHUNK)))
                w = pltpu.roll(w, 0, 1, stride=2, stride_axis=0)
                blocks.append(pltpu.bitcast(w, BF16))
            col_blocks.append(jnp.concatenate(blocks, axis=1))
        return jnp.concatenate(col_blocks, axis=0)

    for slab in range(SSM_GROUP_CH // per_slab):
        cols = slice(slab * MXU_DEPTH, (slab + 1) * MXU_DEPTH)
        y = (_dot(ab, toeplitz_slab(slab)) + _dot(states, out_ref[g, :, cols])
             + a[:, cols] * drow_ref[g][:, cols])
        for k in range(per_slab):
            hp = slab * per_slab + k
            y_ref[:, row0 + hp * BATCH:row0 + (hp + 1) * BATCH, :] = (
                y[:, k * CHUNK:(k + 1) * CHUNK].reshape(N_CHUNKS, BATCH, CHUNK))


def _ssm(u4, tables, drow):
    ng = SSM_STEP_GROUPS
    data = pl.BlockSpec((N_CHUNKS, ng * SSM_GROUP_CH * BATCH, CHUNK), lambda g: (0, g, 0))
    return pl.pallas_call(
        _ssm_kernel,
        grid=(SSM_GROUPS // ng,),
        in_specs=[data] + [_group_spec(s, ng) for s, _ in _TABLES] + [_group_spec((1, FLAT), ng)],
        out_specs=data,
        out_shape=jax.ShapeDtypeStruct((N_CHUNKS, SSM_WIDTH * BATCH, CHUNK), F32),
        compiler_params=pltpu.CompilerParams(vmem_limit_bytes=VMEM_LIMIT),
        name="ssm",
    )(u4, *tables, drow)


_ATTN_CASES = [(b, j) for b in range(BATCH) for j in range(N_KV_HEADS)]


def _attn_scores(q_ref, kp_ref, kc_ref, kn_ref):
    zeros = jnp.zeros((HEAD_DIM, GQA_GROUP * CHUNK), BF16)
    keys = []
    for b in range(BATCH):
        k_t = jnp.concatenate([kp_ref[b], kc_ref[b], kn_ref[b]], axis=1)
        keys.append(k_t.astype(F32).T.astype(BF16))
    scores = []
    for b, j in _ATTN_CASES:
        q_t = jnp.concatenate(
            [q_ref[b, (j * GQA_GROUP + g) * HEAD_DIM:(j * GQA_GROUP + g + 1) * HEAD_DIM, :]
             for g in range(GQA_GROUP)], axis=1)
        q_pad = jnp.concatenate([q_t, zeros] if j == 0 else [zeros, q_t], axis=0)
        scores.append(_dot(keys[b], q_pad))
    return scores


def _attn_finish(batches, c, scores, vp_ref, vc_ref, vn_ref, sink_ref):
    kk = lax.broadcasted_iota(jnp.int32, (CHUNK, CHUNK), 0)
    qq = lax.broadcasted_iota(jnp.int32, (CHUNK, CHUNK), 1)
    prev_ok = jnp.concatenate([kk >= qq] * GQA_GROUP, axis=1) & (c > 0)
    next_ok = jnp.concatenate([kk <= qq] * GQA_GROUP, axis=1) & (c < N_CHUNKS - 1)
    ones = jnp.ones((SUM_ROWS, 3 * CHUNK), BF16)
    heads = {}
    for b, j in [(b, j) for b in batches for j in range(N_KV_HEADS)]:
        s = scores[_ATTN_CASES.index((b, j))]
        v_t = jnp.concatenate([vp_ref[b], vc_ref[b], vn_ref[b]], axis=1)
        s = jnp.concatenate([jnp.where(prev_ok, s[:CHUNK], NEG_INF), s[CHUNK:2 * CHUNK],
                             jnp.where(next_ok, s[2 * CHUNK:], NEG_INF)], axis=0)
        sink = sink_ref[j] * LOG2E
        m = jnp.maximum(jnp.max(s, axis=0, keepdims=True), sink)
        p = jnp.exp2(s - m).astype(BF16)
        v_aug = jnp.concatenate([v_t[j * HEAD_DIM:(j + 1) * HEAD_DIM], ones], axis=0)
        o_aug = _dot(v_aug, p)
        o = o_aug[:HEAD_DIM] / (o_aug[HEAD_DIM:HEAD_DIM + 1] + jnp.exp2(sink - m))
        for g in range(GQA_GROUP):
            heads[b, j * GQA_GROUP + g] = o[:, g * CHUNK:(g + 1) * CHUNK].astype(BF16)
    return jnp.concatenate(
        [jnp.concatenate([heads[b, hd] for hd in range(N_HEADS)], axis=0) for b in batches],
        axis=1)


def _mix_kernel(q_ref, kp_ref, kc_ref, kn_ref, vp_ref, vc_ref, vn_ref, sink_ref,
                y_ref, ga_ref, gs_ref, x_ref, wglu_ref, wab_ref, wsb_ref, wout_ref, o_ref):
    scores = _attn_scores(q_ref, kp_ref, kc_ref, kn_ref)
    nb = BATCH // MERGE_SPLIT
    tiles = [range(t * nb, (t + 1) * nb) for t in range(MERGE_SPLIT)]
    cat = lambda ref, bs: jnp.concatenate([ref[b] for b in bs], axis=1)
    y_t = [jnp.concatenate([y_ref[0, pl.ds(b, SSM_WIDTH, stride=BATCH), :] for b in bs],
                           axis=1) for bs in tiles]
    yg = [0.5 * y * (1.0 + jnp.tanh(math.sqrt(2.0 / math.pi) * (y + 0.044715 * y ** 3)))
          for y in y_t]
    z_t = [_dot(wglu_ref[...], g.astype(BF16)) for g in yg]
    ssm_t = [g * _sigmoid(z) for g, z in zip(yg, z_t)]
    s_t = [_dot(wsb_ref[...], t.astype(BF16)) for t in ssm_t]
    attn = [_attn_finish(bs, pl.program_id(0), scores, vp_ref, vc_ref, vn_ref, sink_ref)
            for bs in tiles]
    a_t = [_dot(wab_ref[...], t) for t in attn]
    merged = [cat(ga_ref, bs).astype(F32) * a + cat(gs_ref, bs).astype(F32) * sb
              for bs, a, sb in zip(tiles, a_t, s_t)]
    for bs, m in zip(tiles, merged):
        mb = m.astype(BF16)
        for r in range(0, D_MODEL, MXU_DEPTH):
            o = _dot(wout_ref[r:r + MXU_DEPTH, :], mb).T
            o_ref[bs.start:bs.stop, :, r:r + MXU_DEPTH] = (
                x_ref[bs.start:bs.stop, :, r:r + MXU_DEPTH] + o.reshape(nb, CHUNK, MXU_DEPTH))


def _mix(q_t, k_t, v_t, sink_rows, y4, ga_t, gs_t, x, wglu_t, wab_t, wsb_t, wout_t):
    prev = lambda c: (0, 0, jnp.maximum(c - 1, 0))
    cur = lambda c: (0, 0, c)
    nxt = lambda c: (0, 0, jnp.minimum(c + 1, N_CHUNKS - 1))
    tspec = lambda rows, im=cur: pl.BlockSpec((BATCH, rows, CHUNK), im)
    wspec = lambda r, k: pl.BlockSpec((r, k), lambda c: (0, 0), pipeline_mode=pl.Buffered(1))
    xspec = pl.BlockSpec((BATCH, CHUNK, D_MODEL), lambda c: (0, c, 0))
    return pl.pallas_call(
        _mix_kernel,
        grid=(N_CHUNKS,),
        in_specs=[tspec(ATTN_WIDTH),
                  tspec(KV_WIDTH, prev), tspec(KV_WIDTH), tspec(KV_WIDTH, nxt),
                  tspec(KV_WIDTH, prev), tspec(KV_WIDTH), tspec(KV_WIDTH, nxt),
                  pl.BlockSpec((N_KV_HEADS, 1, GQA_GROUP * CHUNK), lambda c: (0, 0, 0)),
                  pl.BlockSpec((1, SSM_WIDTH * BATCH, CHUNK), lambda c: (c, 0, 0)),
                  tspec(D_MODEL), tspec(D_MODEL), xspec,
                  wspec(SSM_WIDTH, SSM_WIDTH), wspec(D_MODEL, ATTN_WIDTH),
                  wspec(D_MODEL, SSM_WIDTH), wspec(D_MODEL, D_MODEL)],
        out_specs=xspec,
        out_shape=jax.ShapeDtypeStruct((BATCH, SEQ, D_MODEL), F32),
        compiler_params=pltpu.CompilerParams(vmem_limit_bytes=VMEM_LIMIT),
        name="mix",
    )(q_t, k_t, k_t, k_t, v_t, v_t, v_t, sink_rows, y4, ga_t, gs_t, x,
      wglu_t, wab_t, wsb_t, wout_t)


def _ffn_kernel(x_ref, g2_ref, wg_ref, wu_ref, wd_ref, gf_ref, o_ref):
    tile = FFN_ROWS // FFN_SPLIT
    rows = [slice(t * tile, (t + 1) * tile) for t in range(FFN_SPLIT)]
    h = [_rmsnorm(x_ref[r, :], g2_ref[...]).astype(BF16) for r in rows]
    acc = [jnp.zeros((tile, D_MODEL), F32) for _ in rows]
    for j in range(D_FF // FFN_COLS):
        cols = slice(j * FFN_COLS, (j + 1) * FFN_COLS)
        for t in range(FFN_SPLIT):
            gate = _dot(h[t], wg_ref[:, cols])
            up = _dot(h[t], wu_ref[:, cols])
            act = (gate * _sigmoid(gate) * up).astype(BF16)
            acc[t] = acc[t] + _dot(act, wd_ref[cols, :])
    for t, r in enumerate(rows):
        o_ref[r, :] = _rmsnorm(x_ref[r, :] + acc[t], gf_ref[...])


def _ffn(x, g2, wg, wu, wd, gf):
    n_tok = BATCH * SEQ
    row = pl.BlockSpec((FFN_ROWS, D_MODEL), lambda i: (i, 0))
    vec = pl.BlockSpec((1, D_MODEL), lambda i: (0, 0))
    wspec = lambda r, k: pl.BlockSpec((r, k), lambda i: (0, 0), pipeline_mode=pl.Buffered(1))
    return pl.pallas_call(
        _ffn_kernel,
        grid=(n_tok // FFN_ROWS,),
        in_specs=[row, vec, wspec(D_MODEL, D_FF), wspec(D_MODEL, D_FF),
                  wspec(D_FF, D_MODEL), vec],
        out_specs=row,
        out_shape=jax.ShapeDtypeStruct((n_tok, D_MODEL), F32),
        compiler_params=pltpu.CompilerParams(vmem_limit_bytes=VMEM_LIMIT),
        name="ffn",
    )(x, g2, wg, wu, wd, gf)


def _ssm_param_layouts(lam_re, lam_im, log_dt, b_re, b_im, c_re, c_im, d):
    both = lambda t: jnp.concatenate([t[0], t[1]], axis=-1)
    ldt = jnp.broadcast_to(log_dt[:, :, None], (2, SSM_GROUPS, SSM_STATE))
    rows = jnp.stack([both(lam_re), both(lam_im), both(ldt)], axis=1)
    rowp = jnp.pad(rows, ((0, 0), (0, SUBLANES - 3), (0, 0)))
    bt =jnp.concatenate([both(jnp.swapaxes(b_re, 2, 3)),
                          both(jnp.swapaxes(b_im, 2, 3))], axis=1)
    cw = jnp.concatenate([jnp.tile(c_re, (1, 1, 2)), jnp.tile(c_im, (1, 1, 2))], axis=1)
    ct = jnp.swapaxes(cw, 1, 2)
    drow = jnp.repeat(d, CHUNK, axis=-1)[:, None, :]
    return rowp, bt, cw, ct, drow


def kernel(x, norm1_g, w_in, attn_sink, ssm_lambda_re, ssm_lambda_im, ssm_log_dt, ssm_b_re, ssm_b_im, ssm_c_re, ssm_c_im, ssm_d, w_glu, w_attn_branch, w_ssm_branch, w_out, norm2_g, w_ffn_gate, w_ffn_up, w_ffn_down, norm_f_g):
    pos = jnp.arange(SEQ, dtype=F32)
    inv_freq = ROPE_THETA ** (-jnp.arange(0, ROPE_DIM, 2, dtype=F32) / ROPE_DIM)
    ang = inv_freq[:, None] * pos[None, :]
    cos_t, sin_t = jnp.cos(ang), jnp.sin(ang)
    sink_rows = jnp.repeat(attn_sink[0], CHUNK).reshape(
        N_KV_HEADS, 1, GQA_GROUP * CHUNK)

    rowp, bt, cw, ct, drow = _ssm_param_layouts(
        ssm_lambda_re[0], ssm_lambda_im[0], ssm_log_dt[0], ssm_b_re[0], ssm_b_im[0],
        ssm_c_re[0], ssm_c_im[0], ssm_d[0])
    q_t, k_t, v_t, u4, ga_t, gs_t, *rest = _in_proj(
        x, norm1_g[0][None, :], w_in[0].T.astype(BF16), cos_t, sin_t, rowp, bt, cw, ct,
        w_ffn_gate[0], w_ffn_up[0], w_ffn_down[0],
        (w_glu[0], w_attn_branch[0], w_ssm_branch[0], w_out[0]))
    n_t = len(_TABLES)
    tables, (wg, wu, wd), mix_weights = rest[:n_t], rest[n_t:n_t + 3], rest[n_t + 3:]
    y4 = _ssm(u4, tables, drow)
    x1 = _mix(q_t, k_t, v_t, sink_rows, y4, ga_t, gs_t, x, *mix_weights)
    out = _ffn(x1.reshape(BATCH * SEQ, D_MODEL), norm2_g[0][None, :], wg, wu, wd,
               norm_f_g[None, :])
    return out.reshape(BATCH, SEQ, D_MODEL)
```

```python
import math

import jax
import jax.numpy as jnp
from jax import lax
from jax.experimental import pallas as pl
from jax.experimental.pallas import tpu as pltpu

D_MODEL = 1024
BATCH = 8
SEQ = 4096
N_HEADS = 8
N_KV_HEADS = 2
HEAD_DIM = 64
GQA_GROUP = N_HEADS // N_KV_HEADS
ATTN_WIDTH = N_HEADS * HEAD_DIM
KV_WIDTH = N_KV_HEADS * HEAD_DIM
ROPE_DIM = HEAD_DIM // 4
ROPE_HALF = ROPE_DIM // 2
ROPE_THETA = 500000.0
WINDOW = 128
SSM_GROUP_CH = 16
SSM_WIDTH = D_MODEL // 2
SSM_GROUPS = SSM_WIDTH // SSM_GROUP_CH
SSM_STATE = 64
D_FF = 2816
IN_COLS = ATTN_WIDTH + 2 * KV_WIDTH + SSM_WIDTH + 2 * D_MODEL
RMS_EPS = 1e-6
NEG_INF = -1e30

LANES = 128
SUBLANES = 8
MXU_DEPTH = 256
CHUNK = LANES
LOG2_CHUNK = CHUNK.bit_length() - 1
N_CHUNKS = SEQ // CHUNK
ROWS = N_CHUNKS * BATCH
FLAT = SSM_GROUP_CH * CHUNK
SSM_STEP_GROUPS = 4
MERGE_SPLIT = 4
LOG2E = math.log2(math.e)
SUM_ROWS = 16
FFN_ROWS = 1024
FFN_SPLIT = 2
FFN_COLS = 256
VMEM_LIMIT = 58 * 1024 * 1024

assert BATCH == SUBLANES and WINDOW == CHUNK and D_FF % FFN_COLS == 0

_Q0 = 0
_K0 = _Q0 + ATTN_WIDTH
_V0 = _K0 + KV_WIDTH
_U0 = _V0 + KV_WIDTH
_GA0 = _U0 + SSM_WIDTH
_GS0 = _GA0 + D_MODEL
assert _GS0 + D_MODEL == IN_COLS

BF16 = jnp.bfloat16
F32 = jnp.float32


def _sigmoid(x):
    return 0.5 * jnp.tanh(0.5 * x) + 0.5


def _rmsnorm(x, g):
    return x * lax.rsqrt(jnp.mean(x * x, axis=-1, keepdims=True) + RMS_EPS) * g


def _dot(a, b):
    return jnp.dot(a, b, preferred_element_type=F32)


def _in_proj_kernel(x_ref, g_ref, w_ref, cos_ref, sin_ref, rowp_ref, bt_ref, cw_ref, ct_ref,
                    wg32_ref, wu32_ref, wd32_ref,
                    q_ref, k_ref, v_ref, u_ref, ga_ref, gs_ref,
                    wf_ref, wb_ref, in_ref, out_ref, lc_ref, wg_ref, wu_ref, wd_ref):
    for src, dst in ((wg32_ref, wg_ref), (wu32_ref, wu_ref), (wd32_ref, wd_ref)):
        dst[...] = src[...].astype(BF16)

    q_ref, k_ref, v_ref, ga_ref, gs_ref = (r.at[0] for r in (q_ref, k_ref, v_ref, ga_ref, gs_ref))
    h = _rmsnorm(x_ref[...].reshape(BATCH * CHUNK, D_MODEL), g_ref[...]).astype(BF16)

    def proj_t(r0, r1):
        return lax.dot_general(w_ref[r0:r1, :], h, (((1,), (1,)), ((), ())),
                               preferred_element_type=F32)

    cos = jnp.concatenate([cos_ref[...]] * BATCH, axis=1)
    sin = jnp.concatenate([sin_ref[...]] * BATCH, axis=1)

    def rope_t(t, n_heads):
        parts = []
        for hd in range(n_heads):
            r1 = t[hd * HEAD_DIM:hd * HEAD_DIM + ROPE_HALF]
            r2 = t[hd * HEAD_DIM + ROPE_HALF:hd * HEAD_DIM + ROPE_DIM]
            parts += [r1 * cos - r2 * sin, r2 * cos + r1 * sin,
                      t[hd * HEAD_DIM + ROPE_DIM:(hd + 1) * HEAD_DIM]]
        return jnp.concatenate(parts, axis=0)

    def put(ref, r0, t):
        for b in range(BATCH):
            ref[b, r0:r0 + t.shape[0], :] = t[:, b * CHUNK:(b + 1) * CHUNK].astype(ref.dtype)

    slab = MXU_DEPTH
    for r in range(0, ATTN_WIDTH, slab):
        t = proj_t(_Q0 + r, _Q0 + r + slab) * (HEAD_DIM ** -0.5 * LOG2E)
        put(q_ref, r, rope_t(t, slab // HEAD_DIM))
    kv = proj_t(_K0, _U0)
    put(k_ref, 0, rope_t(kv[:KV_WIDTH], N_KV_HEADS))
    put(v_ref, 0, kv[KV_WIDTH:])
    for r in range(0, SSM_WIDTH, slab):
        u_t = proj_t(_U0 + r, _U0 + r + slab)
        for b in range(BATCH):
            u_ref[0, pl.ds(r * BATCH + b, slab, stride=BATCH), :] = (
                u_t[:, b * CHUNK:(b + 1) * CHUNK])
    for r in range(0, D_MODEL, slab):
        put(ga_ref, r, _sigmoid(proj_t(_GA0 + r, _GA0 + r + slab)))

    lc_r, lc_i = _ssm_tables(0, rowp_ref, bt_ref, cw_ref, ct_ref, wf_ref, wb_ref, in_ref, out_ref)
    lc_ref[0] = jnp.concatenate([lc_r, lc_i, jnp.zeros((SUBLANES - 2, LANES), F32)], axis=0)

    for r in range(0, D_MODEL, slab):
        put(gs_ref, r, _sigmoid(proj_t(_GS0 + r, _GS0 + r + slab)))


_PAIRS = SSM_GROUP_CH * SSM_GROUP_CH
_TABLES = [((_PAIRS, CHUNK), jnp.int32), ((_PAIRS, CHUNK), jnp.int32),
           ((FLAT, 2 * LANES), BF16), ((2 * LANES, FLAT), BF16), ((SUBLANES, LANES), F32)]
_PARAMS = [(SUBLANES, LANES), (2 * SSM_GROUP_CH, LANES), (2 * SSM_GROUP_CH, LANES),
           (LANES, 2 * SSM_GROUP_CH)]


def _group_spec(shape, n=1):
    return pl.BlockSpec((n,) + shape, lambda g: (g, 0, 0))


def _row_blocks(shape, rows):
    n = shape[0] // rows
    assert shape[0] % rows == 0 and n <= N_CHUNKS and rows % (2 * SUBLANES) == 0
    return pl.BlockSpec((rows, shape[1]), lambda c: (jnp.minimum(c, n - 1), 0))


def _in_proj(x, g1, w_t, cos_t, sin_t, rowp, bt, cw, ct, wg, wu, wd):
    assert N_CHUNKS == SSM_GROUPS
    tok = lambda rows, dt: jax.ShapeDtypeStruct((N_CHUNKS, BATCH, rows, CHUNK), dt)
    tspec = lambda rows: pl.BlockSpec((1, BATCH, rows, CHUNK), lambda c: (c, 0, 0, 0))
    casts = [_row_blocks(wg.shape, D_MODEL // N_CHUNKS), _row_blocks(wu.shape, D_MODEL // N_CHUNKS),
             _row_blocks(wd.shape, CHUNK)]
    return pl.pallas_call(
        _in_proj_kernel,
        grid=(N_CHUNKS,),
        in_specs=[
            pl.BlockSpec((BATCH, CHUNK, D_MODEL), lambda c: (0, c, 0)),
            pl.BlockSpec((1, D_MODEL), lambda c: (0, 0)),
            pl.BlockSpec((IN_COLS, D_MODEL), lambda c: (0, 0), pipeline_mode=pl.Buffered(1)),
            pl.BlockSpec((ROPE_HALF, CHUNK), lambda c: (0, c)),
            pl.BlockSpec((ROPE_HALF, CHUNK), lambda c: (0, c)),
        ] + [_group_spec(s) for s in _PARAMS] + casts,
        out_specs=[
            tspec(ATTN_WIDTH), tspec(KV_WIDTH), tspec(KV_WIDTH),
            pl.BlockSpec((1, SSM_WIDTH * BATCH, CHUNK), lambda c: (c, 0, 0)),
            tspec(D_MODEL), tspec(D_MODEL),
        ] + [_group_spec(s) for s, _ in _TABLES] + casts,
        out_shape=[
            tok(ATTN_WIDTH, BF16), tok(KV_WIDTH, BF16), tok(KV_WIDTH, BF16),
            jax.ShapeDtypeStruct((N_CHUNKS, SSM_WIDTH * BATCH, CHUNK), F32),
            tok(D_MODEL, BF16), tok(D_MODEL, BF16),
        ] + [jax.ShapeDtypeStruct((SSM_GROUPS,) + s, dt) for s, dt in _TABLES]
        + [jax.ShapeDtypeStruct(w.shape, BF16) for w in (wg, wu, wd)],
        compiler_params=pltpu.CompilerParams(vmem_limit_bytes=VMEM_LIMIT),
        name="in_proj",
    )(x, g1, w_t, cos_t, sin_t, rowp, bt, cw, ct, wg, wu, wd)


def _cmul(a_r, a_i, b_r, b_i):
    return a_r * b_r - a_i * b_i, a_r * b_i + a_i * b_r


def _ssm_kernel(u_ref, wf_ref, wb_ref, in_ref, out_ref, lc_ref, drow_ref, y_ref):
    groups = range(SSM_STEP_GROUPS)
    data = [_ssm_states(g, (lc_ref[g, 0:1, :], lc_ref[g, 1:2, :]), u_ref, in_ref) for g in groups]
    for g in groups:
        _ssm_outputs(g, *data[g], drow_ref, y_ref, wf_ref, wb_ref, out_ref)


def _ssm_tables(g, rowp_ref, bt_ref, cw_ref, ct_ref, wf_ref, wb_ref, in_ref, out_ref):
    half = SSM_STATE
    top_col = lax.broadcasted_iota(jnp.int32, (LANES, 1), 0) < half

    lam_r, lam_i = rowp_ref[g, 0:1, :], rowp_ref[g, 1:2, :]
    dt = jnp.exp(rowp_ref[g, 2:3, :])
    z_r, z_i = lam_r * dt, lam_i * dt
    e1 = jnp.exp(z_r)
    lb_r, lb_i = e1 * jnp.cos(z_i), e1 * jnp.sin(z_i)
    den = lam_r * lam_r + lam_i * lam_i
    cf_r = ((lb_r - 1.0) * lam_r + lb_i * lam_i) / den
    cf_i = (lb_i * lam_r - (lb_r - 1.0) * lam_i) / den
    b_r, b_i = bt_ref[g, 0:SSM_GROUP_CH, :], bt_ref[g, SSM_GROUP_CH:, :]
    bb_r = cf_r * b_r - cf_i * b_i
    bb_i = cf_r * b_i + cf_i * b_r
    c_r, c_i = cw_ref[g, 0:SSM_GROUP_CH, :], cw_ref[g, SSM_GROUP_CH:, :]

    sq_r, sq_i = [lb_r], [lb_i]
    for _ in range(LOG2_CHUNK):
        r, i = _cmul(sq_r[-1], sq_i[-1], sq_r[-1], sq_i[-1])
        sq_r.append(r)
        sq_i.append(i)
    lc_r, lc_i = sq_r[LOG2_CHUNK], sq_i[LOG2_CHUNK]
    pad = jnp.zeros((LANES - 2 * SUBLANES, LANES), F32)
    cols = jnp.concatenate(sq_r[:SUBLANES] + sq_i[:SUBLANES] + [pad], axis=0).T
    sq_rc = [cols[:, k:k + 1] for k in range(LOG2_CHUNK)]
    sq_ic = [cols[:, SUBLANES + k:SUBLANES + k + 1] for k in range(LOG2_CHUNK)]

    j_idx = lax.broadcasted_iota(jnp.int32, (LANES, CHUNK), 1)
    one, zero = jnp.ones((LANES, CHUNK), F32), jnp.zeros((LANES, CHUNK), F32)
    asc_r, asc_i, dsc_r, dsc_i = one, zero, one, zero
    for k in range(LOG2_CHUNK):
        bit = (j_idx & (1 << k)) != 0
        n_r, n_i = _cmul(asc_r, asc_i, sq_rc[k], sq_ic[k])
        asc_r, asc_i = jnp.where(bit, n_r, asc_r), jnp.where(bit, n_i, asc_i)
        n_r, n_i = _cmul(dsc_r, dsc_i, sq_rc[k], sq_ic[k])
        dsc_r, dsc_i = jnp.where(bit, dsc_r, n_r), jnp.where(bit, dsc_i, n_i)
    q_r, q_i = _cmul(jnp.where(top_col, asc_r, dsc_r), jnp.where(top_col, asc_i, dsc_i),
                     sq_rc[0], sq_ic[0])
    p_r = jnp.where(top_col, dsc_r, asc_r).T
    p_i = jnp.where(top_col, dsc_i, asc_i).T

    first = j_idx == 0
    ef_r = jnp.where(top_col, asc_r, jnp.where(first, 1.0, 0.0))
    ef_i = jnp.where(top_col, asc_i, 0.0)
    eb_r = jnp.where(top_col, 0.0, jnp.where(first, 0.0, q_r))
    eb_i = jnp.where(top_col, 0.0, jnp.where(first, 0.0, q_i))
    e_r = jnp.concatenate([eb_r, ef_r], axis=1)
    e_i = jnp.concatenate([eb_i, ef_i], axis=1)
    cb_r = jnp.concatenate([c_r * bb_r[h:h + 1] - c_i * bb_i[h:h + 1]
                            for h in range(SSM_GROUP_CH)], axis=0)
    cb_i = jnp.concatenate([c_r * bb_i[h:h + 1] + c_i * bb_r[h:h + 1]
                            for h in range(SSM_GROUP_CH)], axis=0)
    split = lambda t: (t.astype(BF16), (t - t.astype(BF16).astype(F32)).astype(BF16))
    (cr_h, cr_l), (ci_h, ci_l) = split(cb_r), split(-cb_i)
    (er_h, er_l), (ei_h, ei_l) = split(e_r), split(e_i)
    ktab = _dot(jnp.concatenate([cr_h, cr_h, cr_l, ci_h, ci_h, ci_l], axis=1),
                jnp.concatenate([er_h, er_l, er_h, ei_h, ei_l, ei_h], axis=0))
    kb, kf = ktab[:, :CHUNK], ktab[:, CHUNK:]
    kb1 = pltpu.roll(kb, 1, 1)
    kf1 = jnp.where(lax.broadcasted_iota(jnp.int32, (1, CHUNK), 1) == 0, kb1,
                    pltpu.roll(kf, 1, 1))
    bf16_bits = lambda t: lax.bitcast_convert_type(t.astype(BF16).astype(F32), jnp.int32)
    pack = lambda even, odd: bf16_bits(odd) | lax.shift_right_logical(bf16_bits(even), 16)
    wf_ref[g] = pack(kf, kf1)
    wb_ref[g] = pack(kb, kb1)

    for h in range(SSM_GROUP_CH):
        rows = slice(h * CHUNK, (h + 1) * CHUNK)
        in_ref[g, rows, 0:LANES] = (p_r * bb_r[h:h + 1] - p_i * bb_i[h:h + 1]).astype(BF16)
        in_ref[g, rows, LANES:] = (p_r * bb_i[h:h + 1] + p_i * bb_r[h:h + 1]).astype(BF16)

    for hp in range(SSM_GROUP_CH):
        cols = slice(hp * CHUNK, (hp + 1) * CHUNK)
        cc_r = ct_ref[g, :, hp:hp + 1]
        cc_i = ct_ref[g, :, SSM_GROUP_CH + hp:SSM_GROUP_CH + hp + 1]
        out_ref[g, 0:LANES, cols] = (cc_r * q_r - cc_i * q_i).astype(BF16)
        out_ref[g, LANES:, cols] = (-(cc_r * q_i + cc_i * q_r)).astype(BF16)
    return lc_r, lc_i


def _ssm_states(g, lam_chunk, u_ref, in_ref):
    lc_r, lc_i = lam_chunk
    lo_row = lax.broadcasted_iota(jnp.int32, (1, LANES), 1) < SSM_STATE
    row0 = g * SSM_GROUP_CH * BATCH
    a = jnp.concatenate(
        [u_ref[:, row0 + h * BATCH:row0 + (h + 1) * BATCH, :].reshape(ROWS, CHUNK)
         for h in range(SSM_GROUP_CH)], axis=1)
    ab = a.astype(BF16)
    e = _dot(ab, in_ref[g])
    s_r = jnp.zeros((BATCH, LANES), F32)
    s_i = jnp.zeros((BATCH, LANES), F32)
    hist_r, hist_i = [], []
    for i in range(N_CHUNKS):
        hist_r.append(s_r)
        hist_i.append(s_i)
        fr = slice(i * BATCH, (i + 1) * BATCH)
        br = slice((N_CHUNKS - 1 - i) * BATCH, (N_CHUNKS - i) * BATCH)
        in_r = jnp.where(lo_row, e[fr, 0:LANES], e[br, 0:LANES])
        in_i = jnp.where(lo_row, e[fr, LANES:], e[br, LANES:])
        s_r, s_i = (s_r * lc_r - s_i * lc_i + in_r, s_r * lc_i + s_i * lc_r + in_i)
    states = jnp.concatenate(
        [jnp.concatenate([jnp.where(lo_row, hist_r[c], hist_r[N_CHUNKS - 1 - c]),
                          jnp.where(lo_row, hist_i[c], hist_i[N_CHUNKS - 1 - c])], axis=1)
         for c in range(N_CHUNKS)], axis=0).astype(BF16)
    return a, ab, states


def _ssm_outputs(g, a, ab, states, drow_ref, y_ref, wf_ref, wb_ref, out_ref):
    row0 = g * SSM_GROUP_CH * BATCH

    pair = lax.broadcasted_iota(jnp.int32, (CHUNK // 2, CHUNK), 0)
    lane = lax.broadcasted_iota(jnp.int32, (CHUNK // 2, CHUNK), 1)
    upper = lane + 2 * pair < CHUNK
    per_slab = MXU_DEPTH // CHUNK

    def toeplitz_slab(slab):
        col_blocks = []
        for h in range(SSM_GROUP_CH):
            blocks = []
            for hp in range(slab * per_slab, (slab + 1) * per_slab):
                row = pl.ds(h * SSM_GROUP_CH + hp, 1)
                w = jnp.where(upper,
                              jnp.broadcast_to(wf_ref[g, row, :], (CHUNK // 2, CHUNK)),
                              jnp.broadcast_to(wb_ref[g, row, :], (CHUNK // 2, CHUNK)))
                w = pltpu.roll(w, 0, 1, stride=2, stride_axis=0)
                blocks.append(pltpu.bitcast(w, BF16))
            col_blocks.append(jnp.concatenate(blocks, axis=1))
        return jnp.concatenate(col_blocks, axis=0)

    for slab in range(SSM_GROUP_CH // per_slab):
        cols = slice(slab * MXU_DEPTH, (slab + 1) * MXU_DEPTH)
        y = (_dot(ab, toeplitz_slab(slab)) + _dot(states, out_ref[g, :, cols])
             + a[:, cols] * drow_ref[g][:, cols])
        for k in range(per_slab):
            hp = slab * per_slab + k
            y_ref[:, row0 + hp * BATCH:row0 + (hp + 1) * BATCH, :] = (
                y[:, k * CHUNK:(k + 1) * CHUNK].reshape(N_CHUNKS, BATCH, CHUNK))


def _ssm(u4, tables, drow):
    ng = SSM_STEP_GROUPS
    data = pl.BlockSpec((N_CHUNKS, ng * SSM_GROUP_CH * BATCH, CHUNK), lambda g: (0, g, 0))
    return pl.pallas_call(
        _ssm_kernel,
        grid=(SSM_GROUPS // ng,),
        in_specs=[data] + [_group_spec(s, ng) for s, _ in _TABLES] + [_group_spec((1, FLAT), ng)],
        out_specs=data,
        out_shape=jax.ShapeDtypeStruct((N_CHUNKS, SSM_WIDTH * BATCH, CHUNK), F32),
        compiler_params=pltpu.CompilerParams(vmem_limit_bytes=VMEM_LIMIT),
        name="ssm",
    )(u4, *tables, drow)


_ATTN_CASES = [(b, j) for b in range(BATCH) for j in range(N_KV_HEADS)]


def _attn_scores(q_ref, kp_ref, kc_ref, kn_ref):
    zeros = jnp.zeros((HEAD_DIM, GQA_GROUP * CHUNK), BF16)
    keys = []
    for b in range(BATCH):
        k_t = jnp.concatenate([kp_ref[b], kc_ref[b], kn_ref[b]], axis=1)
        keys.append(k_t.astype(F32).T.astype(BF16))
    scores = []
    for b, j in _ATTN_CASES:
        q_t = jnp.concatenate(
            [q_ref[b, (j * GQA_GROUP + g) * HEAD_DIM:(j * GQA_GROUP + g + 1) * HEAD_DIM, :]
             for g in range(GQA_GROUP)], axis=1)
        q_pad = jnp.concatenate([q_t, zeros] if j == 0 else [zeros, q_t], axis=0)
        scores.append(_dot(keys[b], q_pad))
    return scores


def _attn_finish(batches, c, scores, vp_ref, vc_ref, vn_ref, sink_ref):
    kk = lax.broadcasted_iota(jnp.int32, (CHUNK, CHUNK), 0)
    qq = lax.broadcasted_iota(jnp.int32, (CHUNK, CHUNK), 1)
    prev_ok = jnp.concatenate([kk >= qq] * GQA_GROUP, axis=1) & (c > 0)
    next_ok = jnp.concatenate([kk <= qq] * GQA_GROUP, axis=1) & (c < N_CHUNKS - 1)
    ones = jnp.ones((SUM_ROWS, 3 * CHUNK), BF16)
    heads = {}
    for b, j in [(b, j) for b in batches for j in range(N_KV_HEADS)]:
        s = scores[_ATTN_CASES.index((b, j))]
        v_t = jnp.concatenate([vp_ref[b], vc_ref[b], vn_ref[b]], axis=1)
        s = jnp.concatenate([jnp.where(prev_ok, s[:CHUNK], NEG_INF), s[CHUNK:2 * CHUNK],
                             jnp.where(next_ok, s[2 * CHUNK:], NEG_INF)], axis=0)
        sink = sink_ref[j] * LOG2E
        m = jnp.maximum(jnp.max(s, axis=0, keepdims=True), sink)
        p = jnp.exp2(s - m).astype(BF16)
        v_aug = jnp.concatenate([v_t[j * HEAD_DIM:(j + 1) * HEAD_DIM], ones], axis=0)
        o_aug = _dot(v_aug, p)
        o = o_aug[:HEAD_DIM] / (o_aug[HEAD_DIM:HEAD_DIM + 1] + jnp.exp2(sink - m))
        for g in range(GQA_GROUP):
            heads[b, j * GQA_GROUP + g] = o[:, g * CHUNK:(g + 1) * CHUNK].astype(BF16)
    return jnp.concatenate(
        [jnp.concatenate([heads[b, hd] for hd in range(N_HEADS)], axis=0) for b in batches],
        axis=1)


def _mix_kernel(q_ref, kp_ref, kc_ref, kn_ref, vp_ref, vc_ref, vn_ref, sink_ref,
                y_ref, ga_ref, gs_ref, x_ref, wglu_ref, wab_ref, wsb_ref, wout_ref, o_ref):
    q_ref, kp_ref, kc_ref, kn_ref, vp_ref, vc_ref, vn_ref, ga_ref, gs_ref = (
        r.at[0] for r in (q_ref, kp_ref, kc_ref, kn_ref, vp_ref, vc_ref, vn_ref, ga_ref, gs_ref))
    scores = _attn_scores(q_ref, kp_ref, kc_ref, kn_ref)
    nb = BATCH // MERGE_SPLIT
    tiles = [range(t * nb, (t + 1) * nb) for t in range(MERGE_SPLIT)]
    cat = lambda ref, bs: jnp.concatenate([ref[b] for b in bs], axis=1)
    y_t = [jnp.concatenate([y_ref[0, pl.ds(b, SSM_WIDTH, stride=BATCH), :] for b in bs],
                           axis=1) for bs in tiles]
    yg = [0.5 * y * (1.0 + jnp.tanh(math.sqrt(2.0 / math.pi) * (y + 0.044715 * y ** 3)))
          for y in y_t]
    z_t = [_dot(wglu_ref[...], g.astype(BF16)) for g in yg]
    ssm_t = [g * _sigmoid(z) for g, z in zip(yg, z_t)]
    s_t = [_dot(wsb_ref[...], t.astype(BF16)) for t in ssm_t]
    attn = [_attn_finish(bs, pl.program_id(0), scores, vp_ref, vc_ref, vn_ref, sink_ref)
            for bs in tiles]
    a_t = [_dot(wab_ref[...], t) for t in attn]
    merged = [cat(ga_ref, bs).astype(F32) * a + cat(gs_ref, bs).astype(F32) * sb
              for bs, a, sb in zip(tiles, a_t, s_t)]
    for bs, m in zip(tiles, merged):
        mb = m.astype(BF16)
        for r in range(0, D_MODEL, MXU_DEPTH):
            o = _dot(wout_ref[r:r + MXU_DEPTH, :], mb).T
            o_ref[bs.start:bs.stop, :, r:r + MXU_DEPTH] = (
                x_ref[bs.start:bs.stop, :, r:r + MXU_DEPTH] + o.reshape(nb, CHUNK, MXU_DEPTH))


def _mix(q_t, k_t, v_t, sink_rows, y4, ga_t, gs_t, x, wglu_t, wab_t, wsb_t, wout_t):
    prev = lambda c: (jnp.maximum(c - 1, 0), 0, 0, 0)
    cur = lambda c: (c, 0, 0, 0)
    nxt = lambda c: (jnp.minimum(c + 1, N_CHUNKS - 1), 0, 0, 0)
    tspec = lambda rows, im=cur: pl.BlockSpec((1, BATCH, rows, CHUNK), im)
    wspec = lambda r, k: pl.BlockSpec((r, k), lambda c: (0, 0), pipeline_mode=pl.Buffered(1))
    xspec = pl.BlockSpec((BATCH, CHUNK, D_MODEL), lambda c: (0, c, 0))
    return pl.pallas_call(
        _mix_kernel,
        grid=(N_CHUNKS,),
        in_specs=[tspec(ATTN_WIDTH),
                  tspec(KV_WIDTH, prev), tspec(KV_WIDTH), tspec(KV_WIDTH, nxt),
                  tspec(KV_WIDTH, prev), tspec(KV_WIDTH), tspec(KV_WIDTH, nxt),
                  pl.BlockSpec((N_KV_HEADS, 1, GQA_GROUP * CHUNK), lambda c: (0, 0, 0)),
                  pl.BlockSpec((1, SSM_WIDTH * BATCH, CHUNK), lambda c: (c, 0, 0)),
                  tspec(D_MODEL), tspec(D_MODEL), xspec,
                  wspec(SSM_WIDTH, SSM_WIDTH), wspec(D_MODEL, ATTN_WIDTH),
                  wspec(D_MODEL, SSM_WIDTH), wspec(D_MODEL, D_MODEL)],
        out_specs=xspec,
        out_shape=jax.ShapeDtypeStruct((BATCH, SEQ, D_MODEL), F32),
        compiler_params=pltpu.CompilerParams(vmem_limit_bytes=VMEM_LIMIT),
        name="mix",
    )(q_t, k_t, k_t, k_t, v_t, v_t, v_t, sink_rows, y4, ga_t, gs_t, x,
      wglu_t, wab_t, wsb_t, wout_t)


def _ffn_kernel(x_ref, g2_ref, wg_ref, wu_ref, wd_ref, gf_ref, o_ref):
    tile = FFN_ROWS // FFN_SPLIT
    rows = [slice(t * tile, (t + 1) * tile) for t in range(FFN_SPLIT)]
    h = [_rmsnorm(x_ref[r, :], g2_ref[...]).astype(BF16) for r in rows]
    acc = [jnp.zeros((tile, D_MODEL), F32) for _ in rows]
    for j in range(D_FF // FFN_COLS):
        cols = slice(j * FFN_COLS, (j + 1) * FFN_COLS)
        for t in range(FFN_SPLIT):
            gate = _dot(h[t], wg_ref[:, cols])
            up = _dot(h[t], wu_ref[:, cols])
            act = (gate * _sigmoid(gate) * up).astype(BF16)
            acc[t] = acc[t] + _dot(act, wd_ref[cols, :])
    for t, r in enumerate(rows):
        o_ref[r, :] = _rmsnorm(x_ref[r, :] + acc[t], gf_ref[...])


def _ffn(x, g2, wg, wu, wd, gf):
    n_tok = BATCH * SEQ
    row = pl.BlockSpec((FFN_ROWS, D_MODEL), lambda i: (i, 0))
    vec = pl.BlockSpec((1, D_MODEL), lambda i: (0, 0))
    wspec = lambda r, k: pl.BlockSpec((r, k), lambda i: (0, 0), pipeline_mode=pl.Buffered(1))
    return pl.pallas_call(
        _ffn_kernel,
        grid=(n_tok // FFN_ROWS,),
        in_specs=[row, vec, wspec(D_MODEL, D_FF), wspec(D_MODEL, D_FF),
                  wspec(D_FF, D_MODEL), vec],
        out_specs=row,
        out_shape=jax.ShapeDtypeStruct((n_tok, D_MODEL), F32),
        compiler_params=pltpu.CompilerParams(vmem_limit_bytes=VMEM_LIMIT),
        name="ffn",
    )(x, g2, wg, wu, wd, gf)


def _ssm_param_layouts(lam_re, lam_im, log_dt, b_re, b_im, c_re, c_im, d):
    both = lambda t: jnp.concatenate([t[0], t[1]], axis=-1)
    ldt = jnp.broadcast_to(log_dt[:, :, None], (2, SSM_GROUPS, SSM_STATE))
    rows = jnp.stack([both(lam_re), both(lam_im), both(ldt)], axis=1)
    rowp = jnp.pad(rows, ((0, 0), (0, SUBLANES - 3), (0, 0)))
    bt =jnp.concatenate([both(jnp.swapaxes(b_re, 2, 3)),
                          both(jnp.swapaxes(b_im, 2, 3))], axis=1)
    cw = jnp.concatenate([jnp.tile(c_re, (1, 1, 2)), jnp.tile(c_im, (1, 1, 2))], axis=1)
    ct = jnp.swapaxes(cw, 1, 2)
    drow = jnp.repeat(d, CHUNK, axis=-1)[:, None, :]
    return rowp, bt, cw, ct, drow


def kernel(x, norm1_g, w_in, attn_sink, ssm_lambda_re, ssm_lambda_im, ssm_log_dt, ssm_b_re, ssm_b_im, ssm_c_re, ssm_c_im, ssm_d, w_glu, w_attn_branch, w_ssm_branch, w_out, norm2_g, w_ffn_gate, w_ffn_up, w_ffn_down, norm_f_g):
    pos = jnp.arange(SEQ, dtype=F32)
    inv_freq = ROPE_THETA ** (-jnp.arange(0, ROPE_DIM, 2, dtype=F32) / ROPE_DIM)
    ang = inv_freq[:, None] * pos[None, :]
    cos_t, sin_t = jnp.cos(ang), jnp.sin(ang)
    sink_rows = jnp.repeat(attn_sink[0], CHUNK).reshape(
        N_KV_HEADS, 1, GQA_GROUP * CHUNK)

    rowp, bt, cw, ct, drow = _ssm_param_layouts(
        ssm_lambda_re[0], ssm_lambda_im[0], ssm_log_dt[0], ssm_b_re[0], ssm_b_im[0],
        ssm_c_re[0], ssm_c_im[0], ssm_d[0])
    q_t, k_t, v_t, u4, ga_t, gs_t, *rest = _in_proj(
        x, norm1_g[0][None, :], w_in[0].T.astype(BF16), cos_t, sin_t, rowp, bt, cw, ct,
        w_ffn_gate[0], w_ffn_up[0], w_ffn_down[0])
    tables, (wg, wu, wd) = rest[:len(_TABLES)], rest[len(_TABLES):]
    y4 = _ssm(u4, tables, drow)
    x1 = _mix(q_t, k_t, v_t, sink_rows, y4, ga_t, gs_t, x,
              w_glu[0].T.astype(BF16), w_attn_branch[0].T.astype(BF16),
              w_ssm_branch[0].T.astype(BF16), w_out[0].T.astype(BF16))
    out = _ffn(x1.reshape(BATCH * SEQ, D_MODEL), norm2_g[0][None, :], wg, wu, wd,
               norm_f_g[None, :])
    return out.reshape(BATCH, SEQ, D_MODEL)
```

```python
import math

import jax
import jax.numpy as jnp
from jax import lax
from jax.experimental import pallas as pl
from jax.experimental.pallas import tpu as pltpu

D_MODEL = 1024
BATCH = 8
SEQ = 4096
N_HEADS = 8
N_KV_HEADS = 2
HEAD_DIM = 64
GQA_GROUP = N_HEADS // N_KV_HEADS
ATTN_WIDTH = N_HEADS * HEAD_DIM
KV_WIDTH = N_KV_HEADS * HEAD_DIM
ROPE_DIM = HEAD_DIM // 4
ROPE_HALF = ROPE_DIM // 2
ROPE_THETA = 500000.0
WINDOW = 128
SSM_GROUP_CH = 16
SSM_WIDTH = D_MODEL // 2
SSM_GROUPS = SSM_WIDTH // SSM_GROUP_CH
SSM_STATE = 64
D_FF = 2816
IN_COLS = ATTN_WIDTH + 2 * KV_WIDTH + SSM_WIDTH + 2 * D_MODEL
RMS_EPS = 1e-6
NEG_INF = -1e30

LANES = 128
SUBLANES = 8
MXU_DEPTH = 256
CHUNK = LANES
LOG2_CHUNK = CHUNK.bit_length() - 1
N_CHUNKS = SEQ // CHUNK
ROWS = N_CHUNKS * BATCH
FLAT = SSM_GROUP_CH * CHUNK
SSM_STEP_GROUPS = 4
MERGE_SPLIT = 4
LOG2E = math.log2(math.e)
SUM_ROWS = 16
TILE_HEADS = 2
FFN_ROWS = 1024
FFN_SPLIT = 2
FFN_COLS = 256
VMEM_LIMIT = 58 * 1024 * 1024

assert BATCH == SUBLANES and WINDOW == CHUNK and D_FF % FFN_COLS == 0

_Q0 = 0
_K0 = _Q0 + ATTN_WIDTH
_V0 = _K0 + KV_WIDTH
_U0 = _V0 + KV_WIDTH
_GA0 = _U0 + SSM_WIDTH
_GS0 = _GA0 + D_MODEL
assert _GS0 + D_MODEL == IN_COLS

BF16 = jnp.bfloat16
F32 = jnp.float32


def _sigmoid(x):
    return 0.5 * jnp.tanh(0.5 * x) + 0.5


def _rmsnorm(x, g):
    return x * lax.rsqrt(jnp.mean(x * x, axis=-1, keepdims=True) + RMS_EPS) * g


def _dot(a, b):
    return jnp.dot(a, b, preferred_element_type=F32)


def _in_proj_kernel(x_ref, g_ref, w_ref, cos_ref, sin_ref, rowp_ref, bt_ref, cw_ref, ct_ref,
                    wg32_ref, wu32_ref, wd32_ref,
                    q_ref, k_ref, v_ref, u_ref, ga_ref, gs_ref,
                    wf_ref, wb_ref, in_ref, out_ref, lc_ref, wg_ref, wu_ref, wd_ref):
    for src, dst in ((wg32_ref, wg_ref), (wu32_ref, wu_ref), (wd32_ref, wd_ref)):
        dst[...] = src[...].astype(BF16)

    q_ref, k_ref, v_ref, ga_ref, gs_ref = (r.at[0] for r in (q_ref, k_ref, v_ref, ga_ref, gs_ref))
    h = _rmsnorm(x_ref[...].reshape(BATCH * CHUNK, D_MODEL), g_ref[...]).astype(BF16)

    def proj_t(r0, r1):
        return lax.dot_general(w_ref[r0:r1, :], h, (((1,), (1,)), ((), ())),
                               preferred_element_type=F32)

    cos = jnp.concatenate([cos_ref[...]] * BATCH, axis=1)
    sin = jnp.concatenate([sin_ref[...]] * BATCH, axis=1)

    def rope_t(t, n_heads):
        parts = []
        for hd in range(n_heads):
            r1 = t[hd * HEAD_DIM:hd * HEAD_DIM + ROPE_HALF]
            r2 = t[hd * HEAD_DIM + ROPE_HALF:hd * HEAD_DIM + ROPE_DIM]
            parts += [r1 * cos - r2 * sin, r2 * cos + r1 * sin,
                      t[hd * HEAD_DIM + ROPE_DIM:(hd + 1) * HEAD_DIM]]
        return jnp.concatenate(parts, axis=0)

    def put(ref, r0, t):
        for b in range(BATCH):
            ref[b, r0:r0 + t.shape[0], :] = t[:, b * CHUNK:(b + 1) * CHUNK].astype(ref.dtype)

    slab = MXU_DEPTH
    for r in range(0, ATTN_WIDTH, slab):
        t = proj_t(_Q0 + r, _Q0 + r + slab) * (HEAD_DIM ** -0.5 * LOG2E)
        put(q_ref, r, rope_t(t, slab // HEAD_DIM))
    kv = proj_t(_K0, _U0)
    put(k_ref, 0, rope_t(kv[:KV_WIDTH], N_KV_HEADS))
    put(v_ref, 0, kv[KV_WIDTH:])
    for r in range(0, SSM_WIDTH, slab):
        u_t = proj_t(_U0 + r, _U0 + r + slab)
        for b in range(BATCH):
            u_ref[0, pl.ds(r * BATCH + b, slab, stride=BATCH), :] = (
                u_t[:, b * CHUNK:(b + 1) * CHUNK])
    for r in range(0, D_MODEL, slab):
        put(ga_ref, r, _sigmoid(proj_t(_GA0 + r, _GA0 + r + slab)))

    lc_r, lc_i = _ssm_tables(0, rowp_ref, bt_ref, cw_ref, ct_ref, wf_ref, wb_ref, in_ref, out_ref)
    lc_ref[0] = jnp.concatenate([lc_r, lc_i, jnp.zeros((SUBLANES - 2, LANES), F32)], axis=0)

    for r in range(0, D_MODEL, slab):
        put(gs_ref, r, _sigmoid(proj_t(_GS0 + r, _GS0 + r + slab)))


_PAIRS = SSM_GROUP_CH * SSM_GROUP_CH
_TABLES = [((_PAIRS, CHUNK), jnp.int32), ((_PAIRS, CHUNK), jnp.int32),
           ((FLAT, 2 * LANES), BF16), ((2 * LANES, FLAT), BF16), ((SUBLANES, LANES), F32)]
_PARAMS = [(SUBLANES, LANES), (2 * SSM_GROUP_CH, LANES), (2 * SSM_GROUP_CH, LANES),
           (LANES, 2 * SSM_GROUP_CH)]


def _group_spec(shape, n=1):
    return pl.BlockSpec((n,) + shape, lambda g: (g, 0, 0))


def _row_blocks(shape, rows):
    n = shape[0] // rows
    assert shape[0] % rows == 0 and n <= N_CHUNKS and rows % (2 * SUBLANES) == 0
    return pl.BlockSpec((rows, shape[1]), lambda c: (jnp.minimum(c, n - 1), 0))


def _in_proj(x, g1, w_t, cos_t, sin_t, rowp, bt, cw, ct, wg, wu, wd):
    assert N_CHUNKS == SSM_GROUPS
    tok = lambda rows, dt: jax.ShapeDtypeStruct((N_CHUNKS, BATCH, rows, CHUNK), dt)
    tspec = lambda rows: pl.BlockSpec((1, BATCH, rows, CHUNK), lambda c: (c, 0, 0, 0))
    casts = [_row_blocks(wg.shape, D_MODEL // N_CHUNKS), _row_blocks(wu.shape, D_MODEL // N_CHUNKS),
             _row_blocks(wd.shape, CHUNK)]
    return pl.pallas_call(
        _in_proj_kernel,
        grid=(N_CHUNKS,),
        in_specs=[
            pl.BlockSpec((BATCH, CHUNK, D_MODEL), lambda c: (0, c, 0)),
            pl.BlockSpec((1, D_MODEL), lambda c: (0, 0)),
            pl.BlockSpec((IN_COLS, D_MODEL), lambda c: (0, 0), pipeline_mode=pl.Buffered(1)),
            pl.BlockSpec((ROPE_HALF, CHUNK), lambda c: (0, c)),
            pl.BlockSpec((ROPE_HALF, CHUNK), lambda c: (0, c)),
        ] + [_group_spec(s) for s in _PARAMS] + casts,
        out_specs=[
            tspec(ATTN_WIDTH), tspec(KV_WIDTH), tspec(KV_WIDTH),
            pl.BlockSpec((1, SSM_WIDTH * BATCH, CHUNK), lambda c: (c, 0, 0)),
            tspec(D_MODEL), tspec(D_MODEL),
        ] + [_group_spec(s) for s, _ in _TABLES] + casts,
        out_shape=[
            tok(ATTN_WIDTH, BF16), tok(KV_WIDTH, BF16), tok(KV_WIDTH, BF16),
            jax.ShapeDtypeStruct((N_CHUNKS, SSM_WIDTH * BATCH, CHUNK), F32),
            tok(D_MODEL, BF16), tok(D_MODEL, BF16),
        ] + [jax.ShapeDtypeStruct((SSM_GROUPS,) + s, dt) for s, dt in _TABLES]
        + [jax.ShapeDtypeStruct(w.shape, BF16) for w in (wg, wu, wd)],
        compiler_params=pltpu.CompilerParams(vmem_limit_bytes=VMEM_LIMIT),
        name="in_proj",
    )(x, g1, w_t, cos_t, sin_t, rowp, bt, cw, ct, wg, wu, wd)


def _cmul(a_r, a_i, b_r, b_i):
    return a_r * b_r - a_i * b_i, a_r * b_i + a_i * b_r


def _ssm_kernel(u_ref, wf_ref, wb_ref, in_ref, out_ref, lc_ref, drow_ref, y_ref):
    groups = range(SSM_STEP_GROUPS)
    data = [_ssm_states(g, (lc_ref[g, 0:1, :], lc_ref[g, 1:2, :]), u_ref, in_ref) for g in groups]
    for g in groups:
        _ssm_outputs(g, *data[g], drow_ref, y_ref, wf_ref, wb_ref, out_ref)


def _ssm_tables(g, rowp_ref, bt_ref, cw_ref, ct_ref, wf_ref, wb_ref, in_ref, out_ref):
    half = SSM_STATE
    top_col = lax.broadcasted_iota(jnp.int32, (LANES, 1), 0) < half

    lam_r, lam_i = rowp_ref[g, 0:1, :], rowp_ref[g, 1:2, :]
    dt = jnp.exp(rowp_ref[g, 2:3, :])
    z_r, z_i = lam_r * dt, lam_i * dt
    e1 = jnp.exp(z_r)
    lb_r, lb_i = e1 * jnp.cos(z_i), e1 * jnp.sin(z_i)
    den = lam_r * lam_r + lam_i * lam_i
    cf_r = ((lb_r - 1.0) * lam_r + lb_i * lam_i) / den
    cf_i = (lb_i * lam_r - (lb_r - 1.0) * lam_i) / den
    b_r, b_i = bt_ref[g, 0:SSM_GROUP_CH, :], bt_ref[g, SSM_GROUP_CH:, :]
    bb_r = cf_r * b_r - cf_i * b_i
    bb_i = cf_r * b_i + cf_i * b_r
    c_r, c_i = cw_ref[g, 0:SSM_GROUP_CH, :], cw_ref[g, SSM_GROUP_CH:, :]

    sq_r, sq_i = [lb_r], [lb_i]
    for _ in range(LOG2_CHUNK):
        r, i = _cmul(sq_r[-1], sq_i[-1], sq_r[-1], sq_i[-1])
        sq_r.append(r)
        sq_i.append(i)
    lc_r, lc_i = sq_r[LOG2_CHUNK], sq_i[LOG2_CHUNK]
    pad = jnp.zeros((LANES - 2 * SUBLANES, LANES), F32)
    cols = jnp.concatenate(sq_r[:SUBLANES] + sq_i[:SUBLANES] + [pad], axis=0).T
    sq_rc = [cols[:, k:k + 1] for k in range(LOG2_CHUNK)]
    sq_ic = [cols[:, SUBLANES + k:SUBLANES + k + 1] for k in range(LOG2_CHUNK)]

    j_idx = lax.broadcasted_iota(jnp.int32, (LANES, CHUNK), 1)
    one, zero = jnp.ones((LANES, CHUNK), F32), jnp.zeros((LANES, CHUNK), F32)
    asc_r, asc_i, dsc_r, dsc_i = one, zero, one, zero
    for k in range(LOG2_CHUNK):
        bit = (j_idx & (1 << k)) != 0
        n_r, n_i = _cmul(asc_r, asc_i, sq_rc[k], sq_ic[k])
        asc_r, asc_i = jnp.where(bit, n_r, asc_r), jnp.where(bit, n_i, asc_i)
        n_r, n_i = _cmul(dsc_r, dsc_i, sq_rc[k], sq_ic[k])
        dsc_r, dsc_i = jnp.where(bit, dsc_r, n_r), jnp.where(bit, dsc_i, n_i)
    q_r, q_i = _cmul(jnp.where(top_col, asc_r, dsc_r), jnp.where(top_col, asc_i, dsc_i),
                     sq_rc[0], sq_ic[0])
    p_r = jnp.where(top_col, dsc_r, asc_r).T
    p_i = jnp.where(top_col, dsc_i, asc_i).T

    first = j_idx == 0
    ef_r = jnp.where(top_col, asc_r, jnp.where(first, 1.0, 0.0))
    ef_i = jnp.where(top_col, asc_i, 0.0)
    eb_r = jnp.where(top_col, 0.0, jnp.where(first, 0.0, q_r))
    eb_i = jnp.where(top_col, 0.0, jnp.where(first, 0.0, q_i))
    e_r = jnp.concatenate([eb_r, ef_r], axis=1)
    e_i = jnp.concatenate([eb_i, ef_i], axis=1)
    cb_r = jnp.concatenate([c_r * bb_r[h:h + 1] - c_i * bb_i[h:h + 1]
                            for h in range(SSM_GROUP_CH)], axis=0)
    cb_i = jnp.concatenate([c_r * bb_i[h:h + 1] + c_i * bb_r[h:h + 1]
                            for h in range(SSM_GROUP_CH)], axis=0)
    split = lambda t: (t.astype(BF16), (t - t.astype(BF16).astype(F32)).astype(BF16))
    (cr_h, cr_l), (ci_h, ci_l) = split(cb_r), split(-cb_i)
    (er_h, er_l), (ei_h, ei_l) = split(e_r), split(e_i)
    ktab = _dot(jnp.concatenate([cr_h, cr_h, cr_l, ci_h, ci_h, ci_l], axis=1),
                jnp.concatenate([er_h, er_l, er_h, ei_h, ei_l, ei_h], axis=0))
    kb, kf = ktab[:, :CHUNK], ktab[:, CHUNK:]
    kb1 = pltpu.roll(kb, 1, 1)
    kf1 = jnp.where(lax.broadcasted_iota(jnp.int32, (1, CHUNK), 1) == 0, kb1,
                    pltpu.roll(kf, 1, 1))
    bf16_bits = lambda t: lax.bitcast_convert_type(t.astype(BF16).astype(F32), jnp.int32)
    pack = lambda even, odd: bf16_bits(odd) | lax.shift_right_logical(bf16_bits(even), 16)
    wf_ref[g] = pack(kf, kf1)
    wb_ref[g] = pack(kb, kb1)

    for h in range(SSM_GROUP_CH):
        rows = slice(h * CHUNK, (h + 1) * CHUNK)
        in_ref[g, rows, 0:LANES] = (p_r * bb_r[h:h + 1] - p_i * bb_i[h:h + 1]).astype(BF16)
        in_ref[g, rows, LANES:] = (p_r * bb_i[h:h + 1] + p_i * bb_r[h:h + 1]).astype(BF16)

    for hp in range(SSM_GROUP_CH):
        cols = slice(hp * CHUNK, (hp + 1) * CHUNK)
        cc_r = ct_ref[g, :, hp:hp + 1]
        cc_i = ct_ref[g, :, SSM_GROUP_CH + hp:SSM_GROUP_CH + hp + 1]
        out_ref[g, 0:LANES, cols] = (cc_r * q_r - cc_i * q_i).astype(BF16)
        out_ref[g, LANES:, cols] = (-(cc_r * q_i + cc_i * q_r)).astype(BF16)
    return lc_r, lc_i


def _ssm_states(g, lam_chunk, u_ref, in_ref):
    lc_r, lc_i = lam_chunk
    lo_row = lax.broadcasted_iota(jnp.int32, (1, LANES), 1) < SSM_STATE
    row0 = g * SSM_GROUP_CH * BATCH
    a = jnp.concatenate(
        [u_ref[:, row0 + h * BATCH:row0 + (h + 1) * BATCH, :].reshape(ROWS, CHUNK)
         for h in range(SSM_GROUP_CH)], axis=1)
    ab = a.astype(BF16)
    e = _dot(ab, in_ref[g])
    s_r = jnp.zeros((BATCH, LANES), F32)
    s_i = jnp.zeros((BATCH, LANES), F32)
    hist_r, hist_i = [], []
    for i in range(N_CHUNKS):
        hist_r.append(s_r)
        hist_i.append(s_i)
        fr = slice(i * BATCH, (i + 1) * BATCH)
        br = slice((N_CHUNKS - 1 - i) * BATCH, (N_CHUNKS - i) * BATCH)
        in_r = jnp.where(lo_row, e[fr, 0:LANES], e[br, 0:LANES])
        in_i = jnp.where(lo_row, e[fr, LANES:], e[br, LANES:])
        s_r, s_i = (s_r * lc_r - s_i * lc_i + in_r, s_r * lc_i + s_i * lc_r + in_i)
    states = jnp.concatenate(
        [jnp.concatenate([jnp.where(lo_row, hist_r[c], hist_r[N_CHUNKS - 1 - c]),
                          jnp.where(lo_row, hist_i[c], hist_i[N_CHUNKS - 1 - c])], axis=1)
         for c in range(N_CHUNKS)], axis=0).astype(BF16)
    return a, ab, states


def _ssm_outputs(g, a, ab, states, drow_ref, y_ref, wf_ref, wb_ref, out_ref):
    row0 = g * SSM_GROUP_CH * BATCH

    pair = lax.broadcasted_iota(jnp.int32, (CHUNK // 2, CHUNK), 0)
    lane = lax.broadcasted_iota(jnp.int32, (CHUNK // 2, CHUNK), 1)
    upper = lane + 2 * pair < CHUNK
    per_slab = MXU_DEPTH // CHUNK

    def toeplitz_slab(slab):
        col_blocks = []
        for h in range(SSM_GROUP_CH):
            blocks = []
            for hp in range(slab * per_slab, (slab + 1) * per_slab):
                row = pl.ds(h * SSM_GROUP_CH + hp, 1)
                w = jnp.where(upper,
                              jnp.broadcast_to(wf_ref[g, row, :], (CHUNK // 2, CHUNK)),
                              jnp.broadcast_to(wb_ref[g, row, :], (CHUNK // 2, CHUNK)))
                w = pltpu.roll(w, 0, 1, stride=2, stride_axis=0)
                blocks.append(pltpu.bitcast(w, BF16))
            col_blocks.append(jnp.concatenate(blocks, axis=1))
        return jnp.concatenate(col_blocks, axis=0)

    for slab in range(SSM_GROUP_CH // per_slab):
        cols = slice(slab * MXU_DEPTH, (slab + 1) * MXU_DEPTH)
        y = (_dot(ab, toeplitz_slab(slab)) + _dot(states, out_ref[g, :, cols])
             + a[:, cols] * drow_ref[g][:, cols])
        for k in range(per_slab):
            hp = slab * per_slab + k
            y_ref[:, row0 + hp * BATCH:row0 + (hp + 1) * BATCH, :] = (
                y[:, k * CHUNK:(k + 1) * CHUNK].reshape(N_CHUNKS, BATCH, CHUNK))


def _ssm(u4, tables, drow):
    ng = SSM_STEP_GROUPS
    data = pl.BlockSpec((N_CHUNKS, ng * SSM_GROUP_CH * BATCH, CHUNK), lambda g: (0, g, 0))
    return pl.pallas_call(
        _ssm_kernel,
        grid=(SSM_GROUPS // ng,),
        in_specs=[data] + [_group_spec(s, ng) for s, _ in _TABLES] + [_group_spec((1, FLAT), ng)],
        out_specs=data,
        out_shape=jax.ShapeDtypeStruct((N_CHUNKS, SSM_WIDTH * BATCH, CHUNK), F32),
        compiler_params=pltpu.CompilerParams(vmem_limit_bytes=VMEM_LIMIT),
        name="ssm",
    )(u4, *tables, drow)


_ATTN_CASES = [(b, j, t) for b in range(BATCH) for j in range(N_KV_HEADS)
               for t in range(GQA_GROUP // TILE_HEADS)]


def _attn_scores(q_ref, kp_ref, kc_ref, kn_ref):
    zeros = jnp.zeros((HEAD_DIM, TILE_HEADS * CHUNK), BF16)
    keys = []
    for b in range(BATCH):
        k_t = jnp.concatenate([kp_ref[b], kc_ref[b], kn_ref[b]], axis=1)
        keys.append(k_t.astype(F32).T.astype(BF16))
    scores = []
    for b, j, t in _ATTN_CASES:
        first = j * GQA_GROUP + t * TILE_HEADS
        q_t = jnp.concatenate(
            [q_ref[b, hd * HEAD_DIM:(hd + 1) * HEAD_DIM, :]
             for hd in range(first, first + TILE_HEADS)], axis=1)
        q_pad = jnp.concatenate([q_t, zeros] if j == 0 else [zeros, q_t], axis=0)
        scores.append(_dot(keys[b], q_pad))
    return scores


def _attn_finish(batches, c, scores, vp_ref, vc_ref, vn_ref, sink_ref):
    kk = lax.broadcasted_iota(jnp.int32, (CHUNK, CHUNK), 0)
    qq = lax.broadcasted_iota(jnp.int32, (CHUNK, CHUNK), 1)
    prev_ok = jnp.concatenate([kk >= qq] * TILE_HEADS, axis=1) & (c > 0)
    next_ok = jnp.concatenate([kk <= qq] * TILE_HEADS, axis=1) & (c < N_CHUNKS - 1)
    ones = jnp.ones((SUM_ROWS, 3 * CHUNK), BF16)
    width = TILE_HEADS * CHUNK
    heads = {}
    for b, j, t in [case for case in _ATTN_CASES if case[0] in batches]:
        s = scores[_ATTN_CASES.index((b, j, t))]
        v_t = jnp.concatenate([vp_ref[b], vc_ref[b], vn_ref[b]], axis=1)
        s = jnp.concatenate([jnp.where(prev_ok, s[:CHUNK], NEG_INF), s[CHUNK:2 * CHUNK],
                             jnp.where(next_ok, s[2 * CHUNK:], NEG_INF)], axis=0)
        sink = sink_ref[j][:, t * width:(t + 1) * width] * LOG2E
        m = jnp.maximum(jnp.max(s, axis=0, keepdims=True), sink)
        p = jnp.exp2(s - m).astype(BF16)
        v_aug = jnp.concatenate([v_t[j * HEAD_DIM:(j + 1) * HEAD_DIM], ones], axis=0)
        o_aug = _dot(v_aug, p)
        o = o_aug[:HEAD_DIM] / (o_aug[HEAD_DIM:HEAD_DIM + 1] + jnp.exp2(sink - m))
        for g in range(TILE_HEADS):
            heads[b, j * GQA_GROUP + t * TILE_HEADS + g] = (
                o[:, g * CHUNK:(g + 1) * CHUNK].astype(BF16))
    return jnp.concatenate(
        [jnp.concatenate([heads[b, hd] for hd in range(N_HEADS)], axis=0) for b in batches],
        axis=1)


def _mix_kernel(q_ref, kp_ref, kc_ref, kn_ref, vp_ref, vc_ref, vn_ref, sink_ref,
                y_ref, ga_ref, gs_ref, x_ref, wglu_ref, wab_ref, wsb_ref, wout_ref, o_ref):
    q_ref, kp_ref, kc_ref, kn_ref, vp_ref, vc_ref, vn_ref, ga_ref, gs_ref = (
        r.at[0] for r in (q_ref, kp_ref, kc_ref, kn_ref, vp_ref, vc_ref, vn_ref, ga_ref, gs_ref))
    scores = _attn_scores(q_ref, kp_ref, kc_ref, kn_ref)
    nb = BATCH // MERGE_SPLIT
    tiles = [range(t * nb, (t + 1) * nb) for t in range(MERGE_SPLIT)]
    cat = lambda ref, bs: jnp.concatenate([ref[b] for b in bs], axis=1)
    y_t = [jnp.concatenate([y_ref[0, pl.ds(b, SSM_WIDTH, stride=BATCH), :] for b in bs],
                           axis=1) for bs in tiles]
    yg = [0.5 * y * (1.0 + jnp.tanh(math.sqrt(2.0 / math.pi) * (y + 0.044715 * y ** 3)))
          for y in y_t]
    z_t = [_dot(wglu_ref[...], g.astype(BF16)) for g in yg]
    ssm_t = [g * _sigmoid(z) for g, z in zip(yg, z_t)]
    s_t = [_dot(wsb_ref[...], t.astype(BF16)) for t in ssm_t]
    attn = [_attn_finish(bs, pl.program_id(0), scores, vp_ref, vc_ref, vn_ref, sink_ref)
            for bs in tiles]
    a_t = [_dot(wab_ref[...], t) for t in attn]
    merged = [cat(ga_ref, bs).astype(F32) * a + cat(gs_ref, bs).astype(F32) * sb
              for bs, a, sb in zip(tiles, a_t, s_t)]
    for bs, m in zip(tiles, merged):
        mb = m.astype(BF16)
        for r in range(0, D_MODEL, MXU_DEPTH):
            o = _dot(wout_ref[r:r + MXU_DEPTH, :], mb).T
            o_ref[bs.start:bs.stop, :, r:r + MXU_DEPTH] = (
                x_ref[bs.start:bs.stop, :, r:r + MXU_DEPTH] + o.reshape(nb, CHUNK, MXU_DEPTH))


def _mix(q_t, k_t, v_t, sink_rows, y4, ga_t, gs_t, x, wglu_t, wab_t, wsb_t, wout_t):
    prev = lambda c: (jnp.maximum(c - 1, 0), 0, 0, 0)
    cur = lambda c: (c, 0, 0, 0)
    nxt = lambda c: (jnp.minimum(c + 1, N_CHUNKS - 1), 0, 0, 0)
    tspec = lambda rows, im=cur: pl.BlockSpec((1, BATCH, rows, CHUNK), im)
    wspec = lambda r, k: pl.BlockSpec((r, k), lambda c: (0, 0), pipeline_mode=pl.Buffered(1))
    xspec = pl.BlockSpec((BATCH, CHUNK, D_MODEL), lambda c: (0, c, 0))
    return pl.pallas_call(
        _mix_kernel,
        grid=(N_CHUNKS,),
        in_specs=[tspec(ATTN_WIDTH),
                  tspec(KV_WIDTH, prev), tspec(KV_WIDTH), tspec(KV_WIDTH, nxt),
                  tspec(KV_WIDTH, prev), tspec(KV_WIDTH), tspec(KV_WIDTH, nxt),
                  pl.BlockSpec((N_KV_HEADS, 1, GQA_GROUP * CHUNK), lambda c: (0, 0, 0)),
                  pl.BlockSpec((1, SSM_WIDTH * BATCH, CHUNK), lambda c: (c, 0, 0)),
                  tspec(D_MODEL), tspec(D_MODEL), xspec,
                  wspec(SSM_WIDTH, SSM_WIDTH), wspec(D_MODEL, ATTN_WIDTH),
                  wspec(D_MODEL, SSM_WIDTH), wspec(D_MODEL, D_MODEL)],
        out_specs=xspec,
        out_shape=jax.ShapeDtypeStruct((BATCH, SEQ, D_MODEL), F32),
        compiler_params=pltpu.CompilerParams(vmem_limit_bytes=VMEM_LIMIT),
        name="mix",
    )(q_t, k_t, k_t, k_t, v_t, v_t, v_t, sink_rows, y4, ga_t, gs_t, x,
      wglu_t, wab_t, wsb_t, wout_t)


def _ffn_kernel(x_ref, g2_ref, wg_ref, wu_ref, wd_ref, gf_ref, o_ref):
    tile = FFN_ROWS // FFN_SPLIT
    rows = [slice(t * tile, (t + 1) * tile) for t in range(FFN_SPLIT)]
    h = [_rmsnorm(x_ref[r, :], g2_ref[...]).astype(BF16) for r in rows]
    acc = [jnp.zeros((tile, D_MODEL), F32) for _ in rows]
    for j in range(D_FF // FFN_COLS):
        cols = slice(j * FFN_COLS, (j + 1) * FFN_COLS)
        for t in range(FFN_SPLIT):
            gate = _dot(h[t], wg_ref[:, cols])
            up = _dot(h[t], wu_ref[:, cols])
            act = (gate * _sigmoid(gate) * up).astype(BF16)
            acc[t] = acc[t] + _dot(act, wd_ref[cols, :])
    for t, r in enumerate(rows):
        o_ref[r, :] = _rmsnorm(x_ref[r, :] + acc[t], gf_ref[...])


def _ffn(x, g2, wg, wu, wd, gf):
    n_tok = BATCH * SEQ
    row = pl.BlockSpec((FFN_ROWS, D_MODEL), lambda i: (i, 0))
    vec = pl.BlockSpec((1, D_MODEL), lambda i: (0, 0))
    wspec = lambda r, k: pl.BlockSpec((r, k), lambda i: (0, 0), pipeline_mode=pl.Buffered(1))
    return pl.pallas_call(
        _ffn_kernel,
        grid=(n_tok // FFN_ROWS,),
        in_specs=[row, vec, wspec(D_MODEL, D_FF), wspec(D_MODEL, D_FF),
                  wspec(D_FF, D_MODEL), vec],
        out_specs=row,
        out_shape=jax.ShapeDtypeStruct((n_tok, D_MODEL), F32),
        compiler_params=pltpu.CompilerParams(vmem_limit_bytes=VMEM_LIMIT),
        name="ffn",
    )(x, g2, wg, wu, wd, gf)


def _ssm_param_layouts(lam_re, lam_im, log_dt, b_re, b_im, c_re, c_im, d):
    both = lambda t: jnp.concatenate([t[0], t[1]], axis=-1)
    ldt = jnp.broadcast_to(log_dt[:, :, None], (2, SSM_GROUPS, SSM_STATE))
    rows = jnp.stack([both(lam_re), both(lam_im), both(ldt)], axis=1)
    rowp = jnp.pad(rows, ((0, 0), (0, SUBLANES - 3), (0, 0)))
    bt =jnp.concatenate([both(jnp.swapaxes(b_re, 2, 3)),
                          both(jnp.swapaxes(b_im, 2, 3))], axis=1)
    cw = jnp.concatenate([jnp.tile(c_re, (1, 1, 2)), jnp.tile(c_im, (1, 1, 2))], axis=1)
    ct = jnp.swapaxes(cw, 1, 2)
    drow = jnp.repeat(d, CHUNK, axis=-1)[:, None, :]
    return rowp, bt, cw, ct, drow


def kernel(x, norm1_g, w_in, attn_sink, ssm_lambda_re, ssm_lambda_im, ssm_log_dt, ssm_b_re, ssm_b_im, ssm_c_re, ssm_c_im, ssm_d, w_glu, w_attn_branch, w_ssm_branch, w_out, norm2_g, w_ffn_gate, w_ffn_up, w_ffn_down, norm_f_g):
    pos = jnp.arange(SEQ, dtype=F32)
    inv_freq = ROPE_THETA ** (-jnp.arange(0, ROPE_DIM, 2, dtype=F32) / ROPE_DIM)
    ang = inv_freq[:, None] * pos[None, :]
    cos_t, sin_t = jnp.cos(ang), jnp.sin(ang)
    sink_rows = jnp.repeat(attn_sink[0], CHUNK).reshape(
        N_KV_HEADS, 1, GQA_GROUP * CHUNK)

    rowp, bt, cw, ct, drow = _ssm_param_layouts(
        ssm_lambda_re[0], ssm_lambda_im[0], ssm_log_dt[0], ssm_b_re[0], ssm_b_im[0],
        ssm_c_re[0], ssm_c_im[0], ssm_d[0])
    q_t, k_t, v_t, u4, ga_t, gs_t, *rest = _in_proj(
        x, norm1_g[0][None, :], w_in[0].T.astype(BF16), cos_t, sin_t, rowp, bt, cw, ct,
        w_ffn_gate[0], w_ffn_up[0], w_ffn_down[0])
    tables, (wg, wu, wd) = rest[:len(_TABLES)], rest[len(_TABLES):]
    y4 = _ssm(u4, tables, drow)
    x1 = _mix(q_t, k_t, v_t, sink_rows, y4, ga_t, gs_t, x,
              w_glu[0].T.astype(BF16), w_attn_branch[0].T.astype(BF16),
              w_ssm_branch[0].T.astype(BF16), w_out[0].T.astype(BF16))
    out = _ffn(x1.reshape(BATCH * SEQ, D_MODEL), norm2_g[0][None, :], wg, wu, wd,
               norm_f_g[None, :])
    return out.reshape(BATCH, SEQ, D_MODEL)
```
